```python
import numpy as np
import jax
import jax.numpy as jnp
from jax import lax

D_MODEL = 2048
BATCH = 32
SEQ = 256
DEPTH = 1
DEC_BATCH = 8
DEC_SEQ = 2048
PAST_LEN = 512

HG_HEADS = 8
HG_DK = 128
HG_DV = 128
HG_KEY = HG_HEADS * HG_DK
HG_WIDTH = HG_HEADS * HG_DV
ML_HEADS = 8
ML_DQK = 64
ML_DV = 128
ML_QK = ML_HEADS * ML_DQK
ML_WIDTH = ML_HEADS * ML_DV
D_FF = -(-8 * D_MODEL // (3 * 256)) * 256
CHUNK = 64
EPS = 1e-6
IN_SPLITS = (HG_KEY, HG_KEY, HG_KEY, HG_WIDTH, HG_WIDTH,
             ML_QK, ML_QK, ML_WIDTH, 2 * ML_HEADS, 2 * ML_HEADS, ML_WIDTH,
             D_MODEL, D_MODEL)
IN_WIDTH = sum(IN_SPLITS)

kernel_name = 'hybrid_hgrn2_mlstm_diffusion_step'


def rms_norm(x, g):
    xf = x.astype(jnp.float32)
    y = xf * lax.rsqrt(jnp.mean(xf * xf, axis=-1, keepdims=True) + EPS)
    return (y * g.astype(jnp.float32)).astype(x.dtype)


def to_heads(a, n_heads):
    b, t, _ = a.shape
    return a.reshape(b, t, n_heads, -1).transpose(0, 2, 1, 3).astype(jnp.float32)


def from_heads(a):
    b, h, t, d = a.shape
    return a.transpose(0, 2, 1, 3).reshape(b, t, h * d)


def to_chunks(a):
    return a.reshape(a.shape[:2] + (a.shape[2] // CHUNK, CHUNK) + a.shape[3:])


def flip_t(a):
    return jnp.flip(a, axis=2)


def hgrn2_scan(q, k, v, log_f, s0):
    b_, h_, t_, dv = v.shape
    q, k, v, log_f = to_chunks(q), to_chunks(k), to_chunks(v), to_chunks(log_f)
    b = jnp.cumsum(log_f, axis=3)
    b_ref = b[:, :, :, CHUNK // 2:CHUNK // 2 + 1]
    b_last = b[:, :, :, -1:]
    causal = jnp.tril(jnp.ones((CHUNK, CHUNK), dtype=bool))
    att = jnp.einsum('bhnld,bhnsd->bhnls', q * jnp.exp(b - b_ref), k * jnp.exp(b_ref - b))
    att = jnp.where(causal, att, 0.0)
    o_intra = jnp.einsum('bhnls,bhnsv->bhnlv', att, v)
    ds = jnp.einsum('bhnsd,bhnsv->nbhdv', k * jnp.exp(b_last - b), v)
    decay = jnp.moveaxis(jnp.exp(b_last[:, :, :, 0]), 2, 0)

    def step(s, inp):
        dec, d_s = inp
        return dec[..., None] * s + d_s, s

    s_fin, s_start = lax.scan(step, s0.astype(jnp.float32), (decay, ds))
    o_inter = jnp.einsum('bhnld,nbhdv->bhnlv', q * jnp.exp(b), s_start)
    return (o_intra + o_inter).reshape(b_, h_, t_, dv), s_fin


def mlstm_scan(q, k, v, log_i, log_f, c0, n0, m0):
    b_, h_, t_, dv = v.shape
    q, k, v, log_i, log_f = (to_chunks(q), to_chunks(k), to_chunks(v),
                             to_chunks(log_i), to_chunks(log_f))
    g = jnp.cumsum(log_f, axis=-1)
    g_last = g[..., -1]
    causal = jnp.tril(jnp.ones((CHUNK, CHUNK), dtype=bool))
    log_d = jnp.where(causal, g[..., :, None] - g[..., None, :] + log_i[..., None, :], -jnp.inf)
    w_end = g_last[..., None] - g + log_i
    m_loc = jnp.max(w_end, axis=-1)
    wk = jnp.exp(w_end - m_loc[..., None])[..., None] * k
    c_loc = jnp.einsum('bhnld,bhnlv->nbhdv', wk, v)
    n_loc = jnp.moveaxis(jnp.sum(wk, axis=3), 2, 0)

    def step(carry, inp):
        c, n, m = carry
        a, ml, cl, nl = inp
        m_new = jnp.maximum(a + m, ml)
        s_old = jnp.exp(a + m - m_new)
        s_loc = jnp.exp(ml - m_new)
        c_new = s_old[..., None, None] * c + s_loc[..., None, None] * cl
        n_new = s_old[..., None] * n + s_loc[..., None] * nl
        return (c_new, n_new, m_new), (c, n, m)

    init = (c0.astype(jnp.float32), n0.astype(jnp.float32), m0.astype(jnp.float32))
    (c_fin, n_fin, m_fin), (c_st, n_st, m_st) = lax.scan(
        step, init, (jnp.moveaxis(g_last, 2, 0), jnp.moveaxis(m_loc, 2, 0), c_loc, n_loc))
    log_inter = g + jnp.moveaxis(m_st, 0, 2)[..., None]
    m_t = jnp.maximum(log_inter, jnp.max(log_d, axis=-1))
    s_inter = jnp.exp(log_inter - m_t)
    qk = jnp.einsum('bhnld,bhnsd->bhnls', q, k) * jnp.exp(log_d - m_t[..., None])
    num = (jnp.einsum('bhnls,bhnsv->bhnlv', qk, v)
           + s_inter[..., None] * jnp.einsum('bhnld,nbhdv->bhnlv', q, c_st))
    den = jnp.sum(qk, axis=-1) + s_inter * jnp.einsum('bhnld,nbhd->bhnl', q, n_st)
    h = num / jnp.maximum(jnp.abs(den), jnp.exp(-m_t))[..., None]
    return h.reshape(b_, h_, t_, dv), c_fin, n_fin, m_fin


def mixer(h, st, w_in, hg_lb, hg_norm_g, ml_b_i, ml_b_f, ml_norm_g, w_up_hg, w_up_ml, w_out):
    st_hg, st_c, st_n, st_m = st
    bsz, t_len = h.shape[0], h.shape[1]
    offsets = [int(o) for o in np.cumsum(IN_SPLITS)[:-1]]
    (hq, hf_fw, hf_bw, hi, hgate, mq, mk, mv, mi, mf, mo, ga, gb) = jnp.split(h @ w_in, offsets, axis=-1)

    q = jax.nn.silu(to_heads(hq, HG_HEADS))
    v = to_heads(hi, HG_HEADS)
    hg_out, hg_state = [], []
    for d, fz in enumerate((hf_fw, hf_bw)):
        lb = hg_lb[d].reshape(HG_HEADS, 1, HG_DK)
        z = to_heads(fz, HG_HEADS)
        log_f = jnp.log(lb + (1.0 - lb) * jax.nn.sigmoid(z))
        k = (1.0 - lb) * jax.nn.sigmoid(-z)
        if d == 0:
            o, s = hgrn2_scan(q, k, v, log_f, st_hg[:, d])
        else:
            o, s = hgrn2_scan(flip_t(q), flip_t(k), flip_t(v), flip_t(log_f), st_hg[:, d])
            o = flip_t(o)
        hg_out.append(o)
        hg_state.append(s)
    o = hg_out[0] + hg_out[1]
    o = (o * lax.rsqrt(jnp.mean(o * o, axis=-1, keepdims=True) + EPS)
         * hg_norm_g.astype(jnp.float32) * jax.nn.silu(to_heads(hgate, HG_HEADS)))
    y_hg = from_heads(o).astype(h.dtype) @ w_up_hg

    q = to_heads(mq, ML_HEADS)
    k = to_heads(mk, ML_HEADS) * (ML_DQK ** -0.5)
    v = to_heads(mv, ML_HEADS)
    log_i = (mi + ml_b_i).astype(jnp.float32).reshape(bsz, t_len, 2, ML_HEADS).transpose(2, 0, 3, 1)
    log_f = jax.nn.log_sigmoid((mf + ml_b_f).astype(jnp.float32)).reshape(
        bsz, t_len, 2, ML_HEADS).transpose(2, 0, 3, 1)
    ml_out, ml_c, ml_n, ml_m = [], [], [], []
    for d in range(2):
        if d == 0:
            o, cf, nf, mf_ = mlstm_scan(q, k, v, log_i[d], log_f[d], st_c[:, d], st_n[:, d], st_m[:, d])
        else:
            o, cf, nf, mf_ = mlstm_scan(flip_t(q), flip_t(k), flip_t(v), flip_t(log_i[d]), flip_t(log_f[d]),
                                        st_c[:, d], st_n[:, d], st_m[:, d])
            o = flip_t(o)
        ml_out.append(o)
        ml_c.append(cf)
        ml_n.append(nf)
        ml_m.append(mf_)
    o = ml_out[0] + ml_out[1]
    mu = jnp.mean(o, axis=-1, keepdims=True)
    var = jnp.mean(jnp.square(o - mu), axis=-1, keepdims=True)
    o = ((o - mu) * lax.rsqrt(var + EPS) * ml_norm_g.astype(jnp.float32).reshape(ML_HEADS, 1, ML_DV)
         * jax.nn.sigmoid(to_heads(mo, ML_HEADS)))
    y_ml = from_heads(o).astype(h.dtype) @ w_up_ml

    y = (jax.nn.sigmoid(ga) * y_hg + jax.nn.sigmoid(gb) * y_ml) @ w_out
    new_st = (jnp.stack(hg_state, axis=1), jnp.stack(ml_c, axis=1),
              jnp.stack(ml_n, axis=1), jnp.stack(ml_m, axis=1))
    return y, new_st


def layer(x, cond, st, lp):
    (w_mod, b_mod, g_pre_mix, g_post_mix, g_pre_ffn, g_post_ffn, w_in, hg_lb, hg_norm_g,
     ml_b_i, ml_b_f, ml_norm_g, w_up_hg, w_up_ml, w_out, w_ffn_in, w_ffn_out) = lp
    mod = jax.nn.silu(cond) @ w_mod + b_mod
    sh_a, sc_a, gt_a, sh_f, sc_f, gt_f = jnp.split(mod[:, None, :], 6, axis=-1)
    h = rms_norm(x, g_pre_mix) * (1.0 + sc_a) + sh_a
    y, new_st = mixer(h, st, w_in, hg_lb, hg_norm_g, ml_b_i, ml_b_f, ml_norm_g, w_up_hg, w_up_ml, w_out)
    x = x + gt_a * rms_norm(y, g_post_mix)
    h = rms_norm(x, g_pre_ffn) * (1.0 + sc_f) + sh_f
    a, b = jnp.split(h @ w_ffn_in, 2, axis=-1)
    x = x + gt_f * rms_norm((jax.nn.silu(a) * b) @ w_ffn_out, g_post_ffn)
    return x, new_st


def setup_inputs(seed: int = 0) -> dict:
    key = jax.random.key(seed)
    ks = jax.random.split(key, 26)

    def nrm(k, shape, s):
        return jax.random.normal(k, shape, jnp.float32) * s

    return {
        'x_prompt': nrm(ks[0], (BATCH, SEQ, D_MODEL), 1.0),
        'x_sample': nrm(ks[1], (DEC_BATCH, DEC_SEQ, D_MODEL), 1.0),
        'c': nrm(ks[2], (DEC_BATCH, D_MODEL), 1.0),
        'state_hgrn_s': nrm(ks[3], (DEC_BATCH, DEPTH, 2, HG_HEADS, HG_DK, HG_DV), 0.5),
        'state_mlstm_c': nrm(ks[4], (DEC_BATCH, DEPTH, 2, ML_HEADS, ML_DQK, ML_DV), 0.3),
        'state_mlstm_n': nrm(ks[5], (DEC_BATCH, DEPTH, 2, ML_HEADS, ML_DQK), 0.3),
        'state_mlstm_m': nrm(ks[6], (DEC_BATCH, DEPTH, 2, ML_HEADS), 1.0),
        'c_ctx': nrm(ks[7], (D_MODEL,), 1.0),
        'w_mod': nrm(ks[8], (DEPTH, D_MODEL, 6 * D_MODEL), 0.5 * D_MODEL ** -0.5),
        'b_mod': nrm(ks[9], (DEPTH, 6 * D_MODEL), 0.02),
        'norm_pre_mix': 1.0 + nrm(ks[10], (DEPTH, D_MODEL), 0.05),
        'norm_post_mix': 1.0 + nrm(ks[11], (DEPTH, D_MODEL), 0.05),
        'norm_pre_ffn': 1.0 + nrm(ks[12], (DEPTH, D_MODEL), 0.05),
        'norm_post_ffn': 1.0 + nrm(ks[13], (DEPTH, D_MODEL), 0.05),
        'w_in': nrm(ks[14], (DEPTH, D_MODEL, IN_WIDTH), D_MODEL ** -0.5),
        'hgrn_lb_logits': nrm(ks[15], (2, DEPTH + 1, HG_KEY), 0.1),
        'hgrn_norm_g': 1.0 + nrm(ks[16], (DEPTH, HG_DV), 0.05),
        'mlstm_b_i': nrm(ks[17], (DEPTH, 2 * ML_HEADS), 0.1),
        'mlstm_b_f': 3.0 + 3.0 * jax.random.uniform(ks[18], (DEPTH, 2 * ML_HEADS), jnp.float32),
        'mlstm_norm_g': 1.0 + nrm(ks[19], (DEPTH, ML_WIDTH), 0.05),
        'w_up_hgrn': nrm(ks[20], (DEPTH, HG_WIDTH, D_MODEL), HG_WIDTH ** -0.5),
        'w_up_mlstm': nrm(ks[21], (DEPTH, ML_WIDTH, D_MODEL), ML_WIDTH ** -0.5),
        'w_out': nrm(ks[22], (DEPTH, D_MODEL, D_MODEL), D_MODEL ** -0.5),
        'w_ffn_in': nrm(ks[23], (DEPTH, D_MODEL, 2 * D_FF), D_MODEL ** -0.5),
        'w_ffn_out': nrm(ks[24], (DEPTH, D_FF, D_MODEL), D_FF ** -0.5),
    }


def reference(x_prompt, x_sample, c, state_hgrn_s, state_mlstm_c, state_mlstm_n, state_mlstm_m,
              c_ctx, w_mod, b_mod, norm_pre_mix, norm_post_mix, norm_pre_ffn, norm_post_ffn,
              w_in, hgrn_lb_logits, hgrn_norm_g, mlstm_b_i, mlstm_b_f, mlstm_norm_g,
              w_up_hgrn, w_up_mlstm, w_out, w_ffn_in, w_ffn_out):
    hg_lb = jnp.cumsum(jax.nn.softmax(hgrn_lb_logits.astype(jnp.float32), axis=1), axis=1)
    bsz = x_prompt.shape[0]
    f32 = jnp.float32
    zero_st = (jnp.zeros((bsz, 2, HG_HEADS, HG_DK, HG_DV), f32),
               jnp.zeros((bsz, 2, ML_HEADS, ML_DQK, ML_DV), f32),
               jnp.zeros((bsz, 2, ML_HEADS, ML_DQK), f32),
               jnp.zeros((bsz, 2, ML_HEADS), f32))
    xp, xs = x_prompt, x_sample
    ctx_hg, ctx_c, ctx_n, ctx_m = [], [], [], []
    for l in range(DEPTH):
        lp = (w_mod[l], b_mod[l], norm_pre_mix[l], norm_post_mix[l], norm_pre_ffn[l], norm_post_ffn[l],
              w_in[l], hg_lb[:, l], hgrn_norm_g[l], mlstm_b_i[l], mlstm_b_f[l], mlstm_norm_g[l],
              w_up_hgrn[l], w_up_mlstm[l], w_out[l], w_ffn_in[l], w_ffn_out[l])
        xp, (s_hg, s_c, s_n, s_m) = layer(xp, c_ctx[None, :], zero_st, lp)
        ctx_hg.append(s_hg)
        ctx_c.append(s_c)
        ctx_n.append(s_n)
        ctx_m.append(s_m)
        cache_l = (state_hgrn_s[:, l], state_mlstm_c[:, l], state_mlstm_n[:, l], state_mlstm_m[:, l])
        xs, _ = layer(xs, c, cache_l, lp)
    new_hgrn_s = jnp.stack(ctx_hg, axis=1).astype(x_prompt.dtype)
    new_mlstm_c = jnp.stack(ctx_c, axis=1).astype(x_prompt.dtype)
    new_mlstm_n = jnp.stack(ctx_n, axis=1).astype(x_prompt.dtype)
    new_mlstm_m = jnp.stack(ctx_m, axis=1).astype(x_prompt.dtype)
    return (xp, xs, new_hgrn_s, new_mlstm_c, new_mlstm_n, new_mlstm_m)
```

```python
import functools

import jax
import jax.numpy as jnp
from jax import lax
from jax.experimental import pallas as pl
from jax.experimental.pallas import tpu as pltpu

F32 = jnp.float32
BF16 = jnp.bfloat16

D_MODEL = 2048
N_HEADS = 8
HG_DK = 128
HG_DV = 128
ML_DQK = 64
ML_DV = 128
HEAD_W = 128
D_FF = 5632
CHUNK = 64
EPS = 1e-6
N_MOD = 6 * D_MODEL
LANES = 128
MOD_ROWS = 16

SEC_HQ, SEC_HF_FW, SEC_HF_BW, SEC_HI, SEC_HGATE = 0, 8, 16, 24, 32
SEC_MQK, SEC_MV, SEC_MO = 40, 48, 56
SEC_GA, SEC_GB = 64, 80
PROJ_W = 96 * LANES

VMEM_LIMIT = 56 * 1024 * 1024


def _dot(a, b):
    return jnp.dot(a, b, preferred_element_type=F32)


def _dot_nt(a, b):
    return lax.dot_general(a, b, (((1,), (1,)), ((), ())), preferred_element_type=F32)


def _dot_tn(a, b):
    return lax.dot_general(a, b, (((0,), (0,)), ((), ())), preferred_element_type=F32)


def _sigmoid_pair(z):
    e = jnp.exp(-jnp.abs(z))
    r = 1.0 / (1.0 + e)
    er = e * r
    pos = z >= 0
    return jnp.where(pos, r, er), jnp.where(pos, er, r)


def _sigmoid(z):
    return _sigmoid_pair(z)[0]


def _silu(z):
    return z * _sigmoid(z)


def _log_sigmoid(z):
    return jnp.minimum(z, 0.0) - jnp.log1p(jnp.exp(-jnp.abs(z)))


def _split3(x):
    hi = x.astype(BF16)
    r1 = x - hi.astype(F32)
    mid = r1.astype(BF16)
    lo = (r1 - mid.astype(F32)).astype(BF16)
    return hi, mid, lo


def _exact_dot(sel, x):
    hi, mid, lo = _split3(x)
    return (_dot(sel, lo) + _dot(sel, mid)) + _dot(sel, hi)


def _exact_dot_r(x, sel):
    hi, mid, lo = _split3(x)
    return (_dot(lo, sel) + _dot(mid, sel)) + _dot(hi, sel)


def _mod_kernel(c_ref, w_ref, b_ref, o_ref):
    a = _silu(c_ref[...]).astype(BF16)
    o_ref[...] = _dot(a, w_ref[...].astype(BF16)) + b_ref[...]


def _modulation(cond, w_mod, b_mod):
    tn = 1024
    return pl.pallas_call(
        _mod_kernel,
        grid=(N_MOD // tn,),
        in_specs=[pl.BlockSpec((MOD_ROWS, D_MODEL), lambda j: (0, 0)),
                  pl.BlockSpec((D_MODEL, tn), lambda j: (0, j)),
                  pl.BlockSpec((1, tn), lambda j: (0, j))],
        out_specs=pl.BlockSpec((MOD_ROWS, tn), lambda j: (0, j)),
        out_shape=jax.ShapeDtypeStruct((MOD_ROWS, N_MOD), F32),
        compiler_params=pltpu.CompilerParams(dimension_semantics=("arbitrary",),
                                             vmem_limit_bytes=VMEM_LIMIT),
        name="modulation",
    )(cond, w_mod, b_mod)


def _rms_rows(x, g):
    return x * lax.rsqrt(jnp.mean(x * x, axis=-1, keepdims=True) + EPS) * g


def _inproj_kernel(x_ref, sh_ref, sc_ref, g_ref, w_ref, wg_ref, proj_ref, gates_ref, xn_ref, *, tm):
    rows = 128

    @pl.when(pl.program_id(1) == 0)
    def _():
        def body(i, carry):
            r0 = pl.multiple_of(i * rows, rows)
            x = x_ref[pl.ds(r0, rows), :]
            h = _rms_rows(x, g_ref[...]) * (1.0 + sc_ref[0]) + sh_ref[0]
            hb = h.astype(BF16)
            xn_ref[pl.ds(r0, rows), :] = hb
            gates_ref[pl.ds(r0, rows), :] = _dot(hb, wg_ref[...])
            return carry
        lax.fori_loop(0, tm // rows, body, 0)

    proj_ref[...] = _dot(xn_ref[...], w_ref[...])


def _in_proj(x2d, mod3, mod_row_fn, g_pre, w_main, w_gate, *, tm, tn):
    m = x2d.shape[0]
    kern = functools.partial(_inproj_kernel, tm=tm)
    return pl.pallas_call(
        kern,
        grid=(m // tm, PROJ_W // tn),
        in_specs=[pl.BlockSpec((tm, D_MODEL), lambda i, j: (i, 0)),
                  pl.BlockSpec((1, 1, D_MODEL), lambda i, j: (mod_row_fn(i), 0, 0)),
                  pl.BlockSpec((1, 1, D_MODEL), lambda i, j: (mod_row_fn(i), 0, 1)),
                  pl.BlockSpec((1, D_MODEL), lambda i, j: (0, 0)),
                  pl.BlockSpec((D_MODEL, tn), lambda i, j: (0, j)),
                  pl.BlockSpec((D_MODEL, LANES), lambda i, j: (0, 0))],
        out_specs=[pl.BlockSpec((tm, tn), lambda i, j: (i, j)),
                   pl.BlockSpec((tm, LANES), lambda i, j: (i, 0))],
        out_shape=[jax.ShapeDtypeStruct((m, PROJ_W), F32),
                   jax.ShapeDtypeStruct((m, LANES), F32)],
        scratch_shapes=[pltpu.VMEM((tm, D_MODEL), BF16)],
        compiler_params=pltpu.CompilerParams(dimension_semantics=("arbitrary", "arbitrary"),
                                             vmem_limit_bytes=VMEM_LIMIT),
        name="in_proj",
    )(x2d, mod3, mod3, g_pre, w_main, w_gate)


def _tri_masks():
    row = lax.broadcasted_iota(jnp.int32, (CHUNK, CHUNK), 0)
    col = lax.broadcasted_iota(jnp.int32, (CHUNK, CHUNK), 1)
    return col <= row, col >= row


def _hgrn_chunk(q_raw, z, v, lb, st_ref, mask, cum_bf, rev):
    sp, sn = _sigmoid_pair(z)
    oml = 1.0 - lb
    log_f = jnp.log(lb + oml * sp)
    k = oml * sn
    q = _silu(q_raw)
    b = _exact_dot(cum_bf, log_f)
    mid = CHUNK // 2
    if rev:
        b_ref, b_last = b[CHUNK - 1 - mid:CHUNK - mid], b[0:1]
    else:
        b_ref, b_last = b[mid:mid + 1], b[CHUNK - 1:CHUNK]
    qa = (q * jnp.exp(b - b_ref)).astype(BF16)
    ka = (k * jnp.exp(b_ref - b)).astype(BF16)
    att = jnp.where(mask, _dot_nt(qa, ka), 0.0)
    vb = v.astype(BF16)
    o_intra = _dot(att.astype(BF16), vb)
    kd = (k * jnp.exp(b_last - b)).astype(BF16)
    ds_t = _dot_tn(vb, kd)
    st = st_ref[...]
    o_inter = _dot_nt((q * jnp.exp(b)).astype(BF16), st.astype(BF16))
    st_ref[...] = jnp.exp(b_last) * st + ds_t
    return o_intra + o_inter


def _mlstm_chunk(q, k, v, li, lf, c_ref, n_ref, m_ref, mask, cum_bf, rev):
    g = _exact_dot(cum_bf, lf)
    g_last = g[0:1] if rev else g[CHUNK - 1:CHUNK]
    u = li - g
    w_end = g_last + u
    m_loc = jnp.max(w_end, axis=0, keepdims=True)
    kb = k.astype(BF16)
    vb = v.astype(BF16)
    qb = q.astype(BF16)
    wk = jnp.exp(w_end - m_loc)[:, :ML_DQK] * k
    c_loc = _dot_tn(wk.astype(BF16), vb)
    n_loc = jnp.sum(wk, axis=0, keepdims=True)
    c_st, n_st, m_st = c_ref[...], n_ref[...], m_ref[...]

    log_inter = g + m_st
    u_row = jnp.transpose(u)[:CHUNK, :CHUNK]
    log_d = jnp.where(mask, g[:, :CHUNK] + u_row, -jnp.inf)
    m_intra = jnp.max(log_d, axis=-1, keepdims=True)
    m_t = jnp.maximum(log_inter, m_intra)
    s_inter = jnp.exp(log_inter - m_t)
    qk = _dot_nt(qb, kb) * jnp.exp(log_d - m_t[:, :CHUNK])
    num = _dot(qk.astype(BF16), vb) + s_inter * _dot(qb, c_st.astype(BF16))
    qn = jnp.sum(q * n_st, axis=-1, keepdims=True)
    den = jnp.sum(qk, axis=-1, keepdims=True) + s_inter * qn
    h = num / jnp.maximum(jnp.abs(den), jnp.exp(-m_t))

    a = g_last
    m_new = jnp.maximum(a + m_st, m_loc)
    s_old = jnp.exp(a + m_st - m_new)
    s_loc = jnp.exp(m_loc - m_new)
    c_ref[...] = s_old * c_st + s_loc * c_loc
    n_ref[...] = s_old[:, :ML_DQK] * n_st + s_loc[:, :ML_DQK] * n_loc
    m_ref[...] = m_new
    return h


def _scan_kernel(*refs, t_len, has_init):
    (hq_ref, hff_ref, hfb_ref, hi_ref, hgate_ref, mqk_ref, mv_ref, mo_ref, gates_ref,
     lbl_ref, hgn_ref, gbias_ref, mln_ref) = refs[:13]
    pos = 13
    if has_init:
        s0_ref, c0_ref, n0_ref, m0_ref = refs[pos:pos + 4]
        pos += 4
    hg_out_ref, ml_out_ref = refs[pos:pos + 2]
    pos += 2
    if not has_init:
        sfin_ref, cfin_ref, nfin_ref, mfin_ref = refs[pos:pos + 4]
        pos += 4
    sel_ref, ohg_ref, oml_ref, st_ref, c_ref, n_ref, m_ref = refs[pos:]

    b_idx = pl.program_id(0)
    h_idx = pl.program_id(1)
    n_chunks = t_len // CHUNK

    logits = lbl_ref[...]
    l0, l1 = logits[:, 0, :], logits[:, 1, :]
    mx = jnp.maximum(l0, l1)
    e0, e1 = jnp.exp(l0 - mx), jnp.exp(l1 - mx)
    lb_all = e0 / (e0 + e1)

    r_i = lax.broadcasted_iota(jnp.int32, (LANES, 4 * LANES), 0)
    c_i = lax.broadcasted_iota(jnp.int32, (LANES, 4 * LANES), 1)
    blk = jnp.right_shift(c_i, 7)
    target = h_idx + jnp.where(blk == 0, 0, jnp.where(blk == 1, 2 * N_HEADS,
                                                      jnp.where(blk == 2, N_HEADS, 3 * N_HEADS)))
    onehot = jnp.where(r_i == target, 1.0, 0.0).astype(BF16)
    lane = lax.broadcasted_iota(jnp.int32, (1, LANES), 1)
    is_f = (lane >= 2 * N_HEADS) & (lane < 4 * N_HEADS)
    sel_rows = 256 if t_len % 256 == 0 else CHUNK

    def sel_body(i, carry):
        r0 = pl.multiple_of(i * sel_rows, sel_rows)
        xg = gates_ref[pl.ds(r0, sel_rows), :] + gbias_ref[...]
        xg = jnp.where(is_f, _log_sigmoid(xg), xg)
        sel_ref[pl.ds(r0, sel_rows), :] = _exact_dot_r(xg, onehot)
        return carry
    lax.fori_loop(0, t_len // sel_rows, sel_body, 0)

    if has_init:
        for d in range(2):
            st_ref[d] = jnp.transpose(s0_ref[0, 0, d, 0])
            c_ref[d] = c0_ref[0, 0, d, 0]
            n_ref[d] = n0_ref[0, 0, d, 0]
            m_ref[d] = jnp.full((1, LANES), m0_ref[b_idx, d * N_HEADS + h_idx], F32)
    else:
        st_ref[...] = jnp.zeros_like(st_ref)
        c_ref[...] = jnp.zeros_like(c_ref)
        n_ref[...] = jnp.zeros_like(n_ref)
        m_ref[...] = jnp.zeros_like(m_ref)

    mask_fw, mask_bw = _tri_masks()
    cum_fw = jnp.where(mask_fw, 1.0, 0.0).astype(BF16)
    cum_bw = jnp.where(mask_bw, 1.0, 0.0).astype(BF16)

    def chunk_body(n, carry):
        for d, (mask, cum, zf_ref) in enumerate(((mask_fw, cum_fw, hff_ref), (mask_bw, cum_bw, hfb_ref))):
            rev = d == 1
            cn = (n_chunks - 1 - n) if rev else n
            r0 = pl.multiple_of(cn * CHUNK, CHUNK)
            rows = pl.ds(r0, CHUNK)
            o = _hgrn_chunk(hq_ref[rows, :], zf_ref[rows, :], hi_ref[rows, :], lb_all[d:d + 1],
                            st_ref.at[d], mask, cum, rev)
            ohg_ref[d, rows, :] = o
            qk = mqk_ref[rows, :]
            li = sel_ref[rows, (2 * d) * LANES:(2 * d + 1) * LANES]
            lf = sel_ref[rows, (2 * d + 1) * LANES:(2 * d + 2) * LANES]
            hh = _mlstm_chunk(qk[:, :ML_DQK], qk[:, ML_DQK:] * (ML_DQK ** -0.5), mv_ref[rows, :], li, lf,
                              c_ref.at[d], n_ref.at[d], m_ref.at[d], mask, cum, rev)
            oml_ref[d, rows, :] = hh
        return carry
    lax.fori_loop(0, n_chunks, chunk_body, 0)

    fin_rows = 256 if t_len % 256 == 0 else CHUNK

    def fin_body(i, carry):
        r0 = pl.multiple_of(i * fin_rows, fin_rows)
        rows = pl.ds(r0, fin_rows)
        o = ohg_ref[0, rows, :] + ohg_ref[1, rows, :]
        o = (o * lax.rsqrt(jnp.mean(o * o, axis=-1, keepdims=True) + EPS) * hgn_ref[...]
             * _silu(hgate_ref[rows, :]))
        hg_out_ref[rows, :] = o.astype(BF16)
        o = oml_ref[0, rows, :] + oml_ref[1, rows, :]
        mu = jnp.mean(o, axis=-1, keepdims=True)
        oc = o - mu
        var = jnp.mean(oc * oc, axis=-1, keepdims=True)
        o = oc * lax.rsqrt(var + EPS) * mln_ref[...] * _sigmoid(mo_ref[rows, :])
        ml_out_ref[rows, :] = o.astype(BF16)
        return carry
    lax.fori_loop(0, t_len // fin_rows, fin_body, 0)

    if not has_init:
        for d in range(2):
            sfin_ref[0, d, 0] = jnp.transpose(st_ref[d])
            cfin_ref[0, d, 0] = c_ref[d]
            nfin_ref[0, d, 0] = n_ref[d]
            mfin_ref[0, d, 0] = m_ref[d]


def _scan(proj, gates, lb_logits, hg_norm_g, gate_bias, ml_norm_g, init, *, bsz, t_len):
    has_init = init is not None
    m = bsz * t_len

    def col(sec):
        return pl.BlockSpec((t_len, HEAD_W), lambda b, h, sec=sec: (b, sec + h))

    in_specs = [col(SEC_HQ), col(SEC_HF_FW), col(SEC_HF_BW), col(SEC_HI), col(SEC_HGATE),
                col(SEC_MQK), col(SEC_MV), col(SEC_MO),
                pl.BlockSpec((t_len, LANES), lambda b, h: (b, 0)),
                pl.BlockSpec((2, 2, HEAD_W), lambda b, h: (0, 0, h)),
                pl.BlockSpec((1, HG_DV), lambda b, h: (0, 0)),
                pl.BlockSpec((1, LANES), lambda b, h: (0, 0)),
                pl.BlockSpec((1, ML_DV), lambda b, h: (0, h))]
    args = [proj] * 8 + [gates, lb_logits, hg_norm_g, gate_bias, ml_norm_g]
    out_specs = [pl.BlockSpec((t_len, HEAD_W), lambda b, h: (b, h)),
                 pl.BlockSpec((t_len, HEAD_W), lambda b, h: (b, h))]
    out_shape = [jax.ShapeDtypeStruct((m, N_HEADS * HG_DV), BF16),
                 jax.ShapeDtypeStruct((m, N_HEADS * ML_DV), BF16)]
    if has_init:
        s0, c0, n0, m0 = init
        in_specs += [pl.BlockSpec((1, 1, 2, 1, HG_DK, HG_DV), lambda b, h: (b, 0, 0, h, 0, 0)),
                     pl.BlockSpec((1, 1, 2, 1, ML_DQK, ML_DV), lambda b, h: (b, 0, 0, h, 0, 0)),
                     pl.BlockSpec((1, 1, 2, 1, 1, ML_DQK), lambda b, h: (b, 0, 0, h, 0, 0)),
                     pl.BlockSpec(memory_space=pltpu.SMEM)]
        args += [s0, c0, n0.reshape(n0.shape[:4] + (1, ML_DQK)), m0.reshape(bsz, 2 * N_HEADS)]
    else:
        out_specs += [pl.BlockSpec((1, 2, 1, HG_DK, HG_DV), lambda b, h: (b, 0, h, 0, 0)),
                      pl.BlockSpec((1, 2, 1, ML_DQK, ML_DV), lambda b, h: (b, 0, h, 0, 0)),
                      pl.BlockSpec((1, 2, 1, 1, ML_DQK), lambda b, h: (b, 0, h, 0, 0)),
                      pl.BlockSpec((1, 2, 1, 1, LANES), lambda b, h: (b, 0, h, 0, 0))]
        out_shape += [jax.ShapeDtypeStruct((bsz, 2, N_HEADS, HG_DK, HG_DV), F32),
                      jax.ShapeDtypeStruct((bsz, 2, N_HEADS, ML_DQK, ML_DV), F32),
                      jax.ShapeDtypeStruct((bsz, 2, N_HEADS, 1, ML_DQK), F32),
                      jax.ShapeDtypeStruct((bsz, 2, N_HEADS, 1, LANES), F32)]
    scratch = [pltpu.VMEM((t_len, 4 * LANES), F32),
               pltpu.VMEM((2, t_len, HG_DV), F32),
               pltpu.VMEM((2, t_len, ML_DV), F32),
               pltpu.VMEM((2, HG_DV, HG_DK), F32),
               pltpu.VMEM((2, ML_DQK, ML_DV), F32),
               pltpu.VMEM((2, 1, ML_DQK), F32),
               pltpu.VMEM((2, 1, LANES), F32)]
    kern = functools.partial(_scan_kernel, t_len=t_len, has_init=has_init)
    return pl.pallas_call(
        kern,
        grid=(bsz, N_HEADS),
        in_specs=in_specs,
        out_specs=out_specs,
        out_shape=out_shape,
        scratch_shapes=scratch,
        compiler_params=pltpu.CompilerParams(dimension_semantics=("arbitrary", "arbitrary"),
                                             vmem_limit_bytes=VMEM_LIMIT),
        name="scan_init" if has_init else "scan_ctx",
    )(*args)


def _mix_out_kernel(hg_ref, ml_ref, ga_ref, gb_ref, x_ref, gt_ref, sh_ref, sc_ref, gpost_ref, gpre_ref,
                    wuh_ref, wum_ref, wo_ref, x1_ref, h2_ref):
    y_hg = _dot(hg_ref[...], wuh_ref[...])
    y_ml = _dot(ml_ref[...], wum_ref[...])
    merged = _sigmoid(ga_ref[...]) * y_hg + _sigmoid(gb_ref[...]) * y_ml
    y = _dot(merged.astype(BF16), wo_ref[...])
    x1 = x_ref[...] + gt_ref[0] * _rms_rows(y, gpost_ref[...])
    x1_ref[...] = x1
    h2 = _rms_rows(x1, gpre_ref[...]) * (1.0 + sc_ref[0]) + sh_ref[0]
    h2_ref[...] = h2.astype(BF16)


def _mix_out(hg_act, ml_act, proj, x2d, mod3, mod_row_fn, g_post_mix, g_pre_ffn, w_up_hg, w_up_ml, w_out, *, tm):
    m = x2d.shape[0]

    def resident(shape):
        return pl.BlockSpec(shape, lambda i: (0, 0), pipeline_mode=pl.Buffered(1))

    return pl.pallas_call(
        _mix_out_kernel,
        grid=(m // tm,),
        in_specs=[pl.BlockSpec((tm, N_HEADS * HG_DV), lambda i: (i, 0)),
                  pl.BlockSpec((tm, N_HEADS * ML_DV), lambda i: (i, 0)),
                  pl.BlockSpec((tm, D_MODEL), lambda i: (i, SEC_GA * LANES // D_MODEL)),
                  pl.BlockSpec((tm, D_MODEL), lambda i: (i, SEC_GB * LANES // D_MODEL)),
                  pl.BlockSpec((tm, D_MODEL), lambda i: (i, 0)),
                  pl.BlockSpec((1, 1, D_MODEL), lambda i: (mod_row_fn(i), 0, 2)),
                  pl.BlockSpec((1, 1, D_MODEL), lambda i: (mod_row_fn(i), 0, 3)),
                  pl.BlockSpec((1, 1, D_MODEL), lambda i: (mod_row_fn(i), 0, 4)),
                  resident((1, D_MODEL)),
                  resident((1, D_MODEL)),
                  resident((N_HEADS * HG_DV, D_MODEL)),
                  resident((N_HEADS * ML_DV, D_MODEL)),
                  resident((D_MODEL, D_MODEL))],
        out_specs=[pl.BlockSpec((tm, D_MODEL), lambda i: (i, 0)),
                   pl.BlockSpec((tm, D_MODEL), lambda i: (i, 0))],
        out_shape=[jax.ShapeDtypeStruct((m, D_MODEL), F32),
                   jax.ShapeDtypeStruct((m, D_MODEL), BF16)],
        compiler_params=pltpu.CompilerParams(dimension_semantics=("arbitrary",),
                                             vmem_limit_bytes=VMEM_LIMIT),
        name="mix_out",
    )(hg_act, ml_act, proj, proj, x2d, mod3, mod3, mod3, g_post_mix, g_pre_ffn, w_up_hg, w_up_ml, w_out)


def _ffn_kernel(h2_ref, x1_ref, gt_ref, gpost_ref, wa_ref, wb_ref, wo_ref, out_ref, acc_ref):
    j = pl.program_id(1)
    h2 = h2_ref[...]
    a = _dot(h2, wa_ref[...])
    b = _dot(h2, wb_ref[...])
    part = _dot((_silu(a) * b).astype(BF16), wo_ref[...])

    @pl.when(j == 0)
    def _():
        acc_ref[...] = part

    @pl.when(j > 0)
    def _():
        acc_ref[...] += part

    @pl.when(j == pl.num_programs(1) - 1)
    def _():
        out_ref[...] = x1_ref[...] + gt_ref[0] * _rms_rows(acc_ref[...], gpost_ref[...])


def _ffn(h2, x1, mod3, mod_row_fn, g_post_ffn, w_ffn_in, w_ffn_out, *, tm, tf):
    m = x1.shape[0]
    nf = D_FF // tf
    return pl.pallas_call(
        _ffn_kernel,
        grid=(m // tm, nf),
        in_specs=[pl.BlockSpec((tm, D_MODEL), lambda i, j: (i, 0)),
                  pl.BlockSpec((tm, D_MODEL), lambda i, j: (i, 0)),
                  pl.BlockSpec((1, 1, D_MODEL), lambda i, j: (mod_row_fn(i), 0, 5)),
                  pl.BlockSpec((1, D_MODEL), lambda i, j: (0, 0)),
                  pl.BlockSpec((D_MODEL, tf), lambda i, j: (0, j)),
                  pl.BlockSpec((D_MODEL, tf), lambda i, j, nf=nf: (0, j + nf)),
                  pl.BlockSpec((tf, D_MODEL), lambda i, j: (j, 0))],
        out_specs=pl.BlockSpec((tm, D_MODEL), lambda i, j: (i, 0)),
        out_shape=jax.ShapeDtypeStruct((m, D_MODEL), F32),
        scratch_shapes=[pltpu.VMEM((tm, D_MODEL), F32)],
        compiler_params=pltpu.CompilerParams(dimension_semantics=("arbitrary", "arbitrary"),
                                             vmem_limit_bytes=VMEM_LIMIT),
        name="ffn",
    )(h2, x1, mod3, g_post_ffn, w_ffn_in, w_ffn_in, w_ffn_out)


def _prep_w_in(w):
    o = 0
    parts = {}
    for name, width in (("hg", 5 * 1024), ("mq", 512), ("mk", 512), ("mv", 1024), ("mi", 16), ("mf", 16),
                        ("mo", 1024), ("gab", 2 * D_MODEL)):
        parts[name] = w[:, o:o + width]
        o += width
    mqk = jnp.concatenate([parts["mq"].reshape(D_MODEL, N_HEADS, ML_DQK),
                           parts["mk"].reshape(D_MODEL, N_HEADS, ML_DQK)], axis=-1).reshape(D_MODEL, 2 * 512)
    w_main = jnp.concatenate([parts["hg"], mqk, parts["mv"], parts["mo"], parts["gab"]], axis=1).astype(BF16)
    w_gate = jnp.concatenate([parts["mi"], parts["mf"],
                              jnp.zeros((D_MODEL, LANES - 4 * N_HEADS), w.dtype)], axis=1).astype(BF16)
    return w_main, w_gate


def _layer_group(x, mod3, mod_row_fn, init, p, *, tm_in, tn_in, tm_mix, tm_ffn, tf):
    bsz, t_len, _ = x.shape
    x2d = x.reshape(bsz * t_len, D_MODEL)
    proj, gates = _in_proj(x2d, mod3, lambda i: mod_row_fn(i, tm_in), p["g_pre_mix"], p["w_main"], p["w_gate"],
                           tm=tm_in, tn=tn_in)
    outs = _scan(proj, gates, p["lb_logits"], p["hg_norm_g"], p["gate_bias"], p["ml_norm_g"], init,
                 bsz=bsz, t_len=t_len)
    hg_act, ml_act = outs[0], outs[1]
    x1, h2 = _mix_out(hg_act, ml_act, proj, x2d, mod3, lambda i: mod_row_fn(i, tm_mix), p["g_post_mix"],
                      p["g_pre_ffn"], p["w_up_hg"], p["w_up_ml"], p["w_out"], tm=tm_mix)
    y = _ffn(h2, x1, mod3, lambda i: mod_row_fn(i, tm_ffn), p["g_post_ffn"], p["w_ffn_in"], p["w_ffn_out"],
             tm=tm_ffn, tf=tf)
    return y.reshape(bsz, t_len, D_MODEL), outs[2:]


def kernel(x_prompt, x_sample, c, state_hgrn_s, state_mlstm_c, state_mlstm_n, state_mlstm_m, c_ctx, w_mod, b_mod,
           norm_pre_mix, norm_post_mix, norm_pre_ffn, norm_post_ffn, w_in, hgrn_lb_logits, hgrn_norm_g, mlstm_b_i,
           mlstm_b_f, mlstm_norm_g, w_up_hgrn, w_up_mlstm, w_out, w_ffn_in, w_ffn_out):
    bsz_p, t_p, _ = x_prompt.shape
    bsz_s, t_s, _ = x_sample.shape

    cond = jnp.concatenate([c_ctx[None, :], c, jnp.zeros((MOD_ROWS - 1 - bsz_s, D_MODEL), F32)], axis=0)
    mod = _modulation(cond, w_mod[0], b_mod[0][None, :])
    mod3 = mod.reshape(MOD_ROWS, 1, N_MOD)

    w_main, w_gate = _prep_w_in(w_in[0])
    p = dict(
        w_main=w_main, w_gate=w_gate,
        g_pre_mix=norm_pre_mix[0][None, :], g_post_mix=norm_post_mix[0][None, :],
        g_pre_ffn=norm_pre_ffn[0][None, :], g_post_ffn=norm_post_ffn[0][None, :],
        lb_logits=hgrn_lb_logits, hg_norm_g=hgrn_norm_g[0][None, :],
        gate_bias=jnp.concatenate([mlstm_b_i[0], mlstm_b_f[0], jnp.zeros((LANES - 4 * N_HEADS,), F32)])[None, :],
        ml_norm_g=mlstm_norm_g[0][None, :],
        w_up_hg=w_up_hgrn[0].astype(BF16), w_up_ml=w_up_mlstm[0].astype(BF16), w_out=w_out[0].astype(BF16),
        w_ffn_in=w_ffn_in[0].astype(BF16), w_ffn_out=w_ffn_out[0].astype(BF16),
    )
    tiles_p = dict(tm_in=min(1024, bsz_p * t_p), tn_in=1024, tm_mix=256, tm_ffn=min(512, bsz_p * t_p), tf=512)
    tiles_s = dict(tm_in=min(1024, t_s), tn_in=1024, tm_mix=min(256, t_s), tm_ffn=min(512, t_s), tf=512)
    y_p, ctx_states = _layer_group(x_prompt, mod3, lambda i, tm: 0, None, p, **tiles_p)
    init = (state_hgrn_s, state_mlstm_c, state_mlstm_n, state_mlstm_m[:, 0])
    y_s, _ = _layer_group(x_sample, mod3, lambda i, tm: 1 + (i * tm) // t_s, init, p, **tiles_s)

    s_fin, c_fin, n_fin, m_fin = ctx_states
    new_hgrn_s = s_fin[:, None]
    new_mlstm_c = c_fin[:, None]
    new_mlstm_n = n_fin[:, None, :, :, 0, :]
    new_mlstm_m = m_fin[:, None, :, :, 0, 0]
    return (y_p, y_s, new_hgrn_s, new_mlstm_c, new_mlstm_n, new_mlstm_m)
```

```python
import functools

import jax
import jax.numpy as jnp
from jax import lax
from jax.experimental import pallas as pl
from jax.experimental.pallas import tpu as pltpu

F32 = jnp.float32
BF16 = jnp.bfloat16

D_MODEL = 2048
N_HEADS = 8
HG_DK = 128
HG_DV = 128
ML_DQK = 64
ML_DV = 128
HEAD_W = 128
D_FF = 5632
CHUNK = 64
EPS = 1e-6
N_MOD = 6 * D_MODEL
LANES = 128
MOD_ROWS = 16
BLK = 2 * CHUNK
SEL_ROWS = 512
GATE_W = 4 * N_HEADS

SEC_HQ, SEC_HF_FW, SEC_HF_BW, SEC_HI, SEC_HGATE = 0, 8, 16, 24, 32
SEC_MQK, SEC_MV, SEC_MO = 40, 48, 56
SEC_GA, SEC_GB = 64, 80
PROJ_W = 96 * LANES

VMEM_LIMIT = 56 * 1024 * 1024


def _dot(a, b):
    return jnp.dot(a, b, preferred_element_type=F32)


def _dot_nt(a, b):
    return lax.dot_general(a, b, (((1,), (1,)), ((), ())), preferred_element_type=F32)


def _dot_tn(a, b):
    return lax.dot_general(a, b, (((0,), (0,)), ((), ())), preferred_element_type=F32)


def _sigmoid_pair(z):
    e = jnp.exp(-jnp.abs(z))
    r = 1.0 / (1.0 + e)
    er = e * r
    pos = z >= 0
    return jnp.where(pos, r, er), jnp.where(pos, er, r)


def _sigmoid(z):
    return _sigmoid_pair(z)[0]


def _silu(z):
    return z * _sigmoid(z)


def _log_sigmoid(z):
    return jnp.minimum(z, 0.0) - jnp.log1p(jnp.exp(-jnp.abs(z)))


def _mod_kernel(c_ref, w_ref, b_ref, o_ref):
    a = _silu(c_ref[...]).astype(BF16)
    o_ref[...] = _dot(a, w_ref[...].astype(BF16)) + b_ref[...]


def _modulation(cond, w_mod, b_mod):
    tn = 1024
    return pl.pallas_call(
        _mod_kernel,
        grid=(N_MOD // tn,),
        in_specs=[pl.BlockSpec((MOD_ROWS, D_MODEL), lambda j: (0, 0)),
                  pl.BlockSpec((D_MODEL, tn), lambda j: (0, j)),
                  pl.BlockSpec((1, tn), lambda j: (0, j))],
        out_specs=pl.BlockSpec((MOD_ROWS, tn), lambda j: (0, j)),
        out_shape=jax.ShapeDtypeStruct((MOD_ROWS, N_MOD), F32),
        compiler_params=pltpu.CompilerParams(dimension_semantics=("arbitrary",),
                                             vmem_limit_bytes=VMEM_LIMIT),
        name="modulation",
    )(cond, w_mod, b_mod)


def _rms_rows(x, g):
    return x * lax.rsqrt(jnp.mean(x * x, axis=-1, keepdims=True) + EPS) * g


def _inproj_kernel(x_ref, sh_ref, sc_ref, g_ref, w_ref, wg_ref, proj_ref, gates_ref, xn_ref, *, tm):
    rows = 128

    @pl.when(pl.program_id(1) == 0)
    def _():
        def body(i, carry):
            r0 = pl.multiple_of(i * rows, rows)
            x = x_ref[pl.ds(r0, rows), :]
            h = _rms_rows(x, g_ref[...]) * (1.0 + sc_ref[0]) + sh_ref[0]
            hb = h.astype(BF16)
            xn_ref[pl.ds(r0, rows), :] = hb
            gates_ref[pl.ds(r0, rows), :] = _dot(hb, wg_ref[...])
            return carry
        lax.fori_loop(0, tm // rows, body, 0)

    proj_ref[...] = _dot(xn_ref[...], w_ref[...])


def _in_proj(x2d, mod3, mod_row_fn, g_pre, w_main, w_gate, *, tm, tn):
    m = x2d.shape[0]
    kern = functools.partial(_inproj_kernel, tm=tm)
    return pl.pallas_call(
        kern,
        grid=(m // tm, PROJ_W // tn),
        in_specs=[pl.BlockSpec((tm, D_MODEL), lambda i, j: (i, 0)),
                  pl.BlockSpec((1, 1, D_MODEL), lambda i, j: (mod_row_fn(i), 0, 0)),
                  pl.BlockSpec((1, 1, D_MODEL), lambda i, j: (mod_row_fn(i), 0, 1)),
                  pl.BlockSpec((1, D_MODEL), lambda i, j: (0, 0)),
                  pl.BlockSpec((D_MODEL, tn), lambda i, j: (0, j)),
                  pl.BlockSpec((D_MODEL, LANES), lambda i, j: (0, 0))],
        out_specs=[pl.BlockSpec((tm, tn), lambda i, j: (i, j)),
                   pl.BlockSpec((tm, LANES), lambda i, j: (i, 0))],
        out_shape=[jax.ShapeDtypeStruct((m, PROJ_W), F32),
                   jax.ShapeDtypeStruct((m, LANES), F32)],
        scratch_shapes=[pltpu.VMEM((tm, D_MODEL), BF16)],
        compiler_params=pltpu.CompilerParams(dimension_semantics=("arbitrary", "arbitrary"),
                                             vmem_limit_bytes=VMEM_LIMIT),
        name="in_proj",
    )(x2d, mod3, mod3, g_pre, w_main, w_gate)


def _chunk_cumsum(x, pos, rev):
    n = x.shape[0]
    s = 1
    while s < CHUNK:
        if rev:
            x = x + jnp.where(pos < CHUNK - s, pltpu.roll(x, n - s, axis=0), 0.0)
        else:
            x = x + jnp.where(pos >= s, pltpu.roll(x, s, axis=0), 0.0)
        s *= 2
    return x


def _per_chunk(x):
    return x.reshape(BLK // CHUNK, CHUNK, x.shape[-1])


def _scan_kernel(*refs, t_len, has_init):
    (hq_ref, hff_ref, hfb_ref, hi_ref, hgate_ref, mqk_ref, mv_ref, mo_ref, gates_ref,
     lbl_ref, hgn_ref, gbias_ref, mln_ref) = refs[:13]
    pos = 13
    if has_init:
        s0_ref, c0_ref, n0_ref, m0_ref = refs[pos:pos + 4]
        pos += 4
    hg_out_ref, ml_out_ref = refs[pos:pos + 2]
    pos += 2
    if not has_init:
        sfin_ref, cfin_ref, nfin_ref, mfin_ref = refs[pos:pos + 4]
        pos += 4
    (qa_ref, ka_ref, qx_ref, vt_ref, ds_ref, dec_ref, stt_ref,
     sel_ref, g_ref, ut_ref, cloc_ref, nloc_ref, a_ref, mloc_ref, mst_ref, sold_ref, sloc_ref,
     cst_ref, nst_ref, hsum_ref) = refs[pos:]

    b_idx = pl.program_id(0)
    h_idx = pl.program_id(1)
    n_chunks = t_len // CHUNK
    n_blocks = t_len // BLK
    cpb = BLK // CHUNK
    mid = CHUNK // 2

    row = lax.broadcasted_iota(jnp.int32, (BLK, BLK), 0)
    col = lax.broadcasted_iota(jnp.int32, (BLK, BLK), 1)
    same_chunk = jnp.right_shift(row, 6) == jnp.right_shift(col, 6)
    masks = (same_chunk & (col <= row), same_chunk & (col >= row))
    in_pos = jnp.bitwise_and(row, CHUNK - 1)
    row_lo = row < CHUNK
    lane_lo = col < ML_DQK
    lane = lax.broadcasted_iota(jnp.int32, (1, LANES), 1)

    def first_chunk_only(x):
        return jnp.where(row_lo, x, 0.0).astype(BF16)

    def second_chunk_only(x):
        return jnp.where(row_lo, 0.0, x).astype(BF16)

    logits = lbl_ref[...]
    l0, l1 = logits[:, 0, :], logits[:, 1, :]
    mx = jnp.maximum(l0, l1)
    e0, e1 = jnp.exp(l0 - mx), jnp.exp(l1 - mx)
    lb_all = e0 / (e0 + e1)


    def hg_prep(i, carry):
        rows = pl.ds(pl.multiple_of(i * BLK, BLK), BLK)
        q = _silu(hq_ref[rows, :])
        vt = jnp.transpose(hi_ref[rows, :]).astype(BF16)
        vt_ref[i] = vt
        kd_cols = []
        for d, zf_ref in enumerate((hff_ref, hfb_ref)):
            rev = d == 1
            lb = lb_all[d:d + 1]
            oml = 1.0 - lb
            sp, sn = _sigmoid_pair(zf_ref[rows, :])
            k = oml * sn
            b = _chunk_cumsum(jnp.log(lb + oml * sp), in_pos, rev)
            b3, q3, k3 = _per_chunk(b), _per_chunk(q), _per_chunk(k)
            i_mid, i_last = (CHUNK - 1 - mid, 0) if rev else (mid, CHUNK - 1)
            b_mid, b_last = b3[:, i_mid:i_mid + 1], b3[:, i_last:i_last + 1]
            qa_ref[d, rows, :] = (q3 * jnp.exp(b3 - b_mid)).reshape(BLK, HG_DK).astype(BF16)
            ka_ref[d, rows, :] = (k3 * jnp.exp(b_mid - b3)).reshape(BLK, HG_DK).astype(BF16)
            qi = q * jnp.exp(b)
            qx_ref[d, rows, :HG_DK] = first_chunk_only(qi)
            qx_ref[d, rows, HG_DK:] = second_chunk_only(qi)
            kd = (k3 * jnp.exp(b_last - b3)).reshape(BLK, HG_DK)
            kd_cols += [first_chunk_only(kd), second_chunk_only(kd)]
            dec = jnp.exp(b_last)
            for c in range(cpb):
                dec_ref[d, i * cpb + c] = dec[c]
        ds_t = _dot(vt, jnp.concatenate(kd_cols, axis=1))
        for d in range(2):
            for c in range(cpb):
                j = d * cpb + c
                ds_ref[d, i * cpb + c] = ds_t[:, j * HG_DK:(j + 1) * HG_DK]
        return carry
    lax.fori_loop(0, n_blocks, hg_prep, 0, unroll=min(4, n_blocks))

    if has_init:
        st0 = tuple(jnp.transpose(s0_ref[0, 0, d, 0]) for d in range(2))
    else:
        st0 = (jnp.zeros((HG_DV, HG_DK), F32),) * 2

    def hg_state(n, carry):
        new = []
        for d in range(2):
            cn = (n_chunks - 1 - n) if d == 1 else n
            stt_ref[d, cn] = carry[d].astype(BF16)
            new.append(dec_ref[d, cn] * carry[d] + ds_ref[d, cn])
        return tuple(new)
    st_fin = lax.fori_loop(0, n_chunks, hg_state, st0)

    def hg_out(i, carry):
        rows = pl.ds(pl.multiple_of(i * BLK, BLK), BLK)
        vt = vt_ref[i]
        lhs, rhs_t = [], []
        for d in range(2):
            att = jnp.where(masks[d], _dot_nt(qa_ref[d, rows, :], ka_ref[d, rows, :]), 0.0)
            lhs += [att.astype(BF16), qx_ref[d, rows, :]]
            rhs_t += [vt] + [stt_ref[d, i * cpb + c] for c in range(cpb)]
        o = _dot_nt(jnp.concatenate(lhs, axis=1), jnp.concatenate(rhs_t, axis=1))
        o = (o * lax.rsqrt(jnp.mean(o * o, axis=-1, keepdims=True) + EPS) * hgn_ref[...]
             * _silu(hgate_ref[rows, :]))
        hg_out_ref[rows, :] = o.astype(BF16)
        return carry
    lax.fori_loop(0, n_blocks, hg_out, 0, unroll=min(4, n_blocks))


    r_i = lax.broadcasted_iota(jnp.int32, (LANES, 4 * LANES), 0)
    c_i = lax.broadcasted_iota(jnp.int32, (LANES, 4 * LANES), 1)
    cblk = jnp.right_shift(c_i, 7)
    target = h_idx + jnp.where(cblk == 0, 0, jnp.where(cblk == 1, 2 * N_HEADS,
                                                       jnp.where(cblk == 2, N_HEADS, 3 * N_HEADS)))
    onehot = jnp.where((jnp.bitwise_and(r_i, GATE_W - 1) == target) & (r_i < 3 * GATE_W), 1.0, 0.0).astype(BF16)
    is_f = jnp.bitwise_and(lane, GATE_W - 1) >= 2 * N_HEADS
    piece = jnp.right_shift(lane, 5)
    sel_rows = min(SEL_ROWS, t_len)

    def ml_select(i, carry):
        rows = pl.ds(pl.multiple_of(i * sel_rows, sel_rows), sel_rows)
        xg = gates_ref[rows, :] + gbias_ref[...]
        xg = jnp.where(is_f, _log_sigmoid(xg), xg)
        hi = xg.astype(BF16).astype(F32)
        rest = xg - hi
        mid = rest.astype(BF16).astype(F32)
        packed = jnp.where(piece == 0, hi, jnp.where(piece == 1, mid, rest - mid))
        sel_ref[rows, :] = _dot(packed.astype(BF16), onehot)
        return carry
    lax.fori_loop(0, t_len // sel_rows, ml_select, 0)

    def ml_prep(i, carry):
        rows = pl.ds(pl.multiple_of(i * BLK, BLK), BLK)
        qk = mqk_ref[rows, :]
        kk = jnp.where(lane_lo, pltpu.roll(qk, ML_DQK, axis=1), qk) * (ML_DQK ** -0.5)
        v = mv_ref[rows, :]
        e_dir = []
        for d in range(2):
            rev = d == 1
            li = sel_ref[rows, (2 * d) * LANES:(2 * d + 1) * LANES]
            lf = sel_ref[rows, (2 * d + 1) * LANES:(2 * d + 2) * LANES]
            g = _chunk_cumsum(lf, in_pos, rev)
            u = li - g
            g_ref[d, rows, :] = g
            ut_ref[d, rows, :] = jnp.transpose(u)
            i_last = 0 if rev else CHUNK - 1
            g_last = _per_chunk(g)[:, i_last:i_last + 1]
            w_end = g_last + _per_chunk(u)
            m_loc = jnp.max(w_end, axis=1, keepdims=True)
            e_dir.append(jnp.exp(w_end - m_loc).reshape(BLK, LANES))
            for c in range(cpb):
                a_ref[d, i * cpb + c] = g_last[c]
                mloc_ref[d, i * cpb + c] = m_loc[c]
        wk = jnp.where(lane_lo, e_dir[0], e_dir[1]) * kk
        v_cols = jnp.concatenate([first_chunk_only(v), second_chunk_only(v)], axis=1)
        c_loc = _dot_tn(wk.astype(BF16), v_cols)
        n_loc = jnp.sum(_per_chunk(wk), axis=1, keepdims=True)
        for c in range(cpb):
            nloc_ref[i * cpb + c] = n_loc[c]
            for d in range(2):
                cloc_ref[d, i * cpb + c] = c_loc[d * ML_DQK:(d + 1) * ML_DQK, c * ML_DV:(c + 1) * ML_DV]
        return carry
    lax.fori_loop(0, n_blocks, ml_prep, 0, unroll=min(4, n_blocks))

    if has_init:
        m0 = tuple(jnp.full((1, LANES), m0_ref[b_idx, d * N_HEADS + h_idx], F32) for d in range(2))
    else:
        m0 = (jnp.zeros((1, LANES), F32),) * 2

    def ml_stab(n, carry):
        new = []
        for d in range(2):
            cn = (n_chunks - 1 - n) if d == 1 else n
            m_st = carry[d]
            a, m_loc = a_ref[d, cn], mloc_ref[d, cn]
            m_new = jnp.maximum(a + m_st, m_loc)
            mst_ref[d, cn] = m_st
            sold_ref[d, cn] = jnp.exp(a + m_st - m_new)
            sloc_ref[d, cn] = jnp.exp(m_loc - m_new)
            new.append(m_new)
        return tuple(new)
    m_fin = lax.fori_loop(0, n_chunks, ml_stab, m0)

    if has_init:
        c0 = tuple(c0_ref[0, 0, d, 0] for d in range(2))
        n0 = tuple(jnp.concatenate([n0_ref[0, 0, d, 0]] * 2, axis=1) for d in range(2))
    else:
        c0 = (jnp.zeros((ML_DQK, ML_DV), F32),) * 2
        n0 = (jnp.zeros((1, LANES), F32),) * 2

    def ml_state(n, carry):
        cs, ns = carry
        new_c, new_n = [], []
        for d in range(2):
            cn = (n_chunks - 1 - n) if d == 1 else n
            cst_ref[d, cn] = cs[d].astype(BF16)
            nst_ref[d, cn] = ns[d]
            s_old, s_loc = sold_ref[d, cn], sloc_ref[d, cn]
            new_c.append(s_old * cs[d] + s_loc * cloc_ref[d, cn])
            new_n.append(s_old * ns[d] + s_loc * nloc_ref[cn])
        return tuple(new_c), tuple(new_n)
    c_fin, n_fin = lax.fori_loop(0, n_chunks, ml_state, (c0, n0))

    def ml_out(i, carry):
        rows = pl.ds(pl.multiple_of(i * BLK, BLK), BLK)
        chunks = pl.ds(i * cpb, cpb)
        qk = mqk_ref[rows, :]
        qb = qk[:, :ML_DQK].astype(BF16)
        kb = (qk[:, ML_DQK:] * (ML_DQK ** -0.5)).astype(BF16)
        vb = mv_ref[rows, :].astype(BF16)
        qk_raw = _dot_nt(qb, kb)
        qq = jnp.where(lane_lo, qk, pltpu.roll(qk, ML_DQK, axis=1))
        q_cols = jnp.where(lane_lo == row_lo, qq, 0.0).astype(BF16)
        c_rows = jnp.concatenate([cst_ref[d, chunks].reshape(BLK, ML_DV) for d in range(2)], axis=1)
        qc = _dot(q_cols, c_rows)
        parts = []
        for d in range(2):
            g = g_ref[d, rows, :]
            log_inter = (_per_chunk(g) + mst_ref[d, chunks]).reshape(BLK, LANES)
            log_d = jnp.where(masks[d], g + ut_ref[d, rows, :], -jnp.inf)
            m_t = jnp.maximum(log_inter, jnp.max(log_d, axis=-1, keepdims=True))
            s_inter = jnp.exp(log_inter - m_t)
            s_qk = qk_raw * jnp.exp(log_d - m_t)
            own_half = lane_lo if d == 0 else jnp.logical_not(lane_lo)
            q_n = jnp.where(own_half, (_per_chunk(qq) * nst_ref[d, chunks]).reshape(BLK, LANES), 0.0)
            den = jnp.sum(s_qk + s_inter * q_n, axis=-1, keepdims=True)
            parts.append((s_qk, s_inter, jnp.maximum(jnp.abs(den), jnp.exp(-m_t[:, 0:1]))))
        nv = _dot(jnp.concatenate([parts[0][0], parts[1][0]], axis=0).astype(BF16), vb)
        o = None
        for d in range(2):
            _, s_inter, den = parts[d]
            h = (nv[d * BLK:(d + 1) * BLK] + s_inter * qc[:, d * ML_DV:(d + 1) * ML_DV]) / den
            o = h if o is None else o + h
        hsum_ref[rows, :] = o
        return carry
    lax.fori_loop(0, n_blocks, ml_out, 0, unroll=min(2, n_blocks))

    def ml_norm(i, carry):
        rows = pl.ds(pl.multiple_of(i * BLK, BLK), BLK)
        o = hsum_ref[rows, :]
        mu = jnp.mean(o, axis=-1, keepdims=True)
        oc = o - mu
        var = jnp.mean(oc * oc, axis=-1, keepdims=True)
        o = oc * lax.rsqrt(var + EPS) * mln_ref[...] * _sigmoid(mo_ref[rows, :])
        ml_out_ref[rows, :] = o.astype(BF16)
        return carry
    lax.fori_loop(0, n_blocks, ml_norm, 0, unroll=min(4, n_blocks))

    if not has_init:
        for d in range(2):
            sfin_ref[0, d, 0] = jnp.transpose(st_fin[d])
            cfin_ref[0, d, 0] = c_fin[d]
            nfin_ref[0, d, 0] = n_fin[d][:, d * ML_DQK:(d + 1) * ML_DQK]
            mfin_ref[0, d, 0] = m_fin[d]


def _scan(proj, gates, lb_logits, hg_norm_g, gate_bias, ml_norm_g, init, *, bsz, t_len):
    has_init = init is not None
    m = bsz * t_len
    n_chunks = t_len // CHUNK
    n_blocks = t_len // BLK

    def col(sec):
        return pl.BlockSpec((t_len, HEAD_W), lambda b, h, sec=sec: (b, sec + h))

    in_specs = [col(SEC_HQ), col(SEC_HF_FW), col(SEC_HF_BW), col(SEC_HI), col(SEC_HGATE),
                col(SEC_MQK), col(SEC_MV), col(SEC_MO),
                pl.BlockSpec((t_len, LANES), lambda b, h: (b, 0)),
                pl.BlockSpec((2, 2, HEAD_W), lambda b, h: (0, 0, h)),
                pl.BlockSpec((1, HG_DV), lambda b, h: (0, 0)),
                pl.BlockSpec((1, LANES), lambda b, h: (0, 0)),
                pl.BlockSpec((1, ML_DV), lambda b, h: (0, h))]
    args = [proj] * 8 + [gates, lb_logits, hg_norm_g, gate_bias, ml_norm_g]
    out_specs = [pl.BlockSpec((t_len, HEAD_W), lambda b, h: (b, h)),
                 pl.BlockSpec((t_len, HEAD_W), lambda b, h: (b, h))]
    out_shape = [jax.ShapeDtypeStruct((m, N_HEADS * HG_DV), BF16),
                 jax.ShapeDtypeStruct((m, N_HEADS * ML_DV), BF16)]
    if has_init:
        s0, c0, n0, m0 = init
        in_specs += [pl.BlockSpec((1, 1, 2, 1, HG_DK, HG_DV), lambda b, h: (b, 0, 0, h, 0, 0)),
                     pl.BlockSpec((1, 1, 2, 1, ML_DQK, ML_DV), lambda b, h: (b, 0, 0, h, 0, 0)),
                     pl.BlockSpec((1, 1, 2, 1, 1, ML_DQK), lambda b, h: (b, 0, 0, h, 0, 0)),
                     pl.BlockSpec(memory_space=pltpu.SMEM)]
        args += [s0, c0, n0.reshape(n0.shape[:4] + (1, ML_DQK)), m0.reshape(bsz, 2 * N_HEADS)]
    else:
        out_specs += [pl.BlockSpec((1, 2, 1, HG_DK, HG_DV), lambda b, h: (b, 0, h, 0, 0)),
                      pl.BlockSpec((1, 2, 1, ML_DQK, ML_DV), lambda b, h: (b, 0, h, 0, 0)),
                      pl.BlockSpec((1, 2, 1, 1, ML_DQK), lambda b, h: (b, 0, h, 0, 0)),
                      pl.BlockSpec((1, 2, 1, 1, LANES), lambda b, h: (b, 0, h, 0, 0))]
        out_shape += [jax.ShapeDtypeStruct((bsz, 2, N_HEADS, HG_DK, HG_DV), F32),
                      jax.ShapeDtypeStruct((bsz, 2, N_HEADS, ML_DQK, ML_DV), F32),
                      jax.ShapeDtypeStruct((bsz, 2, N_HEADS, 1, ML_DQK), F32),
                      jax.ShapeDtypeStruct((bsz, 2, N_HEADS, 1, LANES), F32)]

    def per_chunk_row(*lead):
        return pltpu.VMEM(lead + (n_chunks, 1, LANES), F32)

    scratch = [pltpu.VMEM((2, t_len, HG_DK), BF16),
               pltpu.VMEM((2, t_len, HG_DK), BF16),
               pltpu.VMEM((2, t_len, 2 * HG_DK), BF16),
               pltpu.VMEM((n_blocks, HG_DV, BLK), BF16),
               pltpu.VMEM((2, n_chunks, HG_DV, HG_DK), F32),
               per_chunk_row(2),
               pltpu.VMEM((2, n_chunks, HG_DV, HG_DK), BF16),
               pltpu.VMEM((t_len, 4 * LANES), F32),
               pltpu.VMEM((2, t_len, LANES), F32),
               pltpu.VMEM((2, t_len, BLK), F32),
               pltpu.VMEM((2, n_chunks, ML_DQK, ML_DV), F32),
               per_chunk_row(),
               per_chunk_row(2),
               per_chunk_row(2),
               per_chunk_row(2),
               per_chunk_row(2),
               per_chunk_row(2),
               pltpu.VMEM((2, n_chunks, ML_DQK, ML_DV), BF16),
               per_chunk_row(2),
               pltpu.VMEM((t_len, ML_DV), F32)]
    kern = functools.partial(_scan_kernel, t_len=t_len, has_init=has_init)
    return pl.pallas_call(
        kern,
        grid=(bsz, N_HEADS),
        in_specs=in_specs,
        out_specs=out_specs,
        out_shape=out_shape,
        scratch_shapes=scratch,
        compiler_params=pltpu.CompilerParams(dimension_semantics=("arbitrary", "arbitrary"),
                                             vmem_limit_bytes=VMEM_LIMIT),
        name="scan_init" if has_init else "scan_ctx",
    )(*args)


def _mix_out_kernel(hg_ref, ml_ref, ga_ref, gb_ref, x_ref, gt_ref, sh_ref, sc_ref, gpost_ref, gpre_ref,
                    wuh_ref, wum_ref, wo_ref, x1_ref, h2_ref):
    y_hg = _dot(hg_ref[...], wuh_ref[...])
    y_ml = _dot(ml_ref[...], wum_ref[...])
    merged = _sigmoid(ga_ref[...]) * y_hg + _sigmoid(gb_ref[...]) * y_ml
    y = _dot(merged.astype(BF16), wo_ref[...])
    x1 = x_ref[...] + gt_ref[0] * _rms_rows(y, gpost_ref[...])
    x1_ref[...] = x1
    h2 = _rms_rows(x1, gpre_ref[...]) * (1.0 + sc_ref[0]) + sh_ref[0]
    h2_ref[...] = h2.astype(BF16)


def _mix_out(hg_act, ml_act, proj, x2d, mod3, mod_row_fn, g_post_mix, g_pre_ffn, w_up_hg, w_up_ml, w_out, *, tm):
    m = x2d.shape[0]

    def resident(shape):
        return pl.BlockSpec(shape, lambda i: (0, 0), pipeline_mode=pl.Buffered(1))

    return pl.pallas_call(
        _mix_out_kernel,
        grid=(m // tm,),
        in_specs=[pl.BlockSpec((tm, N_HEADS * HG_DV), lambda i: (i, 0)),
                  pl.BlockSpec((tm, N_HEADS * ML_DV), lambda i: (i, 0)),
                  pl.BlockSpec((tm, D_MODEL), lambda i: (i, SEC_GA * LANES // D_MODEL)),
                  pl.BlockSpec((tm, D_MODEL), lambda i: (i, SEC_GB * LANES // D_MODEL)),
                  pl.BlockSpec((tm, D_MODEL), lambda i: (i, 0)),
                  pl.BlockSpec((1, 1, D_MODEL), lambda i: (mod_row_fn(i), 0, 2)),
                  pl.BlockSpec((1, 1, D_MODEL), lambda i: (mod_row_fn(i), 0, 3)),
                  pl.BlockSpec((1, 1, D_MODEL), lambda i: (mod_row_fn(i), 0, 4)),
                  resident((1, D_MODEL)),
                  resident((1, D_MODEL)),
                  resident((N_HEADS * HG_DV, D_MODEL)),
                  resident((N_HEADS * ML_DV, D_MODEL)),
                  resident((D_MODEL, D_MODEL))],
        out_specs=[pl.BlockSpec((tm, D_MODEL), lambda i: (i, 0)),
                   pl.BlockSpec((tm, D_MODEL), lambda i: (i, 0))],
        out_shape=[jax.ShapeDtypeStruct((m, D_MODEL), F32),
                   jax.ShapeDtypeStruct((m, D_MODEL), BF16)],
        compiler_params=pltpu.CompilerParams(dimension_semantics=("arbitrary",),
                                             vmem_limit_bytes=VMEM_LIMIT),
        name="mix_out",
    )(hg_act, ml_act, proj, proj, x2d, mod3, mod3, mod3, g_post_mix, g_pre_ffn, w_up_hg, w_up_ml, w_out)


def _ffn_kernel(h2_ref, x1_ref, gt_ref, gpost_ref, wa_ref, wb_ref, wo_ref, out_ref, acc_ref):
    j = pl.program_id(1)
    h2 = h2_ref[...]
    a = _dot(h2, wa_ref[...])
    b = _dot(h2, wb_ref[...])
    part = _dot((_silu(a) * b).astype(BF16), wo_ref[...])

    @pl.when(j == 0)
    def _():
        acc_ref[...] = part

    @pl.when(j > 0)
    def _():
        acc_ref[...] += part

    @pl.when(j == pl.num_programs(1) - 1)
    def _():
        out_ref[...] = x1_ref[...] + gt_ref[0] * _rms_rows(acc_ref[...], gpost_ref[...])


def _ffn(h2, x1, mod3, mod_row_fn, g_post_ffn, w_ffn_in, w_ffn_out, *, tm, tf):
    m = x1.shape[0]
    nf = D_FF // tf
    return pl.pallas_call(
        _ffn_kernel,
        grid=(m // tm, nf),
        in_specs=[pl.BlockSpec((tm, D_MODEL), lambda i, j: (i, 0)),
                  pl.BlockSpec((tm, D_MODEL), lambda i, j: (i, 0)),
                  pl.BlockSpec((1, 1, D_MODEL), lambda i, j: (mod_row_fn(i), 0, 5)),
                  pl.BlockSpec((1, D_MODEL), lambda i, j: (0, 0)),
                  pl.BlockSpec((D_MODEL, tf), lambda i, j: (0, j)),
                  pl.BlockSpec((D_MODEL, tf), lambda i, j, nf=nf: (0, j + nf)),
                  pl.BlockSpec((tf, D_MODEL), lambda i, j: (j, 0))],
        out_specs=pl.BlockSpec((tm, D_MODEL), lambda i, j: (i, 0)),
        out_shape=jax.ShapeDtypeStruct((m, D_MODEL), F32),
        scratch_shapes=[pltpu.VMEM((tm, D_MODEL), F32)],
        compiler_params=pltpu.CompilerParams(dimension_semantics=("arbitrary", "arbitrary"),
                                             vmem_limit_bytes=VMEM_LIMIT),
        name="ffn",
    )(h2, x1, mod3, g_post_ffn, w_ffn_in, w_ffn_in, w_ffn_out)


def _prep_w_in(w):
    o = 0
    parts = {}
    for name, width in (("hg", 5 * 1024), ("mq", 512), ("mk", 512), ("mv", 1024), ("mi", 16), ("mf", 16),
                        ("mo", 1024), ("gab", 2 * D_MODEL)):
        parts[name] = w[:, o:o + width]
        o += width
    mqk = jnp.concatenate([parts["mq"].reshape(D_MODEL, N_HEADS, ML_DQK),
                           parts["mk"].reshape(D_MODEL, N_HEADS, ML_DQK)], axis=-1).reshape(D_MODEL, 2 * 512)
    w_main = jnp.concatenate([parts["hg"], mqk, parts["mv"], parts["mo"], parts["gab"]], axis=1).astype(BF16)
    w_gate = jnp.tile(jnp.concatenate([parts["mi"], parts["mf"]], axis=1), (1, LANES // GATE_W)).astype(BF16)
    return w_main, w_gate


def _layer_group(x, mod3, mod_row_fn, init, p, *, tm_in, tn_in, tm_mix, tm_ffn, tf):
    bsz, t_len, _ = x.shape
    x2d = x.reshape(bsz * t_len, D_MODEL)
    proj, gates = _in_proj(x2d, mod3, lambda i: mod_row_fn(i, tm_in), p["g_pre_mix"], p["w_main"], p["w_gate"],
                           tm=tm_in, tn=tn_in)
    outs = _scan(proj, gates, p["lb_logits"], p["hg_norm_g"], p["gate_bias"], p["ml_norm_g"], init,
                 bsz=bsz, t_len=t_len)
    hg_act, ml_act = outs[0], outs[1]
    x1, h2 = _mix_out(hg_act, ml_act, proj, x2d, mod3, lambda i: mod_row_fn(i, tm_mix), p["g_post_mix"],
                      p["g_pre_ffn"], p["w_up_hg"], p["w_up_ml"], p["w_out"], tm=tm_mix)
    y = _ffn(h2, x1, mod3, lambda i: mod_row_fn(i, tm_ffn), p["g_post_ffn"], p["w_ffn_in"], p["w_ffn_out"],
             tm=tm_ffn, tf=tf)
    return y.reshape(bsz, t_len, D_MODEL), outs[2:]


def kernel(x_prompt, x_sample, c, state_hgrn_s, state_mlstm_c, state_mlstm_n, state_mlstm_m, c_ctx, w_mod, b_mod,
           norm_pre_mix, norm_post_mix, norm_pre_ffn, norm_post_ffn, w_in, hgrn_lb_logits, hgrn_norm_g, mlstm_b_i,
           mlstm_b_f, mlstm_norm_g, w_up_hgrn, w_up_mlstm, w_out, w_ffn_in, w_ffn_out):
    bsz_p, t_p, _ = x_prompt.shape
    bsz_s, t_s, _ = x_sample.shape

    cond = jnp.concatenate([c_ctx[None, :], c, jnp.zeros((MOD_ROWS - 1 - bsz_s, D_MODEL), F32)], axis=0)
    mod = _modulation(cond, w_mod[0], b_mod[0][None, :])
    mod3 = mod.reshape(MOD_ROWS, 1, N_MOD)

    w_main, w_gate = _prep_w_in(w_in[0])
    p = dict(
        w_main=w_main, w_gate=w_gate,
        g_pre_mix=norm_pre_mix[0][None, :], g_post_mix=norm_post_mix[0][None, :],
        g_pre_ffn=norm_pre_ffn[0][None, :], g_post_ffn=norm_post_ffn[0][None, :],
        lb_logits=hgrn_lb_logits, hg_norm_g=hgrn_norm_g[0][None, :],
        gate_bias=jnp.tile(jnp.concatenate([mlstm_b_i[0], mlstm_b_f[0]]), LANES // GATE_W)[None, :],
        ml_norm_g=mlstm_norm_g[0][None, :],
        w_up_hg=w_up_hgrn[0].astype(BF16), w_up_ml=w_up_mlstm[0].astype(BF16), w_out=w_out[0].astype(BF16),
        w_ffn_in=w_ffn_in[0].astype(BF16), w_ffn_out=w_ffn_out[0].astype(BF16),
    )
    tiles_p = dict(tm_in=min(1024, bsz_p * t_p), tn_in=1024, tm_mix=256, tm_ffn=min(512, bsz_p * t_p), tf=512)
    tiles_s = dict(tm_in=min(1024, t_s), tn_in=1024, tm_mix=min(256, t_s), tm_ffn=min(512, t_s), tf=512)
    y_p, ctx_states = _layer_group(x_prompt, mod3, lambda i, tm: 0, None, p, **tiles_p)
    init = (state_hgrn_s, state_mlstm_c, state_mlstm_n, state_mlstm_m[:, 0])
    y_s, _ = _layer_group(x_sample, mod3, lambda i, tm: 1 + (i * tm) // t_s, init, p, **tiles_s)

    s_fin, c_fin, n_fin, m_fin = ctx_states
    new_hgrn_s = s_fin[:, None]
    new_mlstm_c = c_fin[:, None]
    new_mlstm_n = n_fin[:, None, :, :, 0, :]
    new_mlstm_m = m_fin[:, None, :, :, 0, 0]
    return (y_p, y_s, new_hgrn_s, new_mlstm_c, new_mlstm_n, new_mlstm_m)
```

```python
import functools

import jax
import jax.numpy as jnp
from jax import lax
from jax.experimental import pallas as pl
from jax.experimental.pallas import tpu as pltpu

F32 = jnp.float32
BF16 = jnp.bfloat16

D_MODEL = 2048
N_HEADS = 8
HG_DK = 128
HG_DV = 128
ML_DQK = 64
ML_DV = 128
HEAD_W = 128
D_FF = 5632
CHUNK = 64
EPS = 1e-6
N_MOD = 6 * D_MODEL
LANES = 128
MOD_ROWS = 16
BLK = 2 * CHUNK
SEL_ROWS = 512
FFN_ACC_COLS = 512
GATE_W = 4 * N_HEADS

SEC_HQ, SEC_HF_FW, SEC_HF_BW, SEC_HI, SEC_HGATE = 0, 8, 16, 24, 32
SEC_MQK, SEC_MV, SEC_MO = 40, 48, 56
SEC_GA, SEC_GB = 64, 80
PROJ_W = 96 * LANES

W_MQ = 5 * N_HEADS * HG_DK
W_MK = W_MQ + N_HEADS * ML_DQK
W_MV = W_MK + N_HEADS * ML_DQK
W_GATES = W_MV + N_HEADS * ML_DV
IN_W = W_GATES + 4 * N_HEADS + N_HEADS * ML_DV + 2 * D_MODEL

VMEM_LIMIT = 56 * 1024 * 1024


def _dot(a, b):
    return jnp.dot(a, b, preferred_element_type=F32)


def _dot_nt(a, b):
    return lax.dot_general(a, b, (((1,), (1,)), ((), ())), preferred_element_type=F32)


def _dot_tn(a, b):
    return lax.dot_general(a, b, (((0,), (0,)), ((), ())), preferred_element_type=F32)


def _sigmoid_pair(z):
    e = jnp.exp(-jnp.abs(z))
    r = 1.0 / (1.0 + e)
    er = e * r
    pos = z >= 0
    return jnp.where(pos, r, er), jnp.where(pos, er, r)


def _sigmoid(z):
    return _sigmoid_pair(z)[0]


def _silu(z):
    return z * _sigmoid(z)


def _log_sigmoid(z):
    return jnp.minimum(z, 0.0) - jnp.log1p(jnp.exp(-jnp.abs(z)))


def _mod_kernel(c_ref, w_ref, b_ref, o_ref):
    a = _silu(c_ref[...]).astype(BF16)
    o_ref[...] = _dot(a, w_ref[...].astype(BF16)) + b_ref[...]


def _modulation(cond, w_mod, b_mod):
    tn = 1024
    return pl.pallas_call(
        _mod_kernel,
        grid=(N_MOD // tn,),
        in_specs=[pl.BlockSpec((MOD_ROWS, D_MODEL), lambda j: (0, 0)),
                  pl.BlockSpec((D_MODEL, tn), lambda j: (0, j)),
                  pl.BlockSpec((1, tn), lambda j: (0, j))],
        out_specs=pl.BlockSpec((MOD_ROWS, tn), lambda j: (0, j)),
        out_shape=jax.ShapeDtypeStruct((MOD_ROWS, N_MOD), F32),
        compiler_params=pltpu.CompilerParams(dimension_semantics=("arbitrary",),
                                             vmem_limit_bytes=VMEM_LIMIT),
        name="modulation",
    )(cond, w_mod, b_mod)


def _rms_rows(x, g):
    return x * lax.rsqrt(jnp.mean(x * x, axis=-1, keepdims=True) + EPS) * g


def _inproj_kernel(x_ref, sh_ref, sc_ref, g_ref, w_ref, wg_ref, proj_ref, gates_ref, xn_ref, *, tm):
    rows = 128

    @pl.when(pl.program_id(1) == 0)
    def _():
        def body(i, carry):
            r0 = pl.multiple_of(i * rows, rows)
            x = x_ref[pl.ds(r0, rows), :]
            h = _rms_rows(x, g_ref[...]) * (1.0 + sc_ref[0]) + sh_ref[0]
            hb = h.astype(BF16)
            xn_ref[pl.ds(r0, rows), :] = hb
            gates_ref[pl.ds(r0, rows), :] = _dot(hb, wg_ref[...])
            return carry
        lax.fori_loop(0, tm // rows, body, 0)

    proj_ref[...] = _dot(xn_ref[...], w_ref[...])


def _in_proj(x2d, mod3, mod_row_fn, g_pre, w_main, w_gate, *, tm, tn):
    m = x2d.shape[0]
    kern = functools.partial(_inproj_kernel, tm=tm)
    return pl.pallas_call(
        kern,
        grid=(m // tm, PROJ_W // tn),
        in_specs=[pl.BlockSpec((tm, D_MODEL), lambda i, j: (i, 0)),
                  pl.BlockSpec((1, 1, D_MODEL), lambda i, j: (mod_row_fn(i), 0, 0)),
                  pl.BlockSpec((1, 1, D_MODEL), lambda i, j: (mod_row_fn(i), 0, 1)),
                  pl.BlockSpec((1, D_MODEL), lambda i, j: (0, 0)),
                  pl.BlockSpec((D_MODEL, tn), lambda i, j: (0, j)),
                  pl.BlockSpec((D_MODEL, LANES), lambda i, j: (0, 0))],
        out_specs=[pl.BlockSpec((tm, tn), lambda i, j: (i, j)),
                   pl.BlockSpec((tm, LANES), lambda i, j: (i, 0))],
        out_shape=[jax.ShapeDtypeStruct((m, PROJ_W), F32),
                   jax.ShapeDtypeStruct((m, LANES), F32)],
        scratch_shapes=[pltpu.VMEM((tm, D_MODEL), BF16)],
        compiler_params=pltpu.CompilerParams(dimension_semantics=("arbitrary", "arbitrary"),
                                             vmem_limit_bytes=VMEM_LIMIT),
        name="in_proj",
    )(x2d, mod3, mod3, g_pre, w_main, w_gate)


def _chunk_cumsum(x, pos, rev):
    n = x.shape[0]
    s = 1
    while s < CHUNK:
        if rev:
            x = x + jnp.where(pos < CHUNK - s, pltpu.roll(x, n - s, axis=0), 0.0)
        else:
            x = x + jnp.where(pos >= s, pltpu.roll(x, s, axis=0), 0.0)
        s *= 2
    return x


def _per_chunk(x):
    return x.reshape(BLK // CHUNK, CHUNK, x.shape[-1])


def _scan_kernel(*refs, t_len, has_init):
    (hq_ref, hff_ref, hfb_ref, hi_ref, hgate_ref, mqk_ref, mv_ref, mo_ref, gates_ref,
     lbl_ref, hgn_ref, gbias_ref, mln_ref) = refs[:13]
    pos = 13
    if has_init:
        s0_ref, c0_ref, n0_ref, m0_ref = refs[pos:pos + 4]
        pos += 4
    hg_out_ref, ml_out_ref = refs[pos:pos + 2]
    pos += 2
    if not has_init:
        sfin_ref, cfin_ref, nfin_ref, mfin_ref = refs[pos:pos + 4]
        pos += 4
    (qa_ref, ka_ref, qx_ref, vt_ref, ds_ref, dec_ref, stt_ref,
     sel_ref, g_ref, ut_ref, cloc_ref, nloc_ref, a_ref, mloc_ref, mst_ref, sold_ref, sloc_ref,
     cst_ref, nst_ref, hsum_ref, pk_ref) = refs[pos:]

    b_idx = pl.program_id(0)
    h_idx = pl.program_id(1)
    n_chunks = t_len // CHUNK
    n_blocks = t_len // BLK
    cpb = BLK // CHUNK
    mid = CHUNK // 2

    row = lax.broadcasted_iota(jnp.int32, (BLK, BLK), 0)
    col = lax.broadcasted_iota(jnp.int32, (BLK, BLK), 1)
    same_chunk = jnp.right_shift(row, 6) == jnp.right_shift(col, 6)
    masks = (same_chunk & (col <= row), same_chunk & (col >= row))
    in_pos = jnp.bitwise_and(row, CHUNK - 1)
    row_lo = row < CHUNK
    lane_lo = col < ML_DQK
    lane = lax.broadcasted_iota(jnp.int32, (1, LANES), 1)

    def first_chunk_only(x):
        return jnp.concatenate([x[:CHUNK], jnp.zeros_like(x[CHUNK:])], axis=0).astype(BF16)

    def second_chunk_only(x):
        return jnp.concatenate([jnp.zeros_like(x[:CHUNK]), x[CHUNK:]], axis=0).astype(BF16)

    logits = lbl_ref[...]
    l0, l1 = logits[:, 0, :], logits[:, 1, :]
    mx = jnp.maximum(l0, l1)
    e0, e1 = jnp.exp(l0 - mx), jnp.exp(l1 - mx)
    lb_all = e0 / (e0 + e1)


    def hg_prep(i, carry):
        rows = pl.ds(pl.multiple_of(i * BLK, BLK), BLK)
        q = _silu(hq_ref[rows, :])
        vt = jnp.transpose(hi_ref[rows, :]).astype(BF16)
        vt_ref[i] = vt
        kd_cols = []
        for d, zf_ref in enumerate((hff_ref, hfb_ref)):
            rev = d == 1
            lb = lb_all[d:d + 1]
            oml = 1.0 - lb
            sp, sn = _sigmoid_pair(zf_ref[rows, :])
            k = oml * sn
            b = _chunk_cumsum(jnp.log(lb + oml * sp), in_pos, rev)
            b3, q3, k3 = _per_chunk(b), _per_chunk(q), _per_chunk(k)
            i_mid, i_last = (CHUNK - 1 - mid, 0) if rev else (mid, CHUNK - 1)
            b_mid, b_last = b3[:, i_mid:i_mid + 1], b3[:, i_last:i_last + 1]
            qa_ref[d, rows, :] = (q3 * jnp.exp(b3 - b_mid)).reshape(BLK, HG_DK).astype(BF16)
            ka_ref[d, rows, :] = (k3 * jnp.exp(b_mid - b3)).reshape(BLK, HG_DK).astype(BF16)
            qi = q * jnp.exp(b)
            qx_ref[d, rows, :HG_DK] = first_chunk_only(qi)
            qx_ref[d, rows, HG_DK:] = second_chunk_only(qi)
            kd = (k3 * jnp.exp(b_last - b3)).reshape(BLK, HG_DK)
            kd_cols += [first_chunk_only(kd), second_chunk_only(kd)]
            dec = jnp.exp(b_last)
            for c in range(cpb):
                dec_ref[d, i * cpb + c] = dec[c]
        ds_t = _dot(vt, jnp.concatenate(kd_cols, axis=1))
        for d in range(2):
            for c in range(cpb):
                j = d * cpb + c
                ds_ref[d, i * cpb + c] = ds_t[:, j * HG_DK:(j + 1) * HG_DK]
        return carry
    lax.fori_loop(0, n_blocks, hg_prep, 0, unroll=min(4, n_blocks))

    if has_init:
        st0 = tuple(jnp.transpose(s0_ref[0, 0, d, 0]) for d in range(2))
    else:
        st0 = (jnp.zeros((HG_DV, HG_DK), F32),) * 2

    def hg_state(n, carry):
        new = []
        for d in range(2):
            cn = (n_chunks - 1 - n) if d == 1 else n
            stt_ref[d, cn] = carry[d].astype(BF16)
            new.append(dec_ref[d, cn] * carry[d] + ds_ref[d, cn])
        return tuple(new)
    st_fin = lax.fori_loop(0, n_chunks, hg_state, st0)

    def hg_out(i, carry):
        rows = pl.ds(pl.multiple_of(i * BLK, BLK), BLK)
        vt = vt_ref[i]
        lhs, rhs_t = [], []
        for d in range(2):
            att = jnp.where(masks[d], _dot_nt(qa_ref[d, rows, :], ka_ref[d, rows, :]), 0.0)
            lhs += [att.astype(BF16), qx_ref[d, rows, :]]
            rhs_t += [vt] + [stt_ref[d, i * cpb + c] for c in range(cpb)]
        o = _dot_nt(jnp.concatenate(lhs, axis=1), jnp.concatenate(rhs_t, axis=1))
        o = (o * lax.rsqrt(jnp.mean(o * o, axis=-1, keepdims=True) + EPS) * hgn_ref[...]
             * _silu(hgate_ref[rows, :]))
        hg_out_ref[rows, :] = o.astype(BF16)
        return carry
    lax.fori_loop(0, n_blocks, hg_out, 0, unroll=min(4, n_blocks))


    r_i = lax.broadcasted_iota(jnp.int32, (LANES, 4 * LANES), 0)
    c_i = lax.broadcasted_iota(jnp.int32, (LANES, 4 * LANES), 1)
    cblk = jnp.right_shift(c_i, 7)
    target = h_idx + jnp.where(cblk == 0, 0, jnp.where(cblk == 1, 2 * N_HEADS,
                                                       jnp.where(cblk == 2, N_HEADS, 3 * N_HEADS)))
    onehot = jnp.where((jnp.bitwise_and(r_i, GATE_W - 1) == target) & (r_i < 3 * GATE_W), 1.0, 0.0).astype(BF16)
    is_f = jnp.bitwise_and(lane, GATE_W - 1) >= 2 * N_HEADS
    piece = jnp.right_shift(lane, 5)
    sel_rows = min(SEL_ROWS, t_len)

    @pl.when(h_idx == 0)
    def _():
        def ml_pack(i, carry):
            rows = pl.ds(pl.multiple_of(i * sel_rows, sel_rows), sel_rows)
            xg = gates_ref[rows, :] + gbias_ref[...]
            xg = jnp.where(is_f, _log_sigmoid(xg), xg)
            hi = xg.astype(BF16).astype(F32)
            rest = xg - hi
            mid = rest.astype(BF16).astype(F32)
            packed = jnp.where(piece == 0, hi, jnp.where(piece == 1, mid, rest - mid))
            pk_ref[rows, :] = packed.astype(BF16)
            return carry
        lax.fori_loop(0, t_len // sel_rows, ml_pack, 0)

    def ml_select(i, carry):
        rows = pl.ds(pl.multiple_of(i * sel_rows, sel_rows), sel_rows)
        sel_ref[rows, :] = _dot(pk_ref[rows, :], onehot)
        return carry
    lax.fori_loop(0, t_len // sel_rows, ml_select, 0)

    def ml_prep(i, carry):
        rows = pl.ds(pl.multiple_of(i * BLK, BLK), BLK)
        qk = mqk_ref[rows, :]
        kk = jnp.where(lane_lo, pltpu.roll(qk, ML_DQK, axis=1), qk) * (ML_DQK ** -0.5)
        v = mv_ref[rows, :]
        e_dir = []
        for d in range(2):
            rev = d == 1
            li = sel_ref[rows, (2 * d) * LANES:(2 * d + 1) * LANES]
            lf = sel_ref[rows, (2 * d + 1) * LANES:(2 * d + 2) * LANES]
            g = _chunk_cumsum(lf, in_pos, rev)
            u = li - g
            g_ref[d, rows, :] = g
            ut_ref[d, rows, :] = jnp.transpose(u)
            i_last = 0 if rev else CHUNK - 1
            g_last = _per_chunk(g)[:, i_last:i_last + 1]
            w_end = g_last + _per_chunk(u)
            m_loc = jnp.max(w_end, axis=1, keepdims=True)
            e_dir.append(jnp.exp(w_end - m_loc).reshape(BLK, LANES))
            for c in range(cpb):
                a_ref[d, i * cpb + c] = g_last[c]
                mloc_ref[d, i * cpb + c] = m_loc[c]
        wk = jnp.where(lane_lo, e_dir[0], e_dir[1]) * kk
        v_cols = jnp.concatenate([first_chunk_only(v), second_chunk_only(v)], axis=1)
        c_loc = _dot_tn(wk.astype(BF16), v_cols)
        n_loc = jnp.sum(_per_chunk(wk), axis=1, keepdims=True)
        for c in range(cpb):
            nloc_ref[i * cpb + c] = n_loc[c]
            for d in range(2):
                cloc_ref[d, i * cpb + c] = c_loc[d * ML_DQK:(d + 1) * ML_DQK, c * ML_DV:(c + 1) * ML_DV]
        return carry
    lax.fori_loop(0, n_blocks, ml_prep, 0, unroll=min(4, n_blocks))

    if has_init:
        m0 = tuple(jnp.full((1, LANES), m0_ref[b_idx, d * N_HEADS + h_idx], F32) for d in range(2))
    else:
        m0 = (jnp.zeros((1, LANES), F32),) * 2

    def ml_stab(n, carry):
        new = []
        for d in range(2):
            cn = (n_chunks - 1 - n) if d == 1 else n
            m_st = carry[d]
            a, m_loc = a_ref[d, cn], mloc_ref[d, cn]
            m_new = jnp.maximum(a + m_st, m_loc)
            mst_ref[d, cn] = m_st
            sold_ref[d, cn] = jnp.exp(a + m_st - m_new)
            sloc_ref[d, cn] = jnp.exp(m_loc - m_new)
            new.append(m_new)
        return tuple(new)
    m_fin = lax.fori_loop(0, n_chunks, ml_stab, m0)

    if has_init:
        c0 = tuple(c0_ref[0, 0, d, 0] for d in range(2))
        n0 = tuple(jnp.concatenate([n0_ref[0, 0, d, 0]] * 2, axis=1) for d in range(2))
    else:
        c0 = (jnp.zeros((ML_DQK, ML_DV), F32),) * 2
        n0 = (jnp.zeros((1, LANES), F32),) * 2

    def ml_state(n, carry):
        cs, ns = carry
        new_c, new_n = [], []
        for d in range(2):
            cn = (n_chunks - 1 - n) if d == 1 else n
            cst_ref[d, cn] = cs[d].astype(BF16)
            nst_ref[d, cn] = ns[d]
            s_old, s_loc = sold_ref[d, cn], sloc_ref[d, cn]
            new_c.append(s_old * cs[d] + s_loc * cloc_ref[d, cn])
            new_n.append(s_old * ns[d] + s_loc * nloc_ref[cn])
        return tuple(new_c), tuple(new_n)
    c_fin, n_fin = lax.fori_loop(0, n_chunks, ml_state, (c0, n0))

    def ml_out(i, carry):
        rows = pl.ds(pl.multiple_of(i * BLK, BLK), BLK)
        chunks = pl.ds(i * cpb, cpb)
        qk = mqk_ref[rows, :]
        qb = qk[:, :ML_DQK].astype(BF16)
        kb = (qk[:, ML_DQK:] * (ML_DQK ** -0.5)).astype(BF16)
        vb = mv_ref[rows, :].astype(BF16)
        qk_raw = _dot_nt(qb, kb)
        qq = jnp.where(lane_lo, qk, pltpu.roll(qk, ML_DQK, axis=1))
        q_cols = jnp.where(lane_lo == row_lo, qq, 0.0).astype(BF16)
        c_rows = jnp.concatenate([cst_ref[d, chunks].reshape(BLK, ML_DV) for d in range(2)], axis=1)
        qc = _dot(q_cols, c_rows)
        parts = []
        for d in range(2):
            g = g_ref[d, rows, :]
            log_inter = (_per_chunk(g) + mst_ref[d, chunks]).reshape(BLK, LANES)
            log_d = jnp.where(masks[d], g + ut_ref[d, rows, :], -jnp.inf)
            m_t = jnp.maximum(log_inter, jnp.max(log_d, axis=-1, keepdims=True))
            s_inter = jnp.exp(log_inter - m_t)
            s_qk = qk_raw * jnp.exp(log_d - m_t)
            own_half = lane_lo if d == 0 else jnp.logical_not(lane_lo)
            q_n = jnp.where(own_half, (_per_chunk(qq) * nst_ref[d, chunks]).reshape(BLK, LANES), 0.0)
            den = jnp.sum(s_qk + s_inter * q_n, axis=-1, keepdims=True)
            parts.append((s_qk, s_inter, jnp.maximum(jnp.abs(den), jnp.exp(-m_t[:, 0:1]))))
        nv = _dot(jnp.concatenate([parts[0][0], parts[1][0]], axis=0).astype(BF16), vb)
        o = None
        for d in range(2):
            _, s_inter, den = parts[d]
            h = (nv[d * BLK:(d + 1) * BLK] + s_inter * qc[:, d * ML_DV:(d + 1) * ML_DV]) / den
            o = h if o is None else o + h
        hsum_ref[rows, :] = o
        return carry
    lax.fori_loop(0, n_blocks, ml_out, 0, unroll=min(2, n_blocks))

    def ml_norm(i, carry):
        rows = pl.ds(pl.multiple_of(i * BLK, BLK), BLK)
        o = hsum_ref[rows, :]
        mu = jnp.mean(o, axis=-1, keepdims=True)
        oc = o - mu
        var = jnp.mean(oc * oc, axis=-1, keepdims=True)
        o = oc * lax.rsqrt(var + EPS) * mln_ref[...] * _sigmoid(mo_ref[rows, :])
        ml_out_ref[rows, :] = o.astype(BF16)
        return carry
    lax.fori_loop(0, n_blocks, ml_norm, 0, unroll=min(4, n_blocks))

    if not has_init:
        for d in range(2):
            sfin_ref[0, d, 0] = jnp.transpose(st_fin[d])
            cfin_ref[0, d, 0] = c_fin[d]
            nfin_ref[0, d, 0] = n_fin[d][:, d * ML_DQK:(d + 1) * ML_DQK]
            mfin_ref[0, d, 0] = m_fin[d]


def _scan(proj, gates, lb_logits, hg_norm_g, gate_bias, ml_norm_g, init, *, bsz, t_len):
    has_init = init is not None
    m = bsz * t_len
    n_chunks = t_len // CHUNK
    n_blocks = t_len // BLK

    def col(sec):
        return pl.BlockSpec((t_len, HEAD_W), lambda b, h, sec=sec: (b, sec + h))

    in_specs = [col(SEC_HQ), col(SEC_HF_FW), col(SEC_HF_BW), col(SEC_HI), col(SEC_HGATE),
                col(SEC_MQK), col(SEC_MV), col(SEC_MO),
                pl.BlockSpec((t_len, LANES), lambda b, h: (b, 0)),
                pl.BlockSpec((2, 2, HEAD_W), lambda b, h: (0, 0, h)),
                pl.BlockSpec((1, HG_DV), lambda b, h: (0, 0)),
                pl.BlockSpec((1, LANES), lambda b, h: (0, 0)),
                pl.BlockSpec((1, ML_DV), lambda b, h: (0, h))]
    args = [proj] * 8 + [gates, lb_logits, hg_norm_g, gate_bias, ml_norm_g]
    out_specs = [pl.BlockSpec((t_len, HEAD_W), lambda b, h: (b, h)),
                 pl.BlockSpec((t_len, HEAD_W), lambda b, h: (b, h))]
    out_shape = [jax.ShapeDtypeStruct((m, N_HEADS * HG_DV), BF16),
                 jax.ShapeDtypeStruct((m, N_HEADS * ML_DV), BF16)]
    if has_init:
        s0, c0, n0, m0 = init
        in_specs += [pl.BlockSpec((1, 1, 2, 1, HG_DK, HG_DV), lambda b, h: (b, 0, 0, h, 0, 0)),
                     pl.BlockSpec((1, 1, 2, 1, ML_DQK, ML_DV), lambda b, h: (b, 0, 0, h, 0, 0)),
                     pl.BlockSpec((1, 1, 2, 1, 1, ML_DQK), lambda b, h: (b, 0, 0, h, 0, 0)),
                     pl.BlockSpec(memory_space=pltpu.SMEM)]
        args += [s0, c0, n0.reshape(n0.shape[:4] + (1, ML_DQK)), m0.reshape(bsz, 2 * N_HEADS)]
    else:
        out_specs += [pl.BlockSpec((1, 2, 1, HG_DK, HG_DV), lambda b, h: (b, 0, h, 0, 0)),
                      pl.BlockSpec((1, 2, 1, ML_DQK, ML_DV), lambda b, h: (b, 0, h, 0, 0)),
                      pl.BlockSpec((1, 2, 1, 1, ML_DQK), lambda b, h: (b, 0, h, 0, 0)),
                      pl.BlockSpec((1, 2, 1, 1, LANES), lambda b, h: (b, 0, h, 0, 0))]
        out_shape += [jax.ShapeDtypeStruct((bsz, 2, N_HEADS, HG_DK, HG_DV), F32),
                      jax.ShapeDtypeStruct((bsz, 2, N_HEADS, ML_DQK, ML_DV), F32),
                      jax.ShapeDtypeStruct((bsz, 2, N_HEADS, 1, ML_DQK), F32),
                      jax.ShapeDtypeStruct((bsz, 2, N_HEADS, 1, LANES), F32)]

    def per_chunk_row(*lead):
        return pltpu.VMEM(lead + (n_chunks, 1, LANES), F32)

    scratch = [pltpu.VMEM((2, t_len, HG_DK), BF16),
               pltpu.VMEM((2, t_len, HG_DK), BF16),
               pltpu.VMEM((2, t_len, 2 * HG_DK), BF16),
               pltpu.VMEM((n_blocks, HG_DV, BLK), BF16),
               pltpu.VMEM((2, n_chunks, HG_DV, HG_DK), F32),
               per_chunk_row(2),
               pltpu.VMEM((2, n_chunks, HG_DV, HG_DK), BF16),
               pltpu.VMEM((t_len, 4 * LANES), F32),
               pltpu.VMEM((2, t_len, LANES), F32),
               pltpu.VMEM((2, t_len, BLK), F32),
               pltpu.VMEM((2, n_chunks, ML_DQK, ML_DV), F32),
               per_chunk_row(),
               per_chunk_row(2),
               per_chunk_row(2),
               per_chunk_row(2),
               per_chunk_row(2),
               per_chunk_row(2),
               pltpu.VMEM((2, n_chunks, ML_DQK, ML_DV), BF16),
               per_chunk_row(2),
               pltpu.VMEM((t_len, ML_DV), F32),
               pltpu.VMEM((t_len, LANES), BF16)]
    kern = functools.partial(_scan_kernel, t_len=t_len, has_init=has_init)
    return pl.pallas_call(
        kern,
        grid=(bsz, N_HEADS),
        in_specs=in_specs,
        out_specs=out_specs,
        out_shape=out_shape,
        scratch_shapes=scratch,
        compiler_params=pltpu.CompilerParams(dimension_semantics=("arbitrary", "arbitrary"),
                                             vmem_limit_bytes=VMEM_LIMIT),
        name="scan_init" if has_init else "scan_ctx",
    )(*args)


def _mix_out_kernel(hg_ref, ml_ref, ga_ref, gb_ref, x_ref, gt_ref, sh_ref, sc_ref, gpost_ref, gpre_ref,
                    wuh_ref, wum_ref, wo_ref, x1_ref, h2_ref):
    y_hg = _dot(hg_ref[...], wuh_ref[...])
    y_ml = _dot(ml_ref[...], wum_ref[...])
    merged = _sigmoid(ga_ref[...]) * y_hg + _sigmoid(gb_ref[...]) * y_ml
    y = _dot(merged.astype(BF16), wo_ref[...])
    x1 = x_ref[...] + gt_ref[0] * _rms_rows(y, gpost_ref[...])
    x1_ref[...] = x1
    h2 = _rms_rows(x1, gpre_ref[...]) * (1.0 + sc_ref[0]) + sh_ref[0]
    h2_ref[...] = h2.astype(BF16)


def _mix_out(hg_act, ml_act, proj, x2d, mod3, mod_row_fn, g_post_mix, g_pre_ffn, w_up_hg, w_up_ml, w_out, *, tm):
    m = x2d.shape[0]

    def resident(shape):
        return pl.BlockSpec(shape, lambda i: (0, 0), pipeline_mode=pl.Buffered(1))

    return pl.pallas_call(
        _mix_out_kernel,
        grid=(m // tm,),
        in_specs=[pl.BlockSpec((tm, N_HEADS * HG_DV), lambda i: (i, 0)),
                  pl.BlockSpec((tm, N_HEADS * ML_DV), lambda i: (i, 0)),
                  pl.BlockSpec((tm, D_MODEL), lambda i: (i, SEC_GA * LANES // D_MODEL)),
                  pl.BlockSpec((tm, D_MODEL), lambda i: (i, SEC_GB * LANES // D_MODEL)),
                  pl.BlockSpec((tm, D_MODEL), lambda i: (i, 0)),
                  pl.BlockSpec((1, 1, D_MODEL), lambda i: (mod_row_fn(i), 0, 2)),
                  pl.BlockSpec((1, 1, D_MODEL), lambda i: (mod_row_fn(i), 0, 3)),
                  pl.BlockSpec((1, 1, D_MODEL), lambda i: (mod_row_fn(i), 0, 4)),
                  resident((1, D_MODEL)),
                  resident((1, D_MODEL)),
                  resident((N_HEADS * HG_DV, D_MODEL)),
                  resident((N_HEADS * ML_DV, D_MODEL)),
                  resident((D_MODEL, D_MODEL))],
        out_specs=[pl.BlockSpec((tm, D_MODEL), lambda i: (i, 0)),
                   pl.BlockSpec((tm, D_MODEL), lambda i: (i, 0))],
        out_shape=[jax.ShapeDtypeStruct((m, D_MODEL), F32),
                   jax.ShapeDtypeStruct((m, D_MODEL), BF16)],
        compiler_params=pltpu.CompilerParams(dimension_semantics=("arbitrary",),
                                             vmem_limit_bytes=VMEM_LIMIT),
        name="mix_out",
    )(hg_act, ml_act, proj, proj, x2d, mod3, mod3, mod3, g_post_mix, g_pre_ffn, w_up_hg, w_up_ml, w_out)


def _ffn_kernel(h2_ref, x1_ref, gt_ref, gpost_ref, wa_ref, wb_ref, wo_ref, out_ref, acc_ref):
    j = pl.program_id(1)

    @pl.when(j == 0)
    def _():
        acc_ref[...] = jnp.zeros_like(acc_ref)

    h2 = h2_ref[...]
    hid = (_silu(_dot(h2, wa_ref[...])) * _dot(h2, wb_ref[...])).astype(BF16)
    for n0 in range(0, D_MODEL, FFN_ACC_COLS):
        acc_ref[:, n0:n0 + FFN_ACC_COLS] += _dot(hid, wo_ref[:, n0:n0 + FFN_ACC_COLS])

    @pl.when(j == pl.num_programs(1) - 1)
    def _():
        out_ref[...] = x1_ref[...] + gt_ref[0] * _rms_rows(acc_ref[...], gpost_ref[...])


def _ffn(h2, x1, mod3, mod_row_fn, g_post_ffn, w_ffn_in, w_ffn_out, *, tm, tf):
    m = x1.shape[0]
    nf = D_FF // tf
    return pl.pallas_call(
        _ffn_kernel,
        grid=(m // tm, nf),
        in_specs=[pl.BlockSpec((tm, D_MODEL), lambda i, j: (i, 0)),
                  pl.BlockSpec((tm, D_MODEL), lambda i, j: (i, 0)),
                  pl.BlockSpec((1, 1, D_MODEL), lambda i, j: (mod_row_fn(i), 0, 5)),
                  pl.BlockSpec((1, D_MODEL), lambda i, j: (0, 0)),
                  pl.BlockSpec((D_MODEL, tf), lambda i, j: (0, j)),
                  pl.BlockSpec((D_MODEL, tf), lambda i, j, nf=nf: (0, j + nf)),
                  pl.BlockSpec((tf, D_MODEL), lambda i, j: (j, 0))],
        out_specs=pl.BlockSpec((tm, D_MODEL), lambda i, j: (i, 0)),
        out_shape=jax.ShapeDtypeStruct((m, D_MODEL), F32),
        scratch_shapes=[pltpu.VMEM((tm, D_MODEL), F32)],
        compiler_params=pltpu.CompilerParams(dimension_semantics=("arbitrary", "arbitrary"),
                                             vmem_limit_bytes=VMEM_LIMIT),
        name="ffn",
    )(h2, x1, mod3, g_post_ffn, w_ffn_in, w_ffn_in, w_ffn_out)


def _w_prep_kernel(w_ref, main_ref, gate_ref):
    x = w_ref[0]
    mq = x[:, W_MQ:W_MQ + N_HEADS * ML_DQK]
    mk = x[:, W_MK:W_MK + N_HEADS * ML_DQK]
    pieces = [x[:, :W_MQ]]
    for h in range(N_HEADS):
        pieces += [mq[:, h * ML_DQK:(h + 1) * ML_DQK], mk[:, h * ML_DQK:(h + 1) * ML_DQK]]
    pieces += [x[:, W_MV:W_GATES], x[:, W_GATES + GATE_W:]]
    main_ref[...] = jnp.concatenate(pieces, axis=1).astype(BF16)
    gate_ref[...] = jnp.concatenate([x[:, W_GATES:W_GATES + GATE_W]] * (LANES // GATE_W), axis=1).astype(BF16)


def _prep_w_in(w):
    rows = 128
    return pl.pallas_call(
        _w_prep_kernel,
        grid=(D_MODEL // rows,),
        in_specs=[pl.BlockSpec((1, rows, IN_W), lambda i: (0, i, 0))],
        out_specs=[pl.BlockSpec((rows, PROJ_W), lambda i: (i, 0)),
                   pl.BlockSpec((rows, LANES), lambda i: (i, 0))],
        out_shape=[jax.ShapeDtypeStruct((D_MODEL, PROJ_W), BF16),
                   jax.ShapeDtypeStruct((D_MODEL, LANES), BF16)],
        compiler_params=pltpu.CompilerParams(dimension_semantics=("arbitrary",),
                                             vmem_limit_bytes=VMEM_LIMIT),
        name="w_prep",
    )(w)


def _layer_group(x, mod3, mod_row_fn, init, p, *, tm_in, tn_in, tm_mix, tm_ffn, tf):
    bsz, t_len, _ = x.shape
    x2d = x.reshape(bsz * t_len, D_MODEL)
    proj, gates = _in_proj(x2d, mod3, lambda i: mod_row_fn(i, tm_in), p["g_pre_mix"], p["w_main"], p["w_gate"],
                           tm=tm_in, tn=tn_in)
    outs = _scan(proj, gates, p["lb_logits"], p["hg_norm_g"], p["gate_bias"], p["ml_norm_g"], init,
                 bsz=bsz, t_len=t_len)
    hg_act, ml_act = outs[0], outs[1]
    x1, h2 = _mix_out(hg_act, ml_act, proj, x2d, mod3, lambda i: mod_row_fn(i, tm_mix), p["g_post_mix"],
                      p["g_pre_ffn"], p["w_up_hg"], p["w_up_ml"], p["w_out"], tm=tm_mix)
    y = _ffn(h2, x1, mod3, lambda i: mod_row_fn(i, tm_ffn), p["g_post_ffn"], p["w_ffn_in"], p["w_ffn_out"],
             tm=tm_ffn, tf=tf)
    return y.reshape(bsz, t_len, D_MODEL), outs[2:]


def kernel(x_prompt, x_sample, c, state_hgrn_s, state_mlstm_c, state_mlstm_n, state_mlstm_m, c_ctx, w_mod, b_mod,
           norm_pre_mix, norm_post_mix, norm_pre_ffn, norm_post_ffn, w_in, hgrn_lb_logits, hgrn_norm_g, mlstm_b_i,
           mlstm_b_f, mlstm_norm_g, w_up_hgrn, w_up_mlstm, w_out, w_ffn_in, w_ffn_out):
    bsz_p, t_p, _ = x_prompt.shape
    bsz_s, t_s, _ = x_sample.shape

    cond = jnp.concatenate([c_ctx[None, :], c, jnp.zeros((MOD_ROWS - 1 - bsz_s, D_MODEL), F32)], axis=0)
    mod = _modulation(cond, w_mod[0], b_mod[0][None, :])
    mod3 = mod.reshape(MOD_ROWS, 1, N_MOD)

    w_main, w_gate = _prep_w_in(w_in)
    p = dict(
        w_main=w_main, w_gate=w_gate,
        g_pre_mix=norm_pre_mix[0][None, :], g_post_mix=norm_post_mix[0][None, :],
        g_pre_ffn=norm_pre_ffn[0][None, :], g_post_ffn=norm_post_ffn[0][None, :],
        lb_logits=hgrn_lb_logits, hg_norm_g=hgrn_norm_g[0][None, :],
        gate_bias=jnp.tile(jnp.concatenate([mlstm_b_i[0], mlstm_b_f[0]]), LANES // GATE_W)[None, :],
        ml_norm_g=mlstm_norm_g[0][None, :],
        w_up_hg=w_up_hgrn[0].astype(BF16), w_up_ml=w_up_mlstm[0].astype(BF16), w_out=w_out[0].astype(BF16),
        w_ffn_in=w_ffn_in[0].astype(BF16), w_ffn_out=w_ffn_out[0].astype(BF16),
    )
    tiles_p = dict(tm_in=min(1024, bsz_p * t_p), tn_in=1024, tm_mix=256, tm_ffn=min(512, bsz_p * t_p), tf=512)
    tiles_s = dict(tm_in=min(1024, t_s), tn_in=1024, tm_mix=min(256, t_s), tm_ffn=min(512, t_s), tf=512)
    y_p, ctx_states = _layer_group(x_prompt, mod3, lambda i, tm: 0, None, p, **tiles_p)
    init = (state_hgrn_s, state_mlstm_c, state_mlstm_n, state_mlstm_m[:, 0])
    y_s, _ = _layer_group(x_sample, mod3, lambda i, tm: 1 + (i * tm) // t_s, init, p, **tiles_s)

    s_fin, c_fin, n_fin, m_fin = ctx_states
    new_hgrn_s = s_fin[:, None]
    new_mlstm_c = c_fin[:, None]
    new_mlstm_n = n_fin[:, None, :, :, 0, :]
    new_mlstm_m = m_fin[:, None, :, :, 0, 0]
    return (y_p, y_s, new_hgrn_s, new_mlstm_c, new_mlstm_n, new_mlstm_m)
```

```python
import functools

import jax
import jax.numpy as jnp
from jax import lax
from jax.experimental import pallas as pl
from jax.experimental.pallas import tpu as pltpu

F32 = jnp.float32
BF16 = jnp.bfloat16

D_MODEL = 2048
N_HEADS = 8
HG_DK = 128
HG_DV = 128
ML_DQK = 64
ML_DV = 128
HEAD_W = 128
D_FF = 5632
CHUNK = 64
EPS = 1e-6
N_MOD = 6 * D_MODEL
LANES = 128
MOD_ROWS = 16
BLK = 2 * CHUNK
SEL_ROWS = 512
MGATE_COLS = 256
FFN_ACC_COLS = 512
GATE_W = 4 * N_HEADS

SEC_HQ, SEC_HF_FW, SEC_HF_BW, SEC_HI, SEC_HGATE = 0, 8, 16, 24, 32
SEC_MQK, SEC_MV, SEC_MO = 40, 48, 56
SEC_GA, SEC_GB = 64, 80
PROJ_W = 96 * LANES
PLAIN_W = SEC_GA * LANES

W_MQ = 5 * N_HEADS * HG_DK
W_MK = W_MQ + N_HEADS * ML_DQK
W_MV = W_MK + N_HEADS * ML_DQK
W_GATES = W_MV + N_HEADS * ML_DV
IN_W = W_GATES + 4 * N_HEADS + N_HEADS * ML_DV + 2 * D_MODEL

VMEM_LIMIT = 56 * 1024 * 1024


def _dot(a, b):
    return jnp.dot(a, b, preferred_element_type=F32)


def _dot_nt(a, b):
    return lax.dot_general(a, b, (((1,), (1,)), ((), ())), preferred_element_type=F32)


def _dot_tn(a, b):
    return lax.dot_general(a, b, (((0,), (0,)), ((), ())), preferred_element_type=F32)


def _sigmoid_pair(z):
    t = 0.5 * jnp.tanh(0.5 * z)
    return 0.5 + t, 0.5 - t


def _sigmoid(z):
    return 0.5 * jnp.tanh(0.5 * z) + 0.5


def _silu(z):
    return z * _sigmoid(z)


def _log_sigmoid(z):
    return jnp.minimum(z, 0.0) - jnp.log1p(jnp.exp(-jnp.abs(z)))


def _mod_kernel(c_ref, w_ref, b_ref, o_ref):
    a = _silu(c_ref[...]).astype(BF16)
    o_ref[...] = _dot(a, w_ref[...].astype(BF16)) + b_ref[...]


def _modulation(cond, w_mod, b_mod):
    tn = 1024
    return pl.pallas_call(
        _mod_kernel,
        grid=(N_MOD // tn,),
        in_specs=[pl.BlockSpec((MOD_ROWS, D_MODEL), lambda j: (0, 0)),
                  pl.BlockSpec((D_MODEL, tn), lambda j: (0, j)),
                  pl.BlockSpec((1, tn), lambda j: (0, j))],
        out_specs=pl.BlockSpec((MOD_ROWS, tn), lambda j: (0, j)),
        out_shape=jax.ShapeDtypeStruct((MOD_ROWS, N_MOD), F32),
        compiler_params=pltpu.CompilerParams(dimension_semantics=("arbitrary",),
                                             vmem_limit_bytes=VMEM_LIMIT),
        name="modulation",
    )(cond, w_mod, b_mod)


def _rms_rows(x, g):
    return x * lax.rsqrt(jnp.mean(x * x, axis=-1, keepdims=True) + EPS) * g


def _inproj_kernel(x_ref, sh_ref, sc_ref, g_ref, w_ref, wg_ref, proj_ref, mgate_ref, gates_ref, xn_ref, *,
                   tm, n_plain):
    rows = 128
    j = pl.program_id(1)

    @pl.when(j == 0)
    def _():
        def body(i, carry):
            r0 = pl.multiple_of(i * rows, rows)
            x = x_ref[pl.ds(r0, rows), :]
            h = _rms_rows(x, g_ref[...]) * (1.0 + sc_ref[0]) + sh_ref[0]
            hb = h.astype(BF16)
            xn_ref[pl.ds(r0, rows), :] = hb
            gates_ref[pl.ds(r0, rows), :] = _dot_nt(hb, wg_ref[...])
            return carry
        lax.fori_loop(0, tm // rows, body, 0)

    @pl.when(j < n_plain)
    def _():
        proj_ref[...] = _dot_nt(xn_ref[...], w_ref[...])

    @pl.when(j >= n_plain)
    def _():
        for c0 in range(0, w_ref.shape[0], MGATE_COLS):
            z = _dot_nt(xn_ref[...], w_ref[c0:c0 + MGATE_COLS, :])
            mgate_ref[:, c0:c0 + MGATE_COLS] = _sigmoid(z).astype(BF16)


def _in_proj(x2d, mod3, mod_row_fn, g_pre, w_main_t, w_gate_t, *, tm, tn):
    m = x2d.shape[0]
    n_plain = PLAIN_W // tn
    kern = functools.partial(_inproj_kernel, tm=tm, n_plain=n_plain)
    return pl.pallas_call(
        kern,
        grid=(m // tm, PROJ_W // tn),
        in_specs=[pl.BlockSpec((tm, D_MODEL), lambda i, j: (i, 0)),
                  pl.BlockSpec((1, 1, D_MODEL), lambda i, j: (mod_row_fn(i), 0, 0)),
                  pl.BlockSpec((1, 1, D_MODEL), lambda i, j: (mod_row_fn(i), 0, 1)),
                  pl.BlockSpec((1, D_MODEL), lambda i, j: (0, 0)),
                  pl.BlockSpec((tn, D_MODEL), lambda i, j: (j, 0)),
                  pl.BlockSpec((LANES, D_MODEL), lambda i, j: (0, 0))],
        out_specs=[pl.BlockSpec((tm, tn), lambda i, j: (i, jnp.minimum(j, n_plain - 1))),
                   pl.BlockSpec((tm, tn), lambda i, j: (i, jnp.maximum(j - n_plain, 0))),
                   pl.BlockSpec((tm, LANES), lambda i, j: (i, 0))],
        out_shape=[jax.ShapeDtypeStruct((m, PLAIN_W), F32),
                   jax.ShapeDtypeStruct((m, PROJ_W - PLAIN_W), BF16),
                   jax.ShapeDtypeStruct((m, LANES), F32)],
        scratch_shapes=[pltpu.VMEM((tm, D_MODEL), BF16)],
        compiler_params=pltpu.CompilerParams(dimension_semantics=("arbitrary", "arbitrary"),
                                             vmem_limit_bytes=VMEM_LIMIT),
        name="in_proj",
    )(x2d, mod3, mod3, g_pre, w_main_t, w_gate_t)


def _chunk_cumsum(x, pos, rev):
    n = x.shape[0]
    s = 1
    while s < CHUNK:
        if rev:
            x = x + jnp.where(pos < CHUNK - s, pltpu.roll(x, n - s, axis=0), 0.0)
        else:
            x = x + jnp.where(pos >= s, pltpu.roll(x, s, axis=0), 0.0)
        s *= 2
    return x


def _per_chunk(x):
    return x.reshape(BLK // CHUNK, CHUNK, x.shape[-1])


def _scan_kernel(*refs, t_len, has_init):
    (hq_ref, hff_ref, hfb_ref, hi_ref, hgate_ref, mqk_ref, mv_ref, mo_ref, gates_ref,
     lbl_ref, hgn_ref, gbias_ref, mln_ref) = refs[:13]
    pos = 13
    if has_init:
        s0_ref, c0_ref, n0_ref, m0_ref = refs[pos:pos + 4]
        pos += 4
    hg_out_ref, ml_out_ref = refs[pos:pos + 2]
    pos += 2
    if not has_init:
        sfin_ref, cfin_ref, nfin_ref, mfin_ref = refs[pos:pos + 4]
        pos += 4
    (qa_ref, ka_ref, qx_ref, vt_ref, ds_ref, dec_ref, stt_ref,
     sel_ref, g_ref, ut_ref, cloc_ref, nloc_ref, a_ref, mloc_ref, mst_ref, sold_ref, sloc_ref,
     cst_ref, nst_ref, hsum_ref, pk_ref) = refs[pos:]

    b_idx = pl.program_id(0)
    h_idx = pl.program_id(1)
    n_chunks = t_len // CHUNK
    n_blocks = t_len // BLK
    cpb = BLK // CHUNK
    mid = CHUNK // 2

    row = lax.broadcasted_iota(jnp.int32, (BLK, BLK), 0)
    col = lax.broadcasted_iota(jnp.int32, (BLK, BLK), 1)
    same_chunk = jnp.right_shift(row, 6) == jnp.right_shift(col, 6)
    masks = (same_chunk & (col <= row), same_chunk & (col >= row))
    in_pos = jnp.bitwise_and(row, CHUNK - 1)
    row_lo = row < CHUNK
    lane_lo = col < ML_DQK
    lane = lax.broadcasted_iota(jnp.int32, (1, LANES), 1)

    def first_chunk_only(x):
        return jnp.concatenate([x[:CHUNK], jnp.zeros_like(x[CHUNK:])], axis=0).astype(BF16)

    def second_chunk_only(x):
        return jnp.concatenate([jnp.zeros_like(x[:CHUNK]), x[CHUNK:]], axis=0).astype(BF16)

    logits = lbl_ref[...]
    l0, l1 = logits[:, 0, :], logits[:, 1, :]
    mx = jnp.maximum(l0, l1)
    e0, e1 = jnp.exp(l0 - mx), jnp.exp(l1 - mx)
    lb_all = e0 / (e0 + e1)

    is_f = jnp.bitwise_and(lane, GATE_W - 1) >= 2 * N_HEADS
    piece = jnp.right_shift(lane, 5)
    sel_rows = min(SEL_ROWS, t_len)

    @pl.when(h_idx == 0)
    def _():
        def ml_pack(i, carry):
            rows = pl.ds(pl.multiple_of(i * sel_rows, sel_rows), sel_rows)
            xg = gates_ref[rows, :] + gbias_ref[...]
            xg = jnp.where(is_f, _log_sigmoid(xg), xg)
            hi = xg.astype(BF16).astype(F32)
            rest = xg - hi
            mid = rest.astype(BF16).astype(F32)
            packed = jnp.where(piece == 0, hi, jnp.where(piece == 1, mid, rest - mid))
            pk_ref[rows, :] = packed.astype(BF16)
            return carry
        lax.fori_loop(0, t_len // sel_rows, ml_pack, 0)


    def hg_prep(i, carry):
        rows = pl.ds(pl.multiple_of(i * BLK, BLK), BLK)
        q = _silu(hq_ref[rows, :])
        vt = jnp.transpose(hi_ref[rows, :]).astype(BF16)
        vt_ref[i] = vt
        kd_cols = []
        for d, zf_ref in enumerate((hff_ref, hfb_ref)):
            rev = d == 1
            lb = lb_all[d:d + 1]
            oml = 1.0 - lb
            sp, sn = _sigmoid_pair(zf_ref[rows, :])
            k = oml * sn
            b = _chunk_cumsum(jnp.log2(lb + oml * sp), in_pos, rev)
            b3, q3, k3 = _per_chunk(b), _per_chunk(q), _per_chunk(k)
            i_mid, i_last = (CHUNK - 1 - mid, 0) if rev else (mid, CHUNK - 1)
            b_mid, b_last = b3[:, i_mid:i_mid + 1], b3[:, i_last:i_last + 1]
            qa_ref[d, rows, :] = (q3 * jnp.exp2(b3 - b_mid)).reshape(BLK, HG_DK).astype(BF16)
            ka_ref[d, rows, :] = (k3 * jnp.exp2(b_mid - b3)).reshape(BLK, HG_DK).astype(BF16)
            qi = q * jnp.exp2(b)
            qx_ref[d, rows, :HG_DK] = first_chunk_only(qi)
            qx_ref[d, rows, HG_DK:] = second_chunk_only(qi)
            kd = (k3 * jnp.exp2(b_last - b3)).reshape(BLK, HG_DK)
            kd_cols += [first_chunk_only(kd), second_chunk_only(kd)]
            dec = jnp.exp2(b_last)
            for c in range(cpb):
                dec_ref[d, i * cpb + c] = dec[c]
        ds_t = _dot(vt, jnp.concatenate(kd_cols, axis=1))
        for d in range(2):
            for c in range(cpb):
                j = d * cpb + c
                ds_ref[d, i * cpb + c] = ds_t[:, j * HG_DK:(j + 1) * HG_DK]
        return carry
    lax.fori_loop(0, n_blocks, hg_prep, 0, unroll=min(4, n_blocks))

    if has_init:
        st0 = tuple(jnp.transpose(s0_ref[0, 0, d, 0]) for d in range(2))
    else:
        st0 = (jnp.zeros((HG_DV, HG_DK), F32),) * 2

    def hg_state(n, carry):
        new = []
        for d in range(2):
            cn = (n_chunks - 1 - n) if d == 1 else n
            stt_ref[d, cn] = carry[d].astype(BF16)
            new.append(dec_ref[d, cn] * carry[d] + ds_ref[d, cn])
        return tuple(new)
    st_fin = lax.fori_loop(0, n_chunks, hg_state, st0)

    def hg_out(i, carry):
        rows = pl.ds(pl.multiple_of(i * BLK, BLK), BLK)
        vt = vt_ref[i]
        lhs, rhs_t = [], []
        for d in range(2):
            att = jnp.where(masks[d], _dot_nt(qa_ref[d, rows, :], ka_ref[d, rows, :]), 0.0)
            lhs += [att.astype(BF16), qx_ref[d, rows, :]]
            rhs_t += [vt] + [stt_ref[d, i * cpb + c] for c in range(cpb)]
        o = _dot_nt(jnp.concatenate(lhs, axis=1), jnp.concatenate(rhs_t, axis=1))
        o = (o * lax.rsqrt(jnp.mean(o * o, axis=-1, keepdims=True) + EPS) * hgn_ref[...]
             * _silu(hgate_ref[rows, :]))
        hg_out_ref[rows, :] = o.astype(BF16)
        return carry
    lax.fori_loop(0, n_blocks, hg_out, 0, unroll=min(4, n_blocks))


    r_i = lax.broadcasted_iota(jnp.int32, (LANES, 4 * LANES), 0)
    c_i = lax.broadcasted_iota(jnp.int32, (LANES, 4 * LANES), 1)
    cblk = jnp.right_shift(c_i, 7)
    target = h_idx + jnp.where(cblk == 0, 0, jnp.where(cblk == 1, 2 * N_HEADS,
                                                       jnp.where(cblk == 2, N_HEADS, 3 * N_HEADS)))
    onehot = jnp.where((jnp.bitwise_and(r_i, GATE_W - 1) == target) & (r_i < 3 * GATE_W), 1.0, 0.0).astype(BF16)

    def ml_select(i, carry):
        rows = pl.ds(pl.multiple_of(i * sel_rows, sel_rows), sel_rows)
        sel_ref[rows, :] = _dot(pk_ref[rows, :], onehot)
        return carry
    lax.fori_loop(0, t_len // sel_rows, ml_select, 0)

    def ml_prep(i, carry):
        rows = pl.ds(pl.multiple_of(i * BLK, BLK), BLK)
        qk = mqk_ref[rows, :]
        kk = jnp.where(lane_lo, pltpu.roll(qk, ML_DQK, axis=1), qk) * (ML_DQK ** -0.5)
        v = mv_ref[rows, :]
        e_dir = []
        for d in range(2):
            rev = d == 1
            li = sel_ref[rows, (2 * d) * LANES:(2 * d + 1) * LANES]
            lf = sel_ref[rows, (2 * d + 1) * LANES:(2 * d + 2) * LANES]
            g = _chunk_cumsum(lf, in_pos, rev)
            u = li - g
            g_ref[d, rows, :] = g
            ut_ref[d, rows, :] = jnp.transpose(u)
            i_last = 0 if rev else CHUNK - 1
            g_last = _per_chunk(g)[:, i_last:i_last + 1]
            w_end = g_last + _per_chunk(u)
            m_loc = jnp.max(w_end, axis=1, keepdims=True)
            e_dir.append(jnp.exp(w_end - m_loc).reshape(BLK, LANES))
            for c in range(cpb):
                a_ref[d, i * cpb + c] = g_last[c]
                mloc_ref[d, i * cpb + c] = m_loc[c]
        wk = jnp.where(lane_lo, e_dir[0], e_dir[1]) * kk
        v_cols = jnp.concatenate([first_chunk_only(v), second_chunk_only(v)], axis=1)
        c_loc = _dot_tn(wk.astype(BF16), v_cols)
        n_loc = jnp.sum(_per_chunk(wk), axis=1, keepdims=True)
        for c in range(cpb):
            nloc_ref[i * cpb + c] = n_loc[c]
            for d in range(2):
                cloc_ref[d, i * cpb + c] = c_loc[d * ML_DQK:(d + 1) * ML_DQK, c * ML_DV:(c + 1) * ML_DV]
        return carry
    lax.fori_loop(0, n_blocks, ml_prep, 0, unroll=min(4, n_blocks))

    if has_init:
        m0 = tuple(jnp.full((1, LANES), m0_ref[b_idx, d * N_HEADS + h_idx], F32) for d in range(2))
    else:
        m0 = (jnp.zeros((1, LANES), F32),) * 2

    def ml_stab(n, carry):
        new = []
        for d in range(2):
            cn = (n_chunks - 1 - n) if d == 1 else n
            m_st = carry[d]
            a, m_loc = a_ref[d, cn], mloc_ref[d, cn]
            m_new = jnp.maximum(a + m_st, m_loc)
            mst_ref[d, cn] = m_st
            sold_ref[d, cn] = jnp.exp(a + m_st - m_new)
            sloc_ref[d, cn] = jnp.exp(m_loc - m_new)
            new.append(m_new)
        return tuple(new)
    m_fin = lax.fori_loop(0, n_chunks, ml_stab, m0)

    if has_init:
        c0 = tuple(c0_ref[0, 0, d, 0] for d in range(2))
        n0 = tuple(jnp.concatenate([n0_ref[0, 0, d, 0]] * 2, axis=1) for d in range(2))
    else:
        c0 = (jnp.zeros((ML_DQK, ML_DV), F32),) * 2
        n0 = (jnp.zeros((1, LANES), F32),) * 2

    def ml_state(n, carry):
        cs, ns = carry
        new_c, new_n = [], []
        for d in range(2):
            cn = (n_chunks - 1 - n) if d == 1 else n
            cst_ref[d, cn] = cs[d].astype(BF16)
            nst_ref[d, cn] = ns[d]
            s_old, s_loc = sold_ref[d, cn], sloc_ref[d, cn]
            new_c.append(s_old * cs[d] + s_loc * cloc_ref[d, cn])
            new_n.append(s_old * ns[d] + s_loc * nloc_ref[cn])
        return tuple(new_c), tuple(new_n)
    c_fin, n_fin = lax.fori_loop(0, n_chunks, ml_state, (c0, n0))

    def ml_out(i, carry):
        rows = pl.ds(pl.multiple_of(i * BLK, BLK), BLK)
        chunks = pl.ds(i * cpb, cpb)
        qk = mqk_ref[rows, :]
        qb = qk[:, :ML_DQK].astype(BF16)
        kb = (qk[:, ML_DQK:] * (ML_DQK ** -0.5)).astype(BF16)
        vb = mv_ref[rows, :].astype(BF16)
        qk_raw = _dot_nt(qb, kb)
        qq = jnp.where(lane_lo, qk, pltpu.roll(qk, ML_DQK, axis=1))
        q_cols = jnp.where(lane_lo == row_lo, qq, 0.0).astype(BF16)
        c_rows = jnp.concatenate([cst_ref[d, chunks].reshape(BLK, ML_DV) for d in range(2)], axis=1)
        qc = _dot(q_cols, c_rows)
        parts = []
        for d in range(2):
            g = g_ref[d, rows, :]
            log_inter = (_per_chunk(g) + mst_ref[d, chunks]).reshape(BLK, LANES)
            log_d = jnp.where(masks[d], g + ut_ref[d, rows, :], -jnp.inf)
            m_t = jnp.maximum(log_inter, jnp.max(log_d, axis=-1, keepdims=True))
            s_inter = jnp.exp(log_inter - m_t)
            s_qk = qk_raw * jnp.exp(log_d - m_t)
            own_half = lane_lo if d == 0 else jnp.logical_not(lane_lo)
            q_n = jnp.where(own_half, (_per_chunk(qq) * nst_ref[d, chunks]).reshape(BLK, LANES), 0.0)
            den = jnp.sum(s_qk + s_inter * q_n, axis=-1, keepdims=True)
            parts.append((s_qk, s_inter, jnp.maximum(jnp.abs(den), jnp.exp(-m_t[:, 0:1]))))
        nv = _dot(jnp.concatenate([parts[0][0], parts[1][0]], axis=0).astype(BF16), vb)
        o = None
        for d in range(2):
            _, s_inter, den = parts[d]
            h = (nv[d * BLK:(d + 1) * BLK] + s_inter * qc[:, d * ML_DV:(d + 1) * ML_DV]) / den
            o = h if o is None else o + h
        hsum_ref[rows, :] = o
        return carry
    lax.fori_loop(0, n_blocks, ml_out, 0, unroll=min(2, n_blocks))

    def ml_norm(i, carry):
        rows = pl.ds(pl.multiple_of(i * BLK, BLK), BLK)
        o = hsum_ref[rows, :]
        mu = jnp.mean(o, axis=-1, keepdims=True)
        oc = o - mu
        var = jnp.mean(oc * oc, axis=-1, keepdims=True)
        o = oc * lax.rsqrt(var + EPS) * mln_ref[...] * _sigmoid(mo_ref[rows, :])
        ml_out_ref[rows, :] = o.astype(BF16)
        return carry
    lax.fori_loop(0, n_blocks, ml_norm, 0, unroll=min(4, n_blocks))

    if not has_init:
        for d in range(2):
            sfin_ref[0, d, 0] = jnp.transpose(st_fin[d])
            cfin_ref[0, d, 0] = c_fin[d]
            nfin_ref[0, d, 0] = n_fin[d][:, d * ML_DQK:(d + 1) * ML_DQK]
            mfin_ref[0, d, 0] = m_fin[d]


def _scan(proj, gates, lb_logits, hg_norm_g, gate_bias, ml_norm_g, init, *, bsz, t_len):
    has_init = init is not None
    m = bsz * t_len
    n_chunks = t_len // CHUNK
    n_blocks = t_len // BLK

    def col(sec):
        return pl.BlockSpec((t_len, HEAD_W), lambda b, h, sec=sec: (b, sec + h))

    in_specs = [col(SEC_HQ), col(SEC_HF_FW), col(SEC_HF_BW), col(SEC_HI), col(SEC_HGATE),
                col(SEC_MQK), col(SEC_MV), col(SEC_MO),
                pl.BlockSpec((t_len, LANES), lambda b, h: (b, 0)),
                pl.BlockSpec((2, 2, HEAD_W), lambda b, h: (0, 0, h)),
                pl.BlockSpec((1, HG_DV), lambda b, h: (0, 0)),
                pl.BlockSpec((1, LANES), lambda b, h: (0, 0)),
                pl.BlockSpec((1, ML_DV), lambda b, h: (0, h))]
    args = [proj] * 8 + [gates, lb_logits, hg_norm_g, gate_bias, ml_norm_g]
    out_specs = [pl.BlockSpec((t_len, HEAD_W), lambda b, h: (b, h)),
                 pl.BlockSpec((t_len, HEAD_W), lambda b, h: (b, h))]
    out_shape = [jax.ShapeDtypeStruct((m, N_HEADS * HG_DV), BF16),
                 jax.ShapeDtypeStruct((m, N_HEADS * ML_DV), BF16)]
    if has_init:
        s0, c0, n0, m0 = init
        in_specs += [pl.BlockSpec((1, 1, 2, 1, HG_DK, HG_DV), lambda b, h: (b, 0, 0, h, 0, 0)),
                     pl.BlockSpec((1, 1, 2, 1, ML_DQK, ML_DV), lambda b, h: (b, 0, 0, h, 0, 0)),
                     pl.BlockSpec((1, 1, 2, 1, 1, ML_DQK), lambda b, h: (b, 0, 0, h, 0, 0)),
                     pl.BlockSpec(memory_space=pltpu.SMEM)]
        args += [s0, c0, n0.reshape(n0.shape[:4] + (1, ML_DQK)), m0.reshape(bsz, 2 * N_HEADS)]
    else:
        out_specs += [pl.BlockSpec((1, 2, 1, HG_DK, HG_DV), lambda b, h: (b, 0, h, 0, 0)),
                      pl.BlockSpec((1, 2, 1, ML_DQK, ML_DV), lambda b, h: (b, 0, h, 0, 0)),
                      pl.BlockSpec((1, 2, 1, 1, ML_DQK), lambda b, h: (b, 0, h, 0, 0)),
                      pl.BlockSpec((1, 2, 1, 1, LANES), lambda b, h: (b, 0, h, 0, 0))]
        out_shape += [jax.ShapeDtypeStruct((bsz, 2, N_HEADS, HG_DK, HG_DV), F32),
                      jax.ShapeDtypeStruct((bsz, 2, N_HEADS, ML_DQK, ML_DV), F32),
                      jax.ShapeDtypeStruct((bsz, 2, N_HEADS, 1, ML_DQK), F32),
                      jax.ShapeDtypeStruct((bsz, 2, N_HEADS, 1, LANES), F32)]

    def per_chunk_row(*lead):
        return pltpu.VMEM(lead + (n_chunks, 1, LANES), F32)

    scratch = [pltpu.VMEM((2, t_len, HG_DK), BF16),
               pltpu.VMEM((2, t_len, HG_DK), BF16),
               pltpu.VMEM((2, t_len, 2 * HG_DK), BF16),
               pltpu.VMEM((n_blocks, HG_DV, BLK), BF16),
               pltpu.VMEM((2, n_chunks, HG_DV, HG_DK), F32),
               per_chunk_row(2),
               pltpu.VMEM((2, n_chunks, HG_DV, HG_DK), BF16),
               pltpu.VMEM((t_len, 4 * LANES), F32),
               pltpu.VMEM((2, t_len, LANES), F32),
               pltpu.VMEM((2, t_len, BLK), F32),
               pltpu.VMEM((2, n_chunks, ML_DQK, ML_DV), F32),
               per_chunk_row(),
               per_chunk_row(2),
               per_chunk_row(2),
               per_chunk_row(2),
               per_chunk_row(2),
               per_chunk_row(2),
               pltpu.VMEM((2, n_chunks, ML_DQK, ML_DV), BF16),
               per_chunk_row(2),
               pltpu.VMEM((t_len, ML_DV), F32),
               pltpu.VMEM((t_len, LANES), BF16)]
    kern = functools.partial(_scan_kernel, t_len=t_len, has_init=has_init)
    return pl.pallas_call(
        kern,
        grid=(bsz, N_HEADS),
        in_specs=in_specs,
        out_specs=out_specs,
        out_shape=out_shape,
        scratch_shapes=scratch,
        compiler_params=pltpu.CompilerParams(dimension_semantics=("arbitrary", "arbitrary"),
                                             vmem_limit_bytes=VMEM_LIMIT),
        name="scan_init" if has_init else "scan_ctx",
    )(*args)


def _mix_out_kernel(hg_ref, ml_ref, ga_ref, gb_ref, x_ref, gt_ref, sh_ref, sc_ref, gpost_ref, gpre_ref,
                    wuh_ref, wum_ref, wo_ref, x1_ref, h2_ref):
    y_hg = _dot(hg_ref[...], wuh_ref[...])
    y_ml = _dot(ml_ref[...], wum_ref[...])
    merged = ga_ref[...].astype(F32) * y_hg + gb_ref[...].astype(F32) * y_ml
    y = _dot(merged.astype(BF16), wo_ref[...])
    x1 = x_ref[...] + gt_ref[0] * _rms_rows(y, gpost_ref[...])
    x1_ref[...] = x1
    h2 = _rms_rows(x1, gpre_ref[...]) * (1.0 + sc_ref[0]) + sh_ref[0]
    h2_ref[...] = h2.astype(BF16)


def _mix_out(hg_act, ml_act, mgate, x2d, mod3, mod_row_fn, g_post_mix, g_pre_ffn, w_up_hg, w_up_ml, w_out, *, tm):
    m = x2d.shape[0]

    def resident(shape):
        return pl.BlockSpec(shape, lambda i: (0, 0), pipeline_mode=pl.Buffered(1))

    return pl.pallas_call(
        _mix_out_kernel,
        grid=(m // tm,),
        in_specs=[pl.BlockSpec((tm, N_HEADS * HG_DV), lambda i: (i, 0)),
                  pl.BlockSpec((tm, N_HEADS * ML_DV), lambda i: (i, 0)),
                  pl.BlockSpec((tm, D_MODEL), lambda i: (i, 0)),
                  pl.BlockSpec((tm, D_MODEL), lambda i: (i, 1)),
                  pl.BlockSpec((tm, D_MODEL), lambda i: (i, 0)),
                  pl.BlockSpec((1, 1, D_MODEL), lambda i: (mod_row_fn(i), 0, 2)),
                  pl.BlockSpec((1, 1, D_MODEL), lambda i: (mod_row_fn(i), 0, 3)),
                  pl.BlockSpec((1, 1, D_MODEL), lambda i: (mod_row_fn(i), 0, 4)),
                  resident((1, D_MODEL)),
                  resident((1, D_MODEL)),
                  resident((N_HEADS * HG_DV, D_MODEL)),
                  resident((N_HEADS * ML_DV, D_MODEL)),
                  resident((D_MODEL, D_MODEL))],
        out_specs=[pl.BlockSpec((tm, D_MODEL), lambda i: (i, 0)),
                   pl.BlockSpec((tm, D_MODEL), lambda i: (i, 0))],
        out_shape=[jax.ShapeDtypeStruct((m, D_MODEL), F32),
                   jax.ShapeDtypeStruct((m, D_MODEL), BF16)],
        compiler_params=pltpu.CompilerParams(dimension_semantics=("arbitrary",),
                                             vmem_limit_bytes=VMEM_LIMIT),
        name="mix_out",
    )(hg_act, ml_act, mgate, mgate, x2d, mod3, mod3, mod3, g_post_mix, g_pre_ffn, w_up_hg, w_up_ml, w_out)


def _ffn_kernel(h2_ref, x1_ref, gt_ref, gpost_ref, wa_ref, wb_ref, wo_ref, out_ref, acc_ref):
    j = pl.program_id(1)

    @pl.when(j == 0)
    def _():
        acc_ref[...] = jnp.zeros_like(acc_ref)

    h2 = h2_ref[...]
    hid = (_silu(_dot(h2, wa_ref[...])) * _dot(h2, wb_ref[...])).astype(BF16)
    for n0 in range(0, D_MODEL, FFN_ACC_COLS):
        acc_ref[:, n0:n0 + FFN_ACC_COLS] += _dot(hid, wo_ref[:, n0:n0 + FFN_ACC_COLS])

    @pl.when(j == pl.num_programs(1) - 1)
    def _():
        out_ref[...] = x1_ref[...] + gt_ref[0] * _rms_rows(acc_ref[...], gpost_ref[...])


def _ffn(h2, x1, mod3, mod_row_fn, g_post_ffn, w_ffn_in, w_ffn_out, *, tm, tf):
    m = x1.shape[0]
    nf = D_FF // tf
    return pl.pallas_call(
        _ffn_kernel,
        grid=(m // tm, nf),
        in_specs=[pl.BlockSpec((tm, D_MODEL), lambda i, j: (i, 0)),
                  pl.BlockSpec((tm, D_MODEL), lambda i, j: (i, 0)),
                  pl.BlockSpec((1, 1, D_MODEL), lambda i, j: (mod_row_fn(i), 0, 5)),
                  pl.BlockSpec((1, D_MODEL), lambda i, j: (0, 0)),
                  pl.BlockSpec((D_MODEL, tf), lambda i, j: (0, j)),
                  pl.BlockSpec((D_MODEL, tf), lambda i, j, nf=nf: (0, j + nf)),
                  pl.BlockSpec((tf, D_MODEL), lambda i, j: (j, 0))],
        out_specs=pl.BlockSpec((tm, D_MODEL), lambda i, j: (i, 0)),
        out_shape=jax.ShapeDtypeStruct((m, D_MODEL), F32),
        scratch_shapes=[pltpu.VMEM((tm, D_MODEL), F32)],
        compiler_params=pltpu.CompilerParams(dimension_semantics=("arbitrary", "arbitrary"),
                                             vmem_limit_bytes=VMEM_LIMIT),
        name="ffn",
    )(h2, x1, mod3, g_post_ffn, w_ffn_in, w_ffn_in, w_ffn_out)


def _w_prep_kernel(wt_ref, main_ref, gate_ref):
    x = wt_ref[...]
    pieces = [x[:W_MQ]]
    for h in range(N_HEADS):
        pieces += [x[W_MQ + h * ML_DQK:W_MQ + (h + 1) * ML_DQK], x[W_MK + h * ML_DQK:W_MK + (h + 1) * ML_DQK]]
    pieces += [x[W_MV:W_GATES], x[W_GATES + GATE_W:]]
    main_ref[...] = jnp.concatenate(pieces, axis=0).astype(BF16)
    gate_ref[...] = jnp.concatenate([x[W_GATES:W_GATES + GATE_W]] * (LANES // GATE_W), axis=0).astype(BF16)


def _prep_w_in(w_t):
    cols = 256
    return pl.pallas_call(
        _w_prep_kernel,
        grid=(D_MODEL // cols,),
        in_specs=[pl.BlockSpec((IN_W, cols), lambda i: (0, i))],
        out_specs=[pl.BlockSpec((PROJ_W, cols), lambda i: (0, i)),
                   pl.BlockSpec((LANES, cols), lambda i: (0, i))],
        out_shape=[jax.ShapeDtypeStruct((PROJ_W, D_MODEL), BF16),
                   jax.ShapeDtypeStruct((LANES, D_MODEL), BF16)],
        compiler_params=pltpu.CompilerParams(dimension_semantics=("arbitrary",),
                                             vmem_limit_bytes=VMEM_LIMIT),
        name="w_prep",
    )(w_t)


def _layer_group(x, mod3, mod_row_fn, init, p, *, tm_in, tn_in, tm_mix, tm_ffn, tf):
    bsz, t_len, _ = x.shape
    x2d = x.reshape(bsz * t_len, D_MODEL)
    proj, mgate, gates = _in_proj(x2d, mod3, lambda i: mod_row_fn(i, tm_in), p["g_pre_mix"], p["w_main"],
                                  p["w_gate"], tm=tm_in, tn=tn_in)
    outs = _scan(proj, gates, p["lb_logits"], p["hg_norm_g"], p["gate_bias"], p["ml_norm_g"], init,
                 bsz=bsz, t_len=t_len)
    hg_act, ml_act = outs[0], outs[1]
    x1, h2 = _mix_out(hg_act, ml_act, mgate, x2d, mod3, lambda i: mod_row_fn(i, tm_mix), p["g_post_mix"],
                      p["g_pre_ffn"], p["w_up_hg"], p["w_up_ml"], p["w_out"], tm=tm_mix)
    y = _ffn(h2, x1, mod3, lambda i: mod_row_fn(i, tm_ffn), p["g_post_ffn"], p["w_ffn_in"], p["w_ffn_out"],
             tm=tm_ffn, tf=tf)
    return y.reshape(bsz, t_len, D_MODEL), outs[2:]


def kernel(x_prompt, x_sample, c, state_hgrn_s, state_mlstm_c, state_mlstm_n, state_mlstm_m, c_ctx, w_mod, b_mod,
           norm_pre_mix, norm_post_mix, norm_pre_ffn, norm_post_ffn, w_in, hgrn_lb_logits, hgrn_norm_g, mlstm_b_i,
           mlstm_b_f, mlstm_norm_g, w_up_hgrn, w_up_mlstm, w_out, w_ffn_in, w_ffn_out):
    bsz_p, t_p, _ = x_prompt.shape
    bsz_s, t_s, _ = x_sample.shape

    cond = jnp.concatenate([c_ctx[None, :], c, jnp.zeros((MOD_ROWS - 1 - bsz_s, D_MODEL), F32)], axis=0)
    mod = _modulation(cond, w_mod[0], b_mod[0][None, :])
    mod3 = mod.reshape(MOD_ROWS, 1, N_MOD)

    w_main, w_gate = _prep_w_in(jnp.transpose(w_in[0]))
    p = dict(
        w_main=w_main, w_gate=w_gate,
        g_pre_mix=norm_pre_mix[0][None, :], g_post_mix=norm_post_mix[0][None, :],
        g_pre_ffn=norm_pre_ffn[0][None, :], g_post_ffn=norm_post_ffn[0][None, :],
        lb_logits=hgrn_lb_logits, hg_norm_g=hgrn_norm_g[0][None, :],
        gate_bias=jnp.tile(jnp.concatenate([mlstm_b_i[0], mlstm_b_f[0]]), LANES // GATE_W)[None, :],
        ml_norm_g=mlstm_norm_g[0][None, :],
        w_up_hg=w_up_hgrn[0].astype(BF16), w_up_ml=w_up_mlstm[0].astype(BF16), w_out=w_out[0].astype(BF16),
        w_ffn_in=w_ffn_in[0].astype(BF16), w_ffn_out=w_ffn_out[0].astype(BF16),
    )
    tiles_p = dict(tm_in=min(1024, bsz_p * t_p), tn_in=1024, tm_mix=min(512, bsz_p * t_p),
                   tm_ffn=min(512, bsz_p * t_p), tf=512)
    tiles_s = dict(tm_in=min(1024, t_s), tn_in=1024, tm_mix=min(512, t_s), tm_ffn=min(512, t_s), tf=512)
    y_p, ctx_states = _layer_group(x_prompt, mod3, lambda i, tm: 0, None, p, **tiles_p)
    init = (state_hgrn_s, state_mlstm_c, state_mlstm_n, state_mlstm_m[:, 0])
    y_s, _ = _layer_group(x_sample, mod3, lambda i, tm: 1 + (i * tm) // t_s, init, p, **tiles_s)

    s_fin, c_fin, n_fin, m_fin = ctx_states
    new_hgrn_s = s_fin[:, None]
    new_mlstm_c = c_fin[:, None]
    new_mlstm_n = n_fin[:, None, :, :, 0, :]
    new_mlstm_m = m_fin[:, None, :, :, 0, 0]
    return (y_p, y_s, new_hgrn_s, new_mlstm_c, new_mlstm_n, new_mlstm_m)
```

```python
import functools

import jax
import jax.numpy as jnp
from jax import lax
from jax.experimental import pallas as pl
from jax.experimental.pallas import tpu as pltpu

F32 = jnp.float32
BF16 = jnp.bfloat16

D_MODEL = 2048
N_HEADS = 8
HG_DK = 128
HG_DV = 128
ML_DQK = 64
ML_DV = 128
HEAD_W = 128
D_FF = 5632
CHUNK = 64
EPS = 1e-6
N_MOD = 6 * D_MODEL
LANES = 128
MOD_ROWS = 16
BLK = 2 * CHUNK
BLOCKS_PER_TRIP = 4
MGATE_COLS = 256
FFN_ACC_COLS = 512
GATE_W = 4 * N_HEADS

SEC_HQ, SEC_HF_FW, SEC_HF_BW, SEC_HI, SEC_HGATE = 0, 8, 16, 24, 32
SEC_MQK, SEC_MV, SEC_MO = 40, 48, 56
SEC_GA, SEC_GB = 64, 80
PROJ_W = 96 * LANES
PLAIN_W = SEC_GA * LANES

W_MQ = 5 * N_HEADS * HG_DK
W_MK = W_MQ + N_HEADS * ML_DQK
W_MV = W_MK + N_HEADS * ML_DQK
W_GATES = W_MV + N_HEADS * ML_DV
IN_W = W_GATES + 4 * N_HEADS + N_HEADS * ML_DV + 2 * D_MODEL

VMEM_LIMIT = 56 * 1024 * 1024
VMEM_LIMIT_FFN = 60 * 1024 * 1024


def _dot(a, b):
    return jnp.dot(a, b, preferred_element_type=F32)


def _dot_nt(a, b):
    return lax.dot_general(a, b, (((1,), (1,)), ((), ())), preferred_element_type=F32)


def _dot_tn(a, b):
    return lax.dot_general(a, b, (((0,), (0,)), ((), ())), preferred_element_type=F32)


def _sigmoid_pair(z):
    t = 0.5 * jnp.tanh(0.5 * z)
    return 0.5 + t, 0.5 - t


def _sigmoid(z):
    return 0.5 * jnp.tanh(0.5 * z) + 0.5


def _silu(z):
    return z * _sigmoid(z)


def _log_sigmoid(z):
    return jnp.minimum(z, 0.0) - jnp.log1p(jnp.exp(-jnp.abs(z)))


def _mod_kernel(c_ref, w_ref, b_ref, o_ref):
    a = _silu(c_ref[...]).astype(BF16)
    o_ref[...] = _dot(a, w_ref[...].astype(BF16)) + b_ref[...]


def _modulation(cond, w_mod, b_mod):
    tn = 1024
    return pl.pallas_call(
        _mod_kernel,
        grid=(N_MOD // tn,),
        in_specs=[pl.BlockSpec((MOD_ROWS, D_MODEL), lambda j: (0, 0)),
                  pl.BlockSpec((D_MODEL, tn), lambda j: (0, j)),
                  pl.BlockSpec((1, tn), lambda j: (0, j))],
        out_specs=pl.BlockSpec((MOD_ROWS, tn), lambda j: (0, j)),
        out_shape=jax.ShapeDtypeStruct((MOD_ROWS, N_MOD), F32),
        compiler_params=pltpu.CompilerParams(dimension_semantics=("arbitrary",),
                                             vmem_limit_bytes=VMEM_LIMIT),
        name="modulation",
    )(cond, w_mod, b_mod)


def _rms_rows(x, g):
    return x * lax.rsqrt(jnp.mean(x * x, axis=-1, keepdims=True) + EPS) * g


def _inproj_kernel(x_ref, sh_ref, sc_ref, g_ref, w_ref, wg_ref, proj_ref, mgate_ref, gates_ref, xn_ref, *,
                   tm, n_plain):
    rows = 128
    j = pl.program_id(1)

    @pl.when(j == 0)
    def _():
        for r0 in range(0, tm, rows):
            x = x_ref[r0:r0 + rows, :]
            h = _rms_rows(x, g_ref[...]) * (1.0 + sc_ref[0]) + sh_ref[0]
            hb = h.astype(BF16)
            xn_ref[r0:r0 + rows, :] = hb
            gates_ref[r0:r0 + rows, :] = _dot_nt(hb, wg_ref[...])

    @pl.when(j < n_plain)
    def _():
        proj_ref[...] = _dot_nt(xn_ref[...], w_ref[...])

    @pl.when(j >= n_plain)
    def _():
        for c0 in range(0, w_ref.shape[0], MGATE_COLS):
            z = _dot_nt(xn_ref[...], w_ref[c0:c0 + MGATE_COLS, :])
            mgate_ref[:, c0:c0 + MGATE_COLS] = _sigmoid(z).astype(BF16)


def _in_proj(x2d, mod3, mod_row_fn, g_pre, w_main_t, w_gate_t, *, tm, tn):
    m = x2d.shape[0]
    n_plain = PLAIN_W // tn
    kern = functools.partial(_inproj_kernel, tm=tm, n_plain=n_plain)
    return pl.pallas_call(
        kern,
        grid=(m // tm, PROJ_W // tn),
        in_specs=[pl.BlockSpec((tm, D_MODEL), lambda i, j: (i, 0)),
                  pl.BlockSpec((1, 1, D_MODEL), lambda i, j: (mod_row_fn(i), 0, 0)),
                  pl.BlockSpec((1, 1, D_MODEL), lambda i, j: (mod_row_fn(i), 0, 1)),
                  pl.BlockSpec((1, D_MODEL), lambda i, j: (0, 0)),
                  pl.BlockSpec((tn, D_MODEL), lambda i, j: (j, 0)),
                  pl.BlockSpec((LANES, D_MODEL), lambda i, j: (0, 0))],
        out_specs=[pl.BlockSpec((tm, tn), lambda i, j: (i, jnp.minimum(j, n_plain - 1))),
                   pl.BlockSpec((tm, tn), lambda i, j: (i, jnp.maximum(j - n_plain, 0))),
                   pl.BlockSpec((tm, LANES), lambda i, j: (i, 0))],
        out_shape=[jax.ShapeDtypeStruct((m, PLAIN_W), F32),
                   jax.ShapeDtypeStruct((m, PROJ_W - PLAIN_W), BF16),
                   jax.ShapeDtypeStruct((m, LANES), F32)],
        scratch_shapes=[pltpu.VMEM((tm, D_MODEL), BF16)],
        compiler_params=pltpu.CompilerParams(dimension_semantics=("arbitrary", "arbitrary"),
                                             vmem_limit_bytes=VMEM_LIMIT),
        name="in_proj",
    )(x2d, mod3, mod3, g_pre, w_main_t, w_gate_t)


def _chunk_cumsum(x, pos, rev):
    n = x.shape[0]
    s = 1
    while s < CHUNK:
        if rev:
            x = x + jnp.where(pos < CHUNK - s, pltpu.roll(x, n - s, axis=0), 0.0)
        else:
            x = x + jnp.where(pos >= s, pltpu.roll(x, s, axis=0), 0.0)
        s *= 2
    return x


def _per_chunk(x):
    return x.reshape(BLK // CHUNK, CHUNK, x.shape[-1])


def _scan_kernel(*refs, t_len, has_init):
    (hq_ref, hff_ref, hfb_ref, hi_ref, hgate_ref, mqk_ref, mv_ref, mo_ref, gates_ref,
     lbl_ref, hgn_ref, gbias_ref, mln_ref) = refs[:13]
    pos = 13
    if has_init:
        s0_ref, c0_ref, n0_ref, m0_ref = refs[pos:pos + 4]
        pos += 4
    hg_out_ref, ml_out_ref = refs[pos:pos + 2]
    pos += 2
    if not has_init:
        sfin_ref, cfin_ref, nfin_ref, mfin_ref = refs[pos:pos + 4]
        pos += 4
    (qa_ref, ka_ref, qx_ref, vt_ref, ds_ref, dec_ref, stt_ref,
     sel_ref, g_ref, ut_ref, cloc_ref, nloc_ref, a_ref, mloc_ref, mst_ref, sold_ref, sloc_ref,
     cst_ref, nst_ref, hsum_ref, pk_ref) = refs[pos:]

    b_idx = pl.program_id(0)
    h_idx = pl.program_id(1)
    n_chunks = t_len // CHUNK
    n_blocks = t_len // BLK
    cpb = BLK // CHUNK
    mid = CHUNK // 2

    row = lax.broadcasted_iota(jnp.int32, (BLK, BLK), 0)
    col = lax.broadcasted_iota(jnp.int32, (BLK, BLK), 1)
    same_chunk = jnp.right_shift(row, 6) == jnp.right_shift(col, 6)
    masks = (same_chunk & (col <= row), same_chunk & (col >= row))
    in_pos = jnp.bitwise_and(row, CHUNK - 1)
    row_lo = row < CHUNK
    lane_lo = col < ML_DQK
    lane = lax.broadcasted_iota(jnp.int32, (1, LANES), 1)

    def first_chunk_only(x):
        return jnp.concatenate([x[:CHUNK], jnp.zeros_like(x[CHUNK:])], axis=0).astype(BF16)

    def second_chunk_only(x):
        return jnp.concatenate([jnp.zeros_like(x[:CHUNK]), x[CHUNK:]], axis=0).astype(BF16)

    logits = lbl_ref[...]
    l0, l1 = logits[:, 0, :], logits[:, 1, :]
    mx = jnp.maximum(l0, l1)
    e0, e1 = jnp.exp(l0 - mx), jnp.exp(l1 - mx)
    lb_all = e0 / (e0 + e1)

    is_f = jnp.bitwise_and(lane, GATE_W - 1) >= 2 * N_HEADS
    piece = jnp.right_shift(lane, 5)
    grp = min(BLOCKS_PER_TRIP, n_blocks)
    sel_rows = grp * BLK

    @pl.when(h_idx == 0)
    def _():
        def ml_pack(i, carry):
            rows = pl.ds(pl.multiple_of(i * sel_rows, sel_rows), sel_rows)
            xg = gates_ref[rows, :] + gbias_ref[...]
            xg = jnp.where(is_f, _log_sigmoid(xg), xg)
            hi = xg.astype(BF16).astype(F32)
            rest = xg - hi
            mid = rest.astype(BF16).astype(F32)
            packed = jnp.where(piece == 0, hi, jnp.where(piece == 1, mid, rest - mid))
            pk_ref[rows, :] = packed.astype(BF16)
            return carry
        lax.fori_loop(0, t_len // sel_rows, ml_pack, 0)


    def hg_prep(i, carry):
        rows = pl.ds(pl.multiple_of(i * BLK, BLK), BLK)
        q = _silu(hq_ref[rows, :])
        vt = jnp.transpose(hi_ref[rows, :]).astype(BF16)
        vt_ref[i] = vt
        kd_cols = []
        for d, zf_ref in enumerate((hff_ref, hfb_ref)):
            rev = d == 1
            lb = lb_all[d:d + 1]
            oml = 1.0 - lb
            sp, sn = _sigmoid_pair(zf_ref[rows, :])
            k = oml * sn
            b = _chunk_cumsum(jnp.log2(lb + oml * sp), in_pos, rev)
            b3, q3, k3 = _per_chunk(b), _per_chunk(q), _per_chunk(k)
            i_mid, i_last = (CHUNK - 1 - mid, 0) if rev else (mid, CHUNK - 1)
            b_mid, b_last = b3[:, i_mid:i_mid + 1], b3[:, i_last:i_last + 1]
            qa_ref[d, rows, :] = (q3 * jnp.exp2(b3 - b_mid)).reshape(BLK, HG_DK).astype(BF16)
            ka_ref[d, rows, :] = (k3 * jnp.exp2(b_mid - b3)).reshape(BLK, HG_DK).astype(BF16)
            qi = q * jnp.exp2(b)
            qx_ref[d, rows, :HG_DK] = first_chunk_only(qi)
            qx_ref[d, rows, HG_DK:] = second_chunk_only(qi)
            kd = (k3 * jnp.exp2(b_last - b3)).reshape(BLK, HG_DK)
            kd_cols += [first_chunk_only(kd), second_chunk_only(kd)]
            dec = jnp.exp2(b_last)
            for c in range(cpb):
                dec_ref[d, i * cpb + c] = dec[c]
        ds_t = _dot(vt, jnp.concatenate(kd_cols, axis=1))
        for d in range(2):
            for c in range(cpb):
                j = d * cpb + c
                ds_ref[d, i * cpb + c] = ds_t[:, j * HG_DK:(j + 1) * HG_DK]
        return carry

    if has_init:
        st0 = tuple(jnp.transpose(s0_ref[0, 0, d, 0]) for d in range(2))
    else:
        st0 = (jnp.zeros((HG_DV, HG_DK), F32),) * 2

    def hg_state(n, carry):
        new = []
        for d in range(2):
            cn = (n_chunks - 1 - n) if d == 1 else n
            stt_ref[d, cn] = carry[d].astype(BF16)
            new.append(dec_ref[d, cn] * carry[d] + ds_ref[d, cn])
        return tuple(new)

    def hg_out(i, carry):
        rows = pl.ds(pl.multiple_of(i * BLK, BLK), BLK)
        vt = vt_ref[i]
        lhs, rhs_t = [], []
        for d in range(2):
            att = jnp.where(masks[d], _dot_nt(qa_ref[d, rows, :], ka_ref[d, rows, :]), 0.0)
            lhs += [att.astype(BF16), qx_ref[d, rows, :]]
            rhs_t += [vt] + [stt_ref[d, i * cpb + c] for c in range(cpb)]
        o = _dot_nt(jnp.concatenate(lhs, axis=1), jnp.concatenate(rhs_t, axis=1))
        o = (o * lax.rsqrt(jnp.mean(o * o, axis=-1, keepdims=True) + EPS) * hgn_ref[...]
             * _silu(hgate_ref[rows, :]))
        hg_out_ref[rows, :] = o.astype(BF16)
        return carry


    r_i = lax.broadcasted_iota(jnp.int32, (LANES, 4 * LANES), 0)
    c_i = lax.broadcasted_iota(jnp.int32, (LANES, 4 * LANES), 1)
    cblk = jnp.right_shift(c_i, 7)
    target = h_idx + jnp.where(cblk == 0, 0, jnp.where(cblk == 1, 2 * N_HEADS,
                                                       jnp.where(cblk == 2, N_HEADS, 3 * N_HEADS)))
    onehot = jnp.where((jnp.bitwise_and(r_i, GATE_W - 1) == target) & (r_i < 3 * GATE_W), 1.0, 0.0).astype(BF16)

    def ml_select(i, carry):
        rows = pl.ds(pl.multiple_of(i * sel_rows, sel_rows), sel_rows)
        sel_ref[rows, :] = _dot(pk_ref[rows, :], onehot)
        return carry

    def ml_prep(i, carry):
        rows = pl.ds(pl.multiple_of(i * BLK, BLK), BLK)
        qk = mqk_ref[rows, :]
        kk = jnp.where(lane_lo, pltpu.roll(qk, ML_DQK, axis=1), qk) * (ML_DQK ** -0.5)
        v = mv_ref[rows, :]
        e_dir = []
        for d in range(2):
            rev = d == 1
            li = sel_ref[rows, (2 * d) * LANES:(2 * d + 1) * LANES]
            lf = sel_ref[rows, (2 * d + 1) * LANES:(2 * d + 2) * LANES]
            g = _chunk_cumsum(lf, in_pos, rev)
            u = li - g
            g_ref[d, rows, :] = g
            ut_ref[d, rows, :] = jnp.transpose(u)
            i_last = 0 if rev else CHUNK - 1
            g_last = _per_chunk(g)[:, i_last:i_last + 1]
            w_end = g_last + _per_chunk(u)
            m_loc = jnp.max(w_end, axis=1, keepdims=True)
            e_dir.append(jnp.exp(w_end - m_loc).reshape(BLK, LANES))
            for c in range(cpb):
                a_ref[d, i * cpb + c] = g_last[c]
                mloc_ref[d, i * cpb + c] = m_loc[c]
        wk = jnp.where(lane_lo, e_dir[0], e_dir[1]) * kk
        v_cols = jnp.concatenate([first_chunk_only(v), second_chunk_only(v)], axis=1)
        c_loc = _dot_tn(wk.astype(BF16), v_cols)
        n_loc = jnp.sum(_per_chunk(wk), axis=1, keepdims=True)
        for c in range(cpb):
            nloc_ref[i * cpb + c] = n_loc[c]
            for d in range(2):
                cloc_ref[d, i * cpb + c] = c_loc[d * ML_DQK:(d + 1) * ML_DQK, c * ML_DV:(c + 1) * ML_DV]
        return carry

    def prep_trip(t, carry):
        for u in range(grp):
            hg_prep(t * grp + u, carry)
        return ml_select(t, carry)
    lax.fori_loop(0, n_blocks // grp, prep_trip, 0)

    st_fin = lax.fori_loop(0, n_chunks, hg_state, st0)

    def mid_trip(t, carry):
        for u in range(grp):
            hg_out(t * grp + u, carry)
            ml_prep(t * grp + u, carry)
        return carry
    lax.fori_loop(0, n_blocks // grp, mid_trip, 0)

    if has_init:
        m0 = tuple(jnp.full((1, LANES), m0_ref[b_idx, d * N_HEADS + h_idx], F32) for d in range(2))
    else:
        m0 = (jnp.zeros((1, LANES), F32),) * 2

    def ml_stab(n, carry):
        new = []
        for d in range(2):
            cn = (n_chunks - 1 - n) if d == 1 else n
            m_st = carry[d]
            a, m_loc = a_ref[d, cn], mloc_ref[d, cn]
            m_new = jnp.maximum(a + m_st, m_loc)
            mst_ref[d, cn] = m_st
            sold_ref[d, cn] = jnp.exp(a + m_st - m_new)
            sloc_ref[d, cn] = jnp.exp(m_loc - m_new)
            new.append(m_new)
        return tuple(new)
    m_fin = lax.fori_loop(0, n_chunks, ml_stab, m0)

    if has_init:
        c0 = tuple(c0_ref[0, 0, d, 0] for d in range(2))
        n0 = tuple(jnp.concatenate([n0_ref[0, 0, d, 0]] * 2, axis=1) for d in range(2))
    else:
        c0 = (jnp.zeros((ML_DQK, ML_DV), F32),) * 2
        n0 = (jnp.zeros((1, LANES), F32),) * 2

    def ml_state(n, carry):
        cs, ns = carry
        new_c, new_n = [], []
        for d in range(2):
            cn = (n_chunks - 1 - n) if d == 1 else n
            cst_ref[d, cn] = cs[d].astype(BF16)
            nst_ref[d, cn] = ns[d]
            s_old, s_loc = sold_ref[d, cn], sloc_ref[d, cn]
            new_c.append(s_old * cs[d] + s_loc * cloc_ref[d, cn])
            new_n.append(s_old * ns[d] + s_loc * nloc_ref[cn])
        return tuple(new_c), tuple(new_n)
    c_fin, n_fin = lax.fori_loop(0, n_chunks, ml_state, (c0, n0))

    def ml_out(i, carry):
        rows = pl.ds(pl.multiple_of(i * BLK, BLK), BLK)
        chunks = pl.ds(i * cpb, cpb)
        qk = mqk_ref[rows, :]
        qb = qk[:, :ML_DQK].astype(BF16)
        kb = (qk[:, ML_DQK:] * (ML_DQK ** -0.5)).astype(BF16)
        vb = mv_ref[rows, :].astype(BF16)
        qk_raw = _dot_nt(qb, kb)
        qq = jnp.where(lane_lo, qk, pltpu.roll(qk, ML_DQK, axis=1))
        q_cols = jnp.where(lane_lo == row_lo, qq, 0.0).astype(BF16)
        c_rows = jnp.concatenate([cst_ref[d, chunks].reshape(BLK, ML_DV) for d in range(2)], axis=1)
        qc = _dot(q_cols, c_rows)
        parts = []
        for d in range(2):
            g = g_ref[d, rows, :]
            log_inter = (_per_chunk(g) + mst_ref[d, chunks]).reshape(BLK, LANES)
            log_d = jnp.where(masks[d], g + ut_ref[d, rows, :], -jnp.inf)
            m_t = jnp.maximum(log_inter, jnp.max(log_d, axis=-1, keepdims=True))
            s_inter = jnp.exp(log_inter - m_t)
            s_qk = qk_raw * jnp.exp(log_d - m_t)
            own_half = lane_lo if d == 0 else jnp.logical_not(lane_lo)
            q_n = jnp.where(own_half, (_per_chunk(qq) * nst_ref[d, chunks]).reshape(BLK, LANES), 0.0)
            den = jnp.sum(s_qk + s_inter * q_n, axis=-1, keepdims=True)
            parts.append((s_qk, s_inter, jnp.maximum(jnp.abs(den), jnp.exp(-m_t[:, 0:1]))))
        nv = _dot(jnp.concatenate([parts[0][0], parts[1][0]], axis=0).astype(BF16), vb)
        o = None
        for d in range(2):
            _, s_inter, den = parts[d]
            h = (nv[d * BLK:(d + 1) * BLK] + s_inter * qc[:, d * ML_DV:(d + 1) * ML_DV]) / den
            o = h if o is None else o + h
        hsum_ref[rows, :] = o
        return carry
    lax.fori_loop(0, n_blocks, ml_out, 0, unroll=min(2, n_blocks))

    def ml_norm(i, carry):
        rows = pl.ds(pl.multiple_of(i * BLK, BLK), BLK)
        o = hsum_ref[rows, :]
        mu = jnp.mean(o, axis=-1, keepdims=True)
        oc = o - mu
        var = jnp.mean(oc * oc, axis=-1, keepdims=True)
        o = oc * lax.rsqrt(var + EPS) * mln_ref[...] * _sigmoid(mo_ref[rows, :])
        ml_out_ref[rows, :] = o.astype(BF16)
        return carry
    lax.fori_loop(0, n_blocks, ml_norm, 0, unroll=min(4, n_blocks))

    if not has_init:
        for d in range(2):
            sfin_ref[0, d, 0] = jnp.transpose(st_fin[d])
            cfin_ref[0, d, 0] = c_fin[d]
            nfin_ref[0, d, 0] = n_fin[d][:, d * ML_DQK:(d + 1) * ML_DQK]
            mfin_ref[0, d, 0] = m_fin[d]


def _scan(proj, gates, lb_logits, hg_norm_g, gate_bias, ml_norm_g, init, *, bsz, t_len):
    has_init = init is not None
    m = bsz * t_len
    n_chunks = t_len // CHUNK
    n_blocks = t_len // BLK

    def col(sec):
        return pl.BlockSpec((t_len, HEAD_W), lambda b, h, sec=sec: (b, sec + h))

    in_specs = [col(SEC_HQ), col(SEC_HF_FW), col(SEC_HF_BW), col(SEC_HI), col(SEC_HGATE),
                col(SEC_MQK), col(SEC_MV), col(SEC_MO),
                pl.BlockSpec((t_len, LANES), lambda b, h: (b, 0)),
                pl.BlockSpec((2, 2, HEAD_W), lambda b, h: (0, 0, h)),
                pl.BlockSpec((1, HG_DV), lambda b, h: (0, 0)),
                pl.BlockSpec((1, LANES), lambda b, h: (0, 0)),
                pl.BlockSpec((1, ML_DV), lambda b, h: (0, h))]
    args = [proj] * 8 + [gates, lb_logits, hg_norm_g, gate_bias, ml_norm_g]
    out_specs = [pl.BlockSpec((t_len, HEAD_W), lambda b, h: (b, h)),
                 pl.BlockSpec((t_len, HEAD_W), lambda b, h: (b, h))]
    out_shape = [jax.ShapeDtypeStruct((m, N_HEADS * HG_DV), BF16),
                 jax.ShapeDtypeStruct((m, N_HEADS * ML_DV), BF16)]
    if has_init:
        s0, c0, n0, m0 = init
        in_specs += [pl.BlockSpec((1, 1, 2, 1, HG_DK, HG_DV), lambda b, h: (b, 0, 0, h, 0, 0)),
                     pl.BlockSpec((1, 1, 2, 1, ML_DQK, ML_DV), lambda b, h: (b, 0, 0, h, 0, 0)),
                     pl.BlockSpec((1, 1, 2, 1, 1, ML_DQK), lambda b, h: (b, 0, 0, h, 0, 0)),
                     pl.BlockSpec(memory_space=pltpu.SMEM)]
        args += [s0, c0, n0.reshape(n0.shape[:4] + (1, ML_DQK)), m0.reshape(bsz, 2 * N_HEADS)]
    else:
        out_specs += [pl.BlockSpec((1, 2, 1, HG_DK, HG_DV), lambda b, h: (b, 0, h, 0, 0)),
                      pl.BlockSpec((1, 2, 1, ML_DQK, ML_DV), lambda b, h: (b, 0, h, 0, 0)),
                      pl.BlockSpec((1, 2, 1, 1, ML_DQK), lambda b, h: (b, 0, h, 0, 0)),
                      pl.BlockSpec((1, 2, 1, 1, LANES), lambda b, h: (b, 0, h, 0, 0))]
        out_shape += [jax.ShapeDtypeStruct((bsz, 2, N_HEADS, HG_DK, HG_DV), F32),
                      jax.ShapeDtypeStruct((bsz, 2, N_HEADS, ML_DQK, ML_DV), F32),
                      jax.ShapeDtypeStruct((bsz, 2, N_HEADS, 1, ML_DQK), F32),
                      jax.ShapeDtypeStruct((bsz, 2, N_HEADS, 1, LANES), F32)]

    def per_chunk_row(*lead):
        return pltpu.VMEM(lead + (n_chunks, 1, LANES), F32)

    scratch = [pltpu.VMEM((2, t_len, HG_DK), BF16),
               pltpu.VMEM((2, t_len, HG_DK), BF16),
               pltpu.VMEM((2, t_len, 2 * HG_DK), BF16),
               pltpu.VMEM((n_blocks, HG_DV, BLK), BF16),
               pltpu.VMEM((2, n_chunks, HG_DV, HG_DK), F32),
               per_chunk_row(2),
               pltpu.VMEM((2, n_chunks, HG_DV, HG_DK), BF16),
               pltpu.VMEM((t_len, 4 * LANES), F32),
               pltpu.VMEM((2, t_len, LANES), F32),
               pltpu.VMEM((2, t_len, BLK), F32),
               pltpu.VMEM((2, n_chunks, ML_DQK, ML_DV), F32),
               per_chunk_row(),
               per_chunk_row(2),
               per_chunk_row(2),
               per_chunk_row(2),
               per_chunk_row(2),
               per_chunk_row(2),
               pltpu.VMEM((2, n_chunks, ML_DQK, ML_DV), BF16),
               per_chunk_row(2),
               pltpu.VMEM((t_len, ML_DV), F32),
               pltpu.VMEM((t_len, LANES), BF16)]
    kern = functools.partial(_scan_kernel, t_len=t_len, has_init=has_init)
    return pl.pallas_call(
        kern,
        grid=(bsz, N_HEADS),
        in_specs=in_specs,
        out_specs=out_specs,
        out_shape=out_shape,
        scratch_shapes=scratch,
        compiler_params=pltpu.CompilerParams(dimension_semantics=("arbitrary", "arbitrary"),
                                             vmem_limit_bytes=VMEM_LIMIT),
        name="scan_init" if has_init else "scan_ctx",
    )(*args)


def _mix_out_kernel(hg_ref, ml_ref, ga_ref, gb_ref, x_ref, gt_ref, sh_ref, sc_ref, gpost_ref, gpre_ref,
                    wuh_ref, wum_ref, wo_ref, x1_ref, h2_ref):
    y_hg = _dot(hg_ref[...], wuh_ref[...])
    y_ml = _dot(ml_ref[...], wum_ref[...])
    merged = ga_ref[...].astype(F32) * y_hg + gb_ref[...].astype(F32) * y_ml
    y = _dot(merged.astype(BF16), wo_ref[...])
    x1 = x_ref[...] + gt_ref[0] * _rms_rows(y, gpost_ref[...])
    x1_ref[...] = x1
    h2 = _rms_rows(x1, gpre_ref[...]) * (1.0 + sc_ref[0]) + sh_ref[0]
    h2_ref[...] = h2.astype(BF16)


def _mix_out(hg_act, ml_act, mgate, x2d, mod3, mod_row_fn, g_post_mix, g_pre_ffn, w_up_hg, w_up_ml, w_out, *, tm):
    m = x2d.shape[0]

    def resident(shape):
        return pl.BlockSpec(shape, lambda i: (0, 0), pipeline_mode=pl.Buffered(1))

    return pl.pallas_call(
        _mix_out_kernel,
        grid=(m // tm,),
        in_specs=[pl.BlockSpec((tm, N_HEADS * HG_DV), lambda i: (i, 0)),
                  pl.BlockSpec((tm, N_HEADS * ML_DV), lambda i: (i, 0)),
                  pl.BlockSpec((tm, D_MODEL), lambda i: (i, 0)),
                  pl.BlockSpec((tm, D_MODEL), lambda i: (i, 1)),
                  pl.BlockSpec((tm, D_MODEL), lambda i: (i, 0)),
                  pl.BlockSpec((1, 1, D_MODEL), lambda i: (mod_row_fn(i), 0, 2)),
                  pl.BlockSpec((1, 1, D_MODEL), lambda i: (mod_row_fn(i), 0, 3)),
                  pl.BlockSpec((1, 1, D_MODEL), lambda i: (mod_row_fn(i), 0, 4)),
                  resident((1, D_MODEL)),
                  resident((1, D_MODEL)),
                  resident((N_HEADS * HG_DV, D_MODEL)),
                  resident((N_HEADS * ML_DV, D_MODEL)),
                  resident((D_MODEL, D_MODEL))],
        out_specs=[pl.BlockSpec((tm, D_MODEL), lambda i: (i, 0)),
                   pl.BlockSpec((tm, D_MODEL), lambda i: (i, 0))],
        out_shape=[jax.ShapeDtypeStruct((m, D_MODEL), F32),
                   jax.ShapeDtypeStruct((m, D_MODEL), BF16)],
        compiler_params=pltpu.CompilerParams(dimension_semantics=("arbitrary",),
                                             vmem_limit_bytes=VMEM_LIMIT),
        name="mix_out",
    )(hg_act, ml_act, mgate, mgate, x2d, mod3, mod3, mod3, g_post_mix, g_pre_ffn, w_up_hg, w_up_ml, w_out)


def _ffn_kernel(h2_ref, x1_ref, gt_ref, gpost_ref, wa_ref, wb_ref, wo_ref, out_ref):
    j = pl.program_id(1)

    @pl.when(j == 0)
    def _():
        out_ref[...] = jnp.zeros_like(out_ref)

    h2 = h2_ref[...]
    hid = (_silu(_dot(h2, wa_ref[...])) * _dot(h2, wb_ref[...])).astype(BF16)
    for n0 in range(0, D_MODEL, FFN_ACC_COLS):
        out_ref[:, n0:n0 + FFN_ACC_COLS] += _dot(hid, wo_ref[:, n0:n0 + FFN_ACC_COLS])

    @pl.when(j == pl.num_programs(1) - 1)
    def _():
        rows = 128

        def body(i, carry):
            r = pl.ds(pl.multiple_of(i * rows, rows), rows)
            out_ref[r, :] = x1_ref[r, :] + gt_ref[0] * _rms_rows(out_ref[r, :], gpost_ref[...])
            return carry
        lax.fori_loop(0, out_ref.shape[0] // rows, body, 0)


def _ffn(h2, x1, mod3, mod_row_fn, g_post_ffn, w_ffn_in, w_ffn_out, *, tm, tf):
    m = x1.shape[0]
    nf = D_FF // tf
    return pl.pallas_call(
        _ffn_kernel,
        grid=(m // tm, nf),
        in_specs=[pl.BlockSpec((tm, D_MODEL), lambda i, j: (i, 0)),
                  pl.BlockSpec((tm, D_MODEL), lambda i, j: (i, 0)),
                  pl.BlockSpec((1, 1, D_MODEL), lambda i, j: (mod_row_fn(i), 0, 5)),
                  pl.BlockSpec((1, D_MODEL), lambda i, j: (0, 0)),
                  pl.BlockSpec((D_MODEL, tf), lambda i, j: (0, j)),
                  pl.BlockSpec((D_MODEL, tf), lambda i, j, nf=nf: (0, j + nf)),
                  pl.BlockSpec((tf, D_MODEL), lambda i, j: (j, 0))],
        out_specs=pl.BlockSpec((tm, D_MODEL), lambda i, j: (i, 0)),
        out_shape=jax.ShapeDtypeStruct((m, D_MODEL), F32),
        compiler_params=pltpu.CompilerParams(dimension_semantics=("arbitrary", "arbitrary"),
                                             vmem_limit_bytes=VMEM_LIMIT_FFN),
        name="ffn",
    )(h2, x1, mod3, g_post_ffn, w_ffn_in, w_ffn_in, w_ffn_out)


def _w_prep_kernel(wt_ref, main_ref, gate_ref):
    x = wt_ref[...]
    pieces = [x[:W_MQ]]
    for h in range(N_HEADS):
        pieces += [x[W_MQ + h * ML_DQK:W_MQ + (h + 1) * ML_DQK], x[W_MK + h * ML_DQK:W_MK + (h + 1) * ML_DQK]]
    pieces += [x[W_MV:W_GATES], x[W_GATES + GATE_W:]]
    main_ref[...] = jnp.concatenate(pieces, axis=0).astype(BF16)
    gate_ref[...] = jnp.concatenate([x[W_GATES:W_GATES + GATE_W]] * (LANES // GATE_W), axis=0).astype(BF16)


def _prep_w_in(w_t):
    cols = 256
    return pl.pallas_call(
        _w_prep_kernel,
        grid=(D_MODEL // cols,),
        in_specs=[pl.BlockSpec((IN_W, cols), lambda i: (0, i))],
        out_specs=[pl.BlockSpec((PROJ_W, cols), lambda i: (0, i)),
                   pl.BlockSpec((LANES, cols), lambda i: (0, i))],
        out_shape=[jax.ShapeDtypeStruct((PROJ_W, D_MODEL), BF16),
                   jax.ShapeDtypeStruct((LANES, D_MODEL), BF16)],
        compiler_params=pltpu.CompilerParams(dimension_semantics=("arbitrary",),
                                             vmem_limit_bytes=VMEM_LIMIT),
        name="w_prep",
    )(w_t)


def _layer_group(x, mod3, mod_row_fn, init, p, *, tm_in, tn_in, tm_mix, tm_ffn, tf):
    bsz, t_len, _ = x.shape
    x2d = x.reshape(bsz * t_len, D_MODEL)
    proj, mgate, gates = _in_proj(x2d, mod3, lambda i: mod_row_fn(i, tm_in), p["g_pre_mix"], p["w_main"],
                                  p["w_gate"], tm=tm_in, tn=tn_in)
    outs = _scan(proj, gates, p["lb_logits"], p["hg_norm_g"], p["gate_bias"], p["ml_norm_g"], init,
                 bsz=bsz, t_len=t_len)
    hg_act, ml_act = outs[0], outs[1]
    x1, h2 = _mix_out(hg_act, ml_act, mgate, x2d, mod3, lambda i: mod_row_fn(i, tm_mix), p["g_post_mix"],
                      p["g_pre_ffn"], p["w_up_hg"], p["w_up_ml"], p["w_out"], tm=tm_mix)
    y = _ffn(h2, x1, mod3, lambda i: mod_row_fn(i, tm_ffn), p["g_post_ffn"], p["w_ffn_in"], p["w_ffn_out"],
             tm=tm_ffn, tf=tf)
    return y.reshape(bsz, t_len, D_MODEL), outs[2:]


def kernel(x_prompt, x_sample, c, state_hgrn_s, state_mlstm_c, state_mlstm_n, state_mlstm_m, c_ctx, w_mod, b_mod,
           norm_pre_mix, norm_post_mix, norm_pre_ffn, norm_post_ffn, w_in, hgrn_lb_logits, hgrn_norm_g, mlstm_b_i,
           mlstm_b_f, mlstm_norm_g, w_up_hgrn, w_up_mlstm, w_out, w_ffn_in, w_ffn_out):
    bsz_p, t_p, _ = x_prompt.shape
    bsz_s, t_s, _ = x_sample.shape

    cond = jnp.concatenate([c_ctx[None, :], c, jnp.zeros((MOD_ROWS - 1 - bsz_s, D_MODEL), F32)], axis=0)
    mod = _modulation(cond, w_mod[0], b_mod[0][None, :])
    mod3 = mod.reshape(MOD_ROWS, 1, N_MOD)

    w_main, w_gate = _prep_w_in(jnp.transpose(w_in[0]))
    p = dict(
        w_main=w_main, w_gate=w_gate,
        g_pre_mix=norm_pre_mix[0][None, :], g_post_mix=norm_post_mix[0][None, :],
        g_pre_ffn=norm_pre_ffn[0][None, :], g_post_ffn=norm_post_ffn[0][None, :],
        lb_logits=hgrn_lb_logits, hg_norm_g=hgrn_norm_g[0][None, :],
        gate_bias=jnp.tile(jnp.concatenate([mlstm_b_i[0], mlstm_b_f[0]]), LANES // GATE_W)[None, :],
        ml_norm_g=mlstm_norm_g[0][None, :],
        w_up_hg=w_up_hgrn[0].astype(BF16), w_up_ml=w_up_mlstm[0].astype(BF16), w_out=w_out[0].astype(BF16),
        w_ffn_in=w_ffn_in[0].astype(BF16), w_ffn_out=w_ffn_out[0].astype(BF16),
    )
    tiles_p = dict(tm_in=min(1024, bsz_p * t_p), tn_in=1024, tm_mix=min(512, bsz_p * t_p),
                   tm_ffn=min(1024, bsz_p * t_p), tf=512)
    tiles_s = dict(tm_in=min(1024, t_s), tn_in=1024, tm_mix=min(512, t_s), tm_ffn=min(1024, t_s), tf=512)
    y_p, ctx_states = _layer_group(x_prompt, mod3, lambda i, tm: 0, None, p, **tiles_p)
    init = (state_hgrn_s, state_mlstm_c, state_mlstm_n, state_mlstm_m[:, 0])
    y_s, _ = _layer_group(x_sample, mod3, lambda i, tm: 1 + (i * tm) // t_s, init, p, **tiles_s)

    s_fin, c_fin, n_fin, m_fin = ctx_states
    new_hgrn_s = s_fin[:, None]
    new_mlstm_c = c_fin[:, None]
    new_mlstm_n = n_fin[:, None, :, :, 0, :]
    new_mlstm_m = m_fin[:, None, :, :, 0, 0]
    return (y_p, y_s, new_hgrn_s, new_mlstm_c, new_mlstm_n, new_mlstm_m)
```

```python
import functools

import jax
import jax.numpy as jnp
from jax import lax
from jax.experimental import pallas as pl
from jax.experimental.pallas import tpu as pltpu

F32 = jnp.float32
BF16 = jnp.bfloat16

D_MODEL = 2048
N_HEADS = 8
HG_DK = 128
HG_DV = 128
ML_DQK = 64
ML_DV = 128
HEAD_W = 128
D_FF = 5632
CHUNK = 64
EPS = 1e-6
N_MOD = 6 * D_MODEL
LANES = 128
MOD_ROWS = 16
BLK = 2 * CHUNK
SCAN_ROWS_PER_STEP = 1024
BLOCKS_PER_TRIP = 4
MGATE_COLS = 256
FFN_ACC_COLS = 512
GATE_W = 4 * N_HEADS

SEC_HQ, SEC_HF_FW, SEC_HF_BW, SEC_HI, SEC_HGATE = 0, 8, 16, 24, 32
SEC_MQK, SEC_MV, SEC_MO = 40, 48, 56
SEC_GA, SEC_GB = 64, 80
PROJ_W = 96 * LANES
PLAIN_W = SEC_GA * LANES

W_MQ = 5 * N_HEADS * HG_DK
W_MK = W_MQ + N_HEADS * ML_DQK
W_MV = W_MK + N_HEADS * ML_DQK
W_GATES = W_MV + N_HEADS * ML_DV
IN_W = W_GATES + 4 * N_HEADS + N_HEADS * ML_DV + 2 * D_MODEL

VMEM_LIMIT = 56 * 1024 * 1024
VMEM_LIMIT_FFN = 60 * 1024 * 1024


def _dot(a, b):
    return jnp.dot(a, b, preferred_element_type=F32)


def _dot_nt(a, b):
    return lax.dot_general(a, b, (((1,), (1,)), ((), ())), preferred_element_type=F32)


def _dot_tn(a, b):
    return lax.dot_general(a, b, (((0,), (0,)), ((), ())), preferred_element_type=F32)


def _sigmoid_pair(z):
    t = 0.5 * jnp.tanh(0.5 * z)
    return 0.5 + t, 0.5 - t


def _sigmoid(z):
    return 0.5 * jnp.tanh(0.5 * z) + 0.5


def _silu(z):
    return z * _sigmoid(z)


def _log_sigmoid(z):
    return jnp.minimum(z, 0.0) - jnp.log1p(jnp.exp(-jnp.abs(z)))


def _mod_kernel(c_ref, w_ref, b_ref, o_ref):
    a = _silu(c_ref[...]).astype(BF16)
    o_ref[...] = _dot(a, w_ref[...].astype(BF16)) + b_ref[...]


def _modulation(cond, w_mod, b_mod):
    tn = 1024
    return pl.pallas_call(
        _mod_kernel,
        grid=(N_MOD // tn,),
        in_specs=[pl.BlockSpec((MOD_ROWS, D_MODEL), lambda j: (0, 0)),
                  pl.BlockSpec((D_MODEL, tn), lambda j: (0, j)),
                  pl.BlockSpec((1, tn), lambda j: (0, j))],
        out_specs=pl.BlockSpec((MOD_ROWS, tn), lambda j: (0, j)),
        out_shape=jax.ShapeDtypeStruct((MOD_ROWS, N_MOD), F32),
        compiler_params=pltpu.CompilerParams(dimension_semantics=("arbitrary",),
                                             vmem_limit_bytes=VMEM_LIMIT),
        name="modulation",
    )(cond, w_mod, b_mod)


def _rms_rows(x, g):
    return x * lax.rsqrt(jnp.mean(x * x, axis=-1, keepdims=True) + EPS) * g


def _inproj_kernel(x_ref, sh_ref, sc_ref, g_ref, w_ref, wg_ref, proj_ref, mgate_ref, gates_ref, xn_ref, *,
                   tm, n_plain):
    rows = 128
    j = pl.program_id(1)

    @pl.when(j == 0)
    def _():
        for r0 in range(0, tm, rows):
            x = x_ref[r0:r0 + rows, :]
            h = _rms_rows(x, g_ref[...]) * (1.0 + sc_ref[0]) + sh_ref[0]
            hb = h.astype(BF16)
            xn_ref[r0:r0 + rows, :] = hb
            gates_ref[r0:r0 + rows, :] = _dot_nt(hb, wg_ref[...])

    @pl.when(j < n_plain)
    def _():
        proj_ref[...] = _dot_nt(xn_ref[...], w_ref[...])

    @pl.when(j >= n_plain)
    def _():
        for c0 in range(0, w_ref.shape[0], MGATE_COLS):
            z = _dot_nt(xn_ref[...], w_ref[c0:c0 + MGATE_COLS, :])
            mgate_ref[:, c0:c0 + MGATE_COLS] = _sigmoid(z).astype(BF16)


def _in_proj(x2d, mod3, mod_row_fn, g_pre, w_main_t, w_gate_t, *, tm, tn):
    m = x2d.shape[0]
    n_plain = PLAIN_W // tn
    kern = functools.partial(_inproj_kernel, tm=tm, n_plain=n_plain)
    return pl.pallas_call(
        kern,
        grid=(m // tm, PROJ_W // tn),
        in_specs=[pl.BlockSpec((tm, D_MODEL), lambda i, j: (i, 0)),
                  pl.BlockSpec((1, 1, D_MODEL), lambda i, j: (mod_row_fn(i), 0, 0)),
                  pl.BlockSpec((1, 1, D_MODEL), lambda i, j: (mod_row_fn(i), 0, 1)),
                  pl.BlockSpec((1, D_MODEL), lambda i, j: (0, 0)),
                  pl.BlockSpec((tn, D_MODEL), lambda i, j: (j, 0)),
                  pl.BlockSpec((LANES, D_MODEL), lambda i, j: (0, 0))],
        out_specs=[pl.BlockSpec((tm, tn), lambda i, j: (i, jnp.minimum(j, n_plain - 1))),
                   pl.BlockSpec((tm, tn), lambda i, j: (i, jnp.maximum(j - n_plain, 0))),
                   pl.BlockSpec((tm, LANES), lambda i, j: (i, 0))],
        out_shape=[jax.ShapeDtypeStruct((m, PLAIN_W), F32),
                   jax.ShapeDtypeStruct((m, PROJ_W - PLAIN_W), BF16),
                   jax.ShapeDtypeStruct((m, LANES), F32)],
        scratch_shapes=[pltpu.VMEM((tm, D_MODEL), BF16)],
        compiler_params=pltpu.CompilerParams(dimension_semantics=("arbitrary", "arbitrary"),
                                             vmem_limit_bytes=VMEM_LIMIT),
        name="in_proj",
    )(x2d, mod3, mod3, g_pre, w_main_t, w_gate_t)


def _chunk_cumsum(x, pos, rev):
    n = x.shape[0]
    s = 1
    while s < CHUNK:
        if rev:
            x = x + jnp.where(pos < CHUNK - s, pltpu.roll(x, n - s, axis=0), 0.0)
        else:
            x = x + jnp.where(pos >= s, pltpu.roll(x, s, axis=0), 0.0)
        s *= 2
    return x


def _per_chunk(x):
    return x.reshape(BLK // CHUNK, CHUNK, x.shape[-1])


N_SCAN_IN = 13


def _scan_kernel(*refs, t_len, has_init, hps):
    n_io = N_SCAN_IN + (4 if has_init else 0) + 2 + (0 if has_init else 4)
    lead = pl.program_id(1) * hps
    for hh in range(hps):
        hcol = slice(hh * HEAD_W, (hh + 1) * HEAD_W)
        head = slice(hh, hh + 1)
        views = [r.at[:, hcol] for r in refs[:8]]
        views += [refs[8], refs[9].at[:, :, hcol], refs[10], refs[11], refs[12].at[:, hcol]]
        pos = N_SCAN_IN
        if has_init:
            views += [r.at[:, :, :, head] for r in refs[pos:pos + 3]] + [refs[pos + 3]]
            pos += 4
        views += [r.at[:, hcol] for r in refs[pos:pos + 2]]
        pos += 2
        if not has_init:
            views += [r.at[:, :, head] for r in refs[pos:pos + 4]]
        views += [r.at[hh] for r in refs[n_io:-1]] + [refs[-1]]
        _scan_head(views, lead + hh, pack_gates=hh == 0, t_len=t_len, has_init=has_init)


def _scan_head(refs, h_idx, *, pack_gates, t_len, has_init):
    (hq_ref, hff_ref, hfb_ref, hi_ref, hgate_ref, mqk_ref, mv_ref, mo_ref, gates_ref,
     lbl_ref, hgn_ref, gbias_ref, mln_ref) = refs[:13]
    pos = 13
    if has_init:
        s0_ref, c0_ref, n0_ref, m0_ref = refs[pos:pos + 4]
        pos += 4
    hg_out_ref, ml_out_ref = refs[pos:pos + 2]
    pos += 2
    if not has_init:
        sfin_ref, cfin_ref, nfin_ref, mfin_ref = refs[pos:pos + 4]
        pos += 4
    (qa_ref, ka_ref, qx_ref, vt_ref, ds_ref, dec_ref, stt_ref,
     sel_ref, g_ref, ut_ref, cloc_ref, nloc_ref, a_ref, mloc_ref, mst_ref, sold_ref, sloc_ref,
     cst_ref, nst_ref, hsum_ref, pk_ref) = refs[pos:]

    b_idx = pl.program_id(0)
    n_chunks = t_len // CHUNK
    n_blocks = t_len // BLK
    cpb = BLK // CHUNK
    mid = CHUNK // 2

    row = lax.broadcasted_iota(jnp.int32, (BLK, BLK), 0)
    col = lax.broadcasted_iota(jnp.int32, (BLK, BLK), 1)
    same_chunk = jnp.right_shift(row, 6) == jnp.right_shift(col, 6)
    masks = (same_chunk & (col <= row), same_chunk & (col >= row))
    in_pos = jnp.bitwise_and(row, CHUNK - 1)
    row_lo = row < CHUNK
    lane_lo = col < ML_DQK
    lane = lax.broadcasted_iota(jnp.int32, (1, LANES), 1)

    def first_chunk_only(x):
        return jnp.concatenate([x[:CHUNK], jnp.zeros_like(x[CHUNK:])], axis=0).astype(BF16)

    def second_chunk_only(x):
        return jnp.concatenate([jnp.zeros_like(x[:CHUNK]), x[CHUNK:]], axis=0).astype(BF16)

    logits = lbl_ref[...]
    l0, l1 = logits[:, 0, :], logits[:, 1, :]
    mx = jnp.maximum(l0, l1)
    e0, e1 = jnp.exp(l0 - mx), jnp.exp(l1 - mx)
    lb_all = e0 / (e0 + e1)

    is_f = jnp.bitwise_and(lane, GATE_W - 1) >= 2 * N_HEADS
    piece = jnp.right_shift(lane, 5)
    grp = min(BLOCKS_PER_TRIP, n_blocks)
    seq_unroll = n_chunks <= BLOCKS_PER_TRIP
    sel_rows = grp * BLK

    if pack_gates:
        @pl.when(h_idx == 0)
        def _():
            def ml_pack(i, carry):
                rows = pl.ds(pl.multiple_of(i * sel_rows, sel_rows), sel_rows)
                xg = gates_ref[rows, :] + gbias_ref[...]
                xg = jnp.where(is_f, _log_sigmoid(xg), xg)
                hi = xg.astype(BF16).astype(F32)
                rest = xg - hi
                mid = rest.astype(BF16).astype(F32)
                packed = jnp.where(piece == 0, hi, jnp.where(piece == 1, mid, rest - mid))
                pk_ref[rows, :] = packed.astype(BF16)
                return carry
            lax.fori_loop(0, t_len // sel_rows, ml_pack, 0)


    def hg_prep(i, carry):
        rows = pl.ds(pl.multiple_of(i * BLK, BLK), BLK)
        q = _silu(hq_ref[rows, :])
        vt = jnp.transpose(hi_ref[rows, :]).astype(BF16)
        vt_ref[i] = vt
        kd_cols = []
        for d, zf_ref in enumerate((hff_ref, hfb_ref)):
            rev = d == 1
            lb = lb_all[d:d + 1]
            oml = 1.0 - lb
            sp, sn = _sigmoid_pair(zf_ref[rows, :])
            k = oml * sn
            b = _chunk_cumsum(jnp.log2(lb + oml * sp), in_pos, rev)
            b3, q3, k3 = _per_chunk(b), _per_chunk(q), _per_chunk(k)
            i_mid, i_last = (CHUNK - 1 - mid, 0) if rev else (mid, CHUNK - 1)
            b_mid, b_last = b3[:, i_mid:i_mid + 1], b3[:, i_last:i_last + 1]
            qa_ref[d, rows, :] = (q3 * jnp.exp2(b3 - b_mid)).reshape(BLK, HG_DK).astype(BF16)
            ka_ref[d, rows, :] = (k3 * jnp.exp2(b_mid - b3)).reshape(BLK, HG_DK).astype(BF16)
            qi = q * jnp.exp2(b)
            qx_ref[d, rows, :HG_DK] = first_chunk_only(qi)
            qx_ref[d, rows, HG_DK:] = second_chunk_only(qi)
            kd = (k3 * jnp.exp2(b_last - b3)).reshape(BLK, HG_DK)
            kd_cols += [first_chunk_only(kd), second_chunk_only(kd)]
            dec = jnp.exp2(b_last)
            for c in range(cpb):
                dec_ref[d, i * cpb + c] = dec[c]
        ds_t = _dot(vt, jnp.concatenate(kd_cols, axis=1))
        for d in range(2):
            for c in range(cpb):
                j = d * cpb + c
                ds_ref[d, i * cpb + c] = ds_t[:, j * HG_DK:(j + 1) * HG_DK]
        return carry

    if has_init:
        st0 = tuple(jnp.transpose(s0_ref[0, 0, d, 0]) for d in range(2))
    else:
        st0 = (jnp.zeros((HG_DV, HG_DK), F32),) * 2

    def hg_state(n, carry):
        new = []
        for d in range(2):
            cn = (n_chunks - 1 - n) if d == 1 else n
            stt_ref[d, cn] = carry[d].astype(BF16)
            new.append(dec_ref[d, cn] * carry[d] + ds_ref[d, cn])
        return tuple(new)

    def hg_out(i, carry):
        rows = pl.ds(pl.multiple_of(i * BLK, BLK), BLK)
        vt = vt_ref[i]
        lhs, rhs_t = [], []
        for d in range(2):
            att = jnp.where(masks[d], _dot_nt(qa_ref[d, rows, :], ka_ref[d, rows, :]), 0.0)
            lhs += [att.astype(BF16), qx_ref[d, rows, :]]
            rhs_t += [vt] + [stt_ref[d, i * cpb + c] for c in range(cpb)]
        o = _dot_nt(jnp.concatenate(lhs, axis=1), jnp.concatenate(rhs_t, axis=1))
        o = (o * lax.rsqrt(jnp.mean(o * o, axis=-1, keepdims=True) + EPS) * hgn_ref[...]
             * _silu(hgate_ref[rows, :]))
        hg_out_ref[rows, :] = o.astype(BF16)
        return carry


    r_i = lax.broadcasted_iota(jnp.int32, (LANES, 4 * LANES), 0)
    c_i = lax.broadcasted_iota(jnp.int32, (LANES, 4 * LANES), 1)
    cblk = jnp.right_shift(c_i, 7)
    target = h_idx + jnp.where(cblk == 0, 0, jnp.where(cblk == 1, 2 * N_HEADS,
                                                       jnp.where(cblk == 2, N_HEADS, 3 * N_HEADS)))
    onehot = jnp.where((jnp.bitwise_and(r_i, GATE_W - 1) == target) & (r_i < 3 * GATE_W), 1.0, 0.0).astype(BF16)

    def ml_select(i, carry):
        rows = pl.ds(pl.multiple_of(i * sel_rows, sel_rows), sel_rows)
        sel_ref[rows, :] = _dot(pk_ref[rows, :], onehot)
        return carry

    def ml_prep(i, carry):
        rows = pl.ds(pl.multiple_of(i * BLK, BLK), BLK)
        qk = mqk_ref[rows, :]
        kk = jnp.where(lane_lo, pltpu.roll(qk, ML_DQK, axis=1), qk) * (ML_DQK ** -0.5)
        v = mv_ref[rows, :]
        e_dir = []
        for d in range(2):
            rev = d == 1
            li = sel_ref[rows, (2 * d) * LANES:(2 * d + 1) * LANES]
            lf = sel_ref[rows, (2 * d + 1) * LANES:(2 * d + 2) * LANES]
            g = _chunk_cumsum(lf, in_pos, rev)
            u = li - g
            g_ref[d, rows, :] = g
            ut_ref[d, rows, :] = jnp.transpose(u)
            i_last = 0 if rev else CHUNK - 1
            g_last = _per_chunk(g)[:, i_last:i_last + 1]
            w_end = g_last + _per_chunk(u)
            m_loc = jnp.max(w_end, axis=1, keepdims=True)
            e_dir.append(jnp.exp(w_end - m_loc).reshape(BLK, LANES))
            for c in range(cpb):
                a_ref[d, i * cpb + c] = g_last[c]
                mloc_ref[d, i * cpb + c] = m_loc[c]
        wk = jnp.where(lane_lo, e_dir[0], e_dir[1]) * kk
        v_cols = jnp.concatenate([first_chunk_only(v), second_chunk_only(v)], axis=1)
        c_loc = _dot_tn(wk.astype(BF16), v_cols)
        n_loc = jnp.sum(_per_chunk(wk), axis=1, keepdims=True)
        for c in range(cpb):
            nloc_ref[i * cpb + c] = n_loc[c]
            for d in range(2):
                cloc_ref[d, i * cpb + c] = c_loc[d * ML_DQK:(d + 1) * ML_DQK, c * ML_DV:(c + 1) * ML_DV]
        return carry

    def sel_trip(t, carry):
        ml_select(t, carry)
        for u in range(grp):
            ml_prep(t * grp + u, carry)
        return carry
    lax.fori_loop(0, n_blocks // grp, sel_trip, 0)

    if has_init:
        m0 = tuple(jnp.full((1, LANES), m0_ref[b_idx, d * N_HEADS + h_idx], F32) for d in range(2))
    else:
        m0 = (jnp.zeros((1, LANES), F32),) * 2

    def ml_stab(n, carry):
        new = []
        for d in range(2):
            cn = (n_chunks - 1 - n) if d == 1 else n
            m_st = carry[d]
            a, m_loc = a_ref[d, cn], mloc_ref[d, cn]
            m_new = jnp.maximum(a + m_st, m_loc)
            mst_ref[d, cn] = m_st
            sold_ref[d, cn] = jnp.exp(a + m_st - m_new)
            sloc_ref[d, cn] = jnp.exp(m_loc - m_new)
            new.append(m_new)
        return tuple(new)
    m_fin = lax.fori_loop(0, n_chunks, ml_stab, m0, unroll=seq_unroll)

    if has_init:
        c0 = tuple(c0_ref[0, 0, d, 0] for d in range(2))
        n0 = tuple(jnp.concatenate([n0_ref[0, 0, d, 0]] * 2, axis=1) for d in range(2))
    else:
        c0 = (jnp.zeros((ML_DQK, ML_DV), F32),) * 2
        n0 = (jnp.zeros((1, LANES), F32),) * 2

    def ml_state(n, carry):
        cs, ns = carry
        new_c, new_n = [], []
        for d in range(2):
            cn = (n_chunks - 1 - n) if d == 1 else n
            cst_ref[d, cn] = cs[d].astype(BF16)
            nst_ref[d, cn] = ns[d]
            s_old, s_loc = sold_ref[d, cn], sloc_ref[d, cn]
            new_c.append(s_old * cs[d] + s_loc * cloc_ref[d, cn])
            new_n.append(s_old * ns[d] + s_loc * nloc_ref[cn])
        return tuple(new_c), tuple(new_n)
    c_fin, n_fin = lax.fori_loop(0, n_chunks, ml_state, (c0, n0), unroll=seq_unroll)

    def ml_out(i, carry):
        rows = pl.ds(pl.multiple_of(i * BLK, BLK), BLK)
        chunks = pl.ds(i * cpb, cpb)
        qk = mqk_ref[rows, :]
        qb = qk[:, :ML_DQK].astype(BF16)
        kb = (qk[:, ML_DQK:] * (ML_DQK ** -0.5)).astype(BF16)
        vb = mv_ref[rows, :].astype(BF16)
        qk_raw = _dot_nt(qb, kb)
        qq = jnp.where(lane_lo, qk, pltpu.roll(qk, ML_DQK, axis=1))
        q_cols = jnp.where(lane_lo == row_lo, qq, 0.0).astype(BF16)
        c_rows = jnp.concatenate([cst_ref[d, chunks].reshape(BLK, ML_DV) for d in range(2)], axis=1)
        qc = _dot(q_cols, c_rows)
        parts = []
        for d in range(2):
            g = g_ref[d, rows, :]
            log_inter = (_per_chunk(g) + mst_ref[d, chunks]).reshape(BLK, LANES)
            log_d = jnp.where(masks[d], g + ut_ref[d, rows, :], -jnp.inf)
            m_t = jnp.maximum(log_inter, jnp.max(log_d, axis=-1, keepdims=True))
            s_inter = jnp.exp(log_inter - m_t)
            s_qk = qk_raw * jnp.exp(log_d - m_t)
            own_half = lane_lo if d == 0 else jnp.logical_not(lane_lo)
            q_n = jnp.where(own_half, (_per_chunk(qq) * nst_ref[d, chunks]).reshape(BLK, LANES), 0.0)
            den = jnp.sum(s_qk + s_inter * q_n, axis=-1, keepdims=True)
            parts.append((s_qk, s_inter, jnp.maximum(jnp.abs(den), jnp.exp(-m_t[:, 0:1]))))
        nv = _dot(jnp.concatenate([parts[0][0], parts[1][0]], axis=0).astype(BF16), vb)
        o = None
        for d in range(2):
            _, s_inter, den = parts[d]
            h = (nv[d * BLK:(d + 1) * BLK] + s_inter * qc[:, d * ML_DV:(d + 1) * ML_DV]) / den
            o = h if o is None else o + h
        hsum_ref[rows, :] = o
        return carry

    def ml_norm(i, carry):
        rows = pl.ds(pl.multiple_of(i * BLK, BLK), BLK)
        o = hsum_ref[rows, :]
        mu = jnp.mean(o, axis=-1, keepdims=True)
        oc = o - mu
        var = jnp.mean(oc * oc, axis=-1, keepdims=True)
        o = oc * lax.rsqrt(var + EPS) * mln_ref[...] * _sigmoid(mo_ref[rows, :])
        ml_out_ref[rows, :] = o.astype(BF16)
        return carry
    pair = min(2, n_blocks)

    def out_trip(t, carry):
        for u in range(pair):
            ml_out(t * pair + u, carry)
            hg_prep(t * pair + u, carry)
        return carry
    lax.fori_loop(0, n_blocks // pair, out_trip, 0)

    st_fin = lax.fori_loop(0, n_chunks, hg_state, st0, unroll=seq_unroll)

    def fin_trip(t, carry):
        for u in range(grp):
            hg_out(t * grp + u, carry)
            ml_norm(t * grp + u, carry)
        return carry
    lax.fori_loop(0, n_blocks // grp, fin_trip, 0)

    if not has_init:
        for d in range(2):
            sfin_ref[0, d, 0] = jnp.transpose(st_fin[d])
            cfin_ref[0, d, 0] = c_fin[d]
            nfin_ref[0, d, 0] = n_fin[d][:, d * ML_DQK:(d + 1) * ML_DQK]
            mfin_ref[0, d, 0] = m_fin[d]


def _scan(proj, gates, lb_logits, hg_norm_g, gate_bias, ml_norm_g, init, *, bsz, t_len, hps):
    has_init = init is not None
    m = bsz * t_len
    n_chunks = t_len // CHUNK
    n_blocks = t_len // BLK
    wide = hps * HEAD_W

    def col(sec):
        return pl.BlockSpec((t_len, wide), lambda b, h, sec=sec: (b, sec // hps + h))

    in_specs = [col(SEC_HQ), col(SEC_HF_FW), col(SEC_HF_BW), col(SEC_HI), col(SEC_HGATE),
                col(SEC_MQK), col(SEC_MV), col(SEC_MO),
                pl.BlockSpec((t_len, LANES), lambda b, h: (b, 0)),
                pl.BlockSpec((2, 2, wide), lambda b, h: (0, 0, h)),
                pl.BlockSpec((1, HG_DV), lambda b, h: (0, 0)),
                pl.BlockSpec((1, LANES), lambda b, h: (0, 0)),
                pl.BlockSpec((1, wide), lambda b, h: (0, h))]
    args = [proj] * 8 + [gates, lb_logits, hg_norm_g, gate_bias, ml_norm_g]
    out_specs = [pl.BlockSpec((t_len, wide), lambda b, h: (b, h)),
                 pl.BlockSpec((t_len, wide), lambda b, h: (b, h))]
    out_shape = [jax.ShapeDtypeStruct((m, N_HEADS * HG_DV), BF16),
                 jax.ShapeDtypeStruct((m, N_HEADS * ML_DV), BF16)]
    if has_init:
        s0, c0, n0, m0 = init
        in_specs += [pl.BlockSpec((1, 1, 2, hps, HG_DK, HG_DV), lambda b, h: (b, 0, 0, h, 0, 0)),
                     pl.BlockSpec((1, 1, 2, hps, ML_DQK, ML_DV), lambda b, h: (b, 0, 0, h, 0, 0)),
                     pl.BlockSpec((1, 1, 2, hps, 1, ML_DQK), lambda b, h: (b, 0, 0, h, 0, 0)),
                     pl.BlockSpec(memory_space=pltpu.SMEM)]
        args += [s0, c0, n0.reshape(n0.shape[:4] + (1, ML_DQK)), m0.reshape(bsz, 2 * N_HEADS)]
    else:
        out_specs += [pl.BlockSpec((1, 2, hps, HG_DK, HG_DV), lambda b, h: (b, 0, h, 0, 0)),
                      pl.BlockSpec((1, 2, hps, ML_DQK, ML_DV), lambda b, h: (b, 0, h, 0, 0)),
                      pl.BlockSpec((1, 2, hps, 1, ML_DQK), lambda b, h: (b, 0, h, 0, 0)),
                      pl.BlockSpec((1, 2, hps, 1, LANES), lambda b, h: (b, 0, h, 0, 0))]
        out_shape += [jax.ShapeDtypeStruct((bsz, 2, N_HEADS, HG_DK, HG_DV), F32),
                      jax.ShapeDtypeStruct((bsz, 2, N_HEADS, ML_DQK, ML_DV), F32),
                      jax.ShapeDtypeStruct((bsz, 2, N_HEADS, 1, ML_DQK), F32),
                      jax.ShapeDtypeStruct((bsz, 2, N_HEADS, 1, LANES), F32)]

    def per_head(shape, dtype):
        return pltpu.VMEM((hps,) + shape, dtype)

    def per_chunk_row(*lead):
        return per_head(lead + (n_chunks, 1, LANES), F32)

    scratch = [per_head((2, t_len, HG_DK), BF16),
               per_head((2, t_len, HG_DK), BF16),
               per_head((2, t_len, 2 * HG_DK), BF16),
               per_head((n_blocks, HG_DV, BLK), BF16),
               per_head((2, n_chunks, HG_DV, HG_DK), F32),
               per_chunk_row(2),
               per_head((2, n_chunks, HG_DV, HG_DK), BF16),
               per_head((t_len, 4 * LANES), F32),
               per_head((2, t_len, LANES), F32),
               per_head((2, t_len, BLK), F32),
               per_head((2, n_chunks, ML_DQK, ML_DV), F32),
               per_chunk_row(),
               per_chunk_row(2),
               per_chunk_row(2),
               per_chunk_row(2),
               per_chunk_row(2),
               per_chunk_row(2),
               per_head((2, n_chunks, ML_DQK, ML_DV), BF16),
               per_chunk_row(2),
               per_head((t_len, ML_DV), F32),
               pltpu.VMEM((t_len, LANES), BF16)]
    kern = functools.partial(_scan_kernel, t_len=t_len, has_init=has_init, hps=hps)
    return pl.pallas_call(
        kern,
        grid=(bsz, N_HEADS // hps),
        in_specs=in_specs,
        out_specs=out_specs,
        out_shape=out_shape,
        scratch_shapes=scratch,
        compiler_params=pltpu.CompilerParams(dimension_semantics=("arbitrary", "arbitrary"),
                                             vmem_limit_bytes=VMEM_LIMIT),
        name="scan_init" if has_init else "scan_ctx",
    )(*args)


def _mix_out_kernel(hg_ref, ml_ref, ga_ref, gb_ref, x_ref, gt_ref, sh_ref, sc_ref, gpost_ref, gpre_ref,
                    wuh_ref, wum_ref, wo_ref, x1_ref, h2_ref):
    y_hg = _dot(hg_ref[...], wuh_ref[...])
    y_ml = _dot(ml_ref[...], wum_ref[...])
    merged = ga_ref[...].astype(F32) * y_hg + gb_ref[...].astype(F32) * y_ml
    y = _dot(merged.astype(BF16), wo_ref[...])
    x1 = x_ref[...] + gt_ref[0] * _rms_rows(y, gpost_ref[...])
    x1_ref[...] = x1
    h2 = _rms_rows(x1, gpre_ref[...]) * (1.0 + sc_ref[0]) + sh_ref[0]
    h2_ref[...] = h2.astype(BF16)


def _mix_out(hg_act, ml_act, mgate, x2d, mod3, mod_row_fn, g_post_mix, g_pre_ffn, w_up_hg, w_up_ml, w_out, *, tm):
    m = x2d.shape[0]

    def resident(shape):
        return pl.BlockSpec(shape, lambda i: (0, 0), pipeline_mode=pl.Buffered(1))

    return pl.pallas_call(
        _mix_out_kernel,
        grid=(m // tm,),
        in_specs=[pl.BlockSpec((tm, N_HEADS * HG_DV), lambda i: (i, 0)),
                  pl.BlockSpec((tm, N_HEADS * ML_DV), lambda i: (i, 0)),
                  pl.BlockSpec((tm, D_MODEL), lambda i: (i, 0)),
                  pl.BlockSpec((tm, D_MODEL), lambda i: (i, 1)),
                  pl.BlockSpec((tm, D_MODEL), lambda i: (i, 0)),
                  pl.BlockSpec((1, 1, D_MODEL), lambda i: (mod_row_fn(i), 0, 2)),
                  pl.BlockSpec((1, 1, D_MODEL), lambda i: (mod_row_fn(i), 0, 3)),
                  pl.BlockSpec((1, 1, D_MODEL), lambda i: (mod_row_fn(i), 0, 4)),
                  resident((1, D_MODEL)),
                  resident((1, D_MODEL)),
                  resident((N_HEADS * HG_DV, D_MODEL)),
                  resident((N_HEADS * ML_DV, D_MODEL)),
                  resident((D_MODEL, D_MODEL))],
        out_specs=[pl.BlockSpec((tm, D_MODEL), lambda i: (i, 0)),
                   pl.BlockSpec((tm, D_MODEL), lambda i: (i, 0))],
        out_shape=[jax.ShapeDtypeStruct((m, D_MODEL), F32),
                   jax.ShapeDtypeStruct((m, D_MODEL), BF16)],
        compiler_params=pltpu.CompilerParams(dimension_semantics=("arbitrary",),
                                             vmem_limit_bytes=VMEM_LIMIT),
        name="mix_out",
    )(hg_act, ml_act, mgate, mgate, x2d, mod3, mod3, mod3, g_post_mix, g_pre_ffn, w_up_hg, w_up_ml, w_out)


def _ffn_kernel(h2_ref, x1_ref, gt_ref, gpost_ref, wa_ref, wb_ref, wo_ref, out_ref):
    j = pl.program_id(1)

    @pl.when(j == 0)
    def _():
        out_ref[...] = jnp.zeros_like(out_ref)

    h2 = h2_ref[...]
    hid = (_silu(_dot(h2, wa_ref[...])) * _dot(h2, wb_ref[...])).astype(BF16)
    for n0 in range(0, D_MODEL, FFN_ACC_COLS):
        out_ref[:, n0:n0 + FFN_ACC_COLS] += _dot(hid, wo_ref[:, n0:n0 + FFN_ACC_COLS])

    @pl.when(j == pl.num_programs(1) - 1)
    def _():
        rows = 128

        def body(i, carry):
            r = pl.ds(pl.multiple_of(i * rows, rows), rows)
            out_ref[r, :] = x1_ref[r, :] + gt_ref[0] * _rms_rows(out_ref[r, :], gpost_ref[...])
            return carry
        lax.fori_loop(0, out_ref.shape[0] // rows, body, 0)


def _ffn(h2, x1, mod3, mod_row_fn, g_post_ffn, w_ffn_in, w_ffn_out, *, tm, tf):
    m = x1.shape[0]
    nf = D_FF // tf
    return pl.pallas_call(
        _ffn_kernel,
        grid=(m // tm, nf),
        in_specs=[pl.BlockSpec((tm, D_MODEL), lambda i, j: (i, 0)),
                  pl.BlockSpec((tm, D_MODEL), lambda i, j: (i, 0)),
                  pl.BlockSpec((1, 1, D_MODEL), lambda i, j: (mod_row_fn(i), 0, 5)),
                  pl.BlockSpec((1, D_MODEL), lambda i, j: (0, 0)),
                  pl.BlockSpec((D_MODEL, tf), lambda i, j: (0, j)),
                  pl.BlockSpec((D_MODEL, tf), lambda i, j, nf=nf: (0, j + nf)),
                  pl.BlockSpec((tf, D_MODEL), lambda i, j: (j, 0))],
        out_specs=pl.BlockSpec((tm, D_MODEL), lambda i, j: (i, 0)),
        out_shape=jax.ShapeDtypeStruct((m, D_MODEL), F32),
        compiler_params=pltpu.CompilerParams(dimension_semantics=("arbitrary", "arbitrary"),
                                             vmem_limit_bytes=VMEM_LIMIT_FFN),
        name="ffn",
    )(h2, x1, mod3, g_post_ffn, w_ffn_in, w_ffn_in, w_ffn_out)


def _w_prep_kernel(wt_ref, main_ref, gate_ref):
    x = wt_ref[...]
    pieces = [x[:W_MQ]]
    for h in range(N_HEADS):
        pieces += [x[W_MQ + h * ML_DQK:W_MQ + (h + 1) * ML_DQK], x[W_MK + h * ML_DQK:W_MK + (h + 1) * ML_DQK]]
    pieces += [x[W_MV:W_GATES], x[W_GATES + GATE_W:]]
    main_ref[...] = jnp.concatenate(pieces, axis=0).astype(BF16)
    gate_ref[...] = jnp.concatenate([x[W_GATES:W_GATES + GATE_W]] * (LANES // GATE_W), axis=0).astype(BF16)


def _prep_w_in(w_t):
    cols = 256
    return pl.pallas_call(
        _w_prep_kernel,
        grid=(D_MODEL // cols,),
        in_specs=[pl.BlockSpec((IN_W, cols), lambda i: (0, i))],
        out_specs=[pl.BlockSpec((PROJ_W, cols), lambda i: (0, i)),
                   pl.BlockSpec((LANES, cols), lambda i: (0, i))],
        out_shape=[jax.ShapeDtypeStruct((PROJ_W, D_MODEL), BF16),
                   jax.ShapeDtypeStruct((LANES, D_MODEL), BF16)],
        compiler_params=pltpu.CompilerParams(dimension_semantics=("arbitrary",),
                                             vmem_limit_bytes=VMEM_LIMIT),
        name="w_prep",
    )(w_t)


def _layer_group(x, mod3, mod_row_fn, init, p, *, tm_in, tn_in, tm_mix, tm_ffn, tf):
    bsz, t_len, _ = x.shape
    x2d = x.reshape(bsz * t_len, D_MODEL)
    proj, mgate, gates = _in_proj(x2d, mod3, lambda i: mod_row_fn(i, tm_in), p["g_pre_mix"], p["w_main"],
                                  p["w_gate"], tm=tm_in, tn=tn_in)
    hps = max(1, min(N_HEADS, SCAN_ROWS_PER_STEP // t_len))
    outs = _scan(proj, gates, p["lb_logits"], p["hg_norm_g"], p["gate_bias"], p["ml_norm_g"], init,
                 bsz=bsz, t_len=t_len, hps=hps)
    hg_act, ml_act = outs[0], outs[1]
    x1, h2 = _mix_out(hg_act, ml_act, mgate, x2d, mod3, lambda i: mod_row_fn(i, tm_mix), p["g_post_mix"],
                      p["g_pre_ffn"], p["w_up_hg"], p["w_up_ml"], p["w_out"], tm=tm_mix)
    y = _ffn(h2, x1, mod3, lambda i: mod_row_fn(i, tm_ffn), p["g_post_ffn"], p["w_ffn_in"], p["w_ffn_out"],
             tm=tm_ffn, tf=tf)
    return y.reshape(bsz, t_len, D_MODEL), outs[2:]


def kernel(x_prompt, x_sample, c, state_hgrn_s, state_mlstm_c, state_mlstm_n, state_mlstm_m, c_ctx, w_mod, b_mod,
           norm_pre_mix, norm_post_mix, norm_pre_ffn, norm_post_ffn, w_in, hgrn_lb_logits, hgrn_norm_g, mlstm_b_i,
           mlstm_b_f, mlstm_norm_g, w_up_hgrn, w_up_mlstm, w_out, w_ffn_in, w_ffn_out):
    bsz_p, t_p, _ = x_prompt.shape
    bsz_s, t_s, _ = x_sample.shape

    cond = jnp.concatenate([c_ctx[None, :], c, jnp.zeros((MOD_ROWS - 1 - bsz_s, D_MODEL), F32)], axis=0)
    mod = _modulation(cond, w_mod[0], b_mod[0][None, :])
    mod3 = mod.reshape(MOD_ROWS, 1, N_MOD)

    w_main, w_gate = _prep_w_in(jnp.transpose(w_in[0]))
    p = dict(
        w_main=w_main, w_gate=w_gate,
        g_pre_mix=norm_pre_mix[0][None, :], g_post_mix=norm_post_mix[0][None, :],
        g_pre_ffn=norm_pre_ffn[0][None, :], g_post_ffn=norm_post_ffn[0][None, :],
        lb_logits=hgrn_lb_logits, hg_norm_g=hgrn_norm_g[0][None, :],
        gate_bias=jnp.tile(jnp.concatenate([mlstm_b_i[0], mlstm_b_f[0]]), LANES // GATE_W)[None, :],
        ml_norm_g=mlstm_norm_g[0][None, :],
        w_up_hg=w_up_hgrn[0].astype(BF16), w_up_ml=w_up_mlstm[0].astype(BF16), w_out=w_out[0].astype(BF16),
        w_ffn_in=w_ffn_in[0].astype(BF16), w_ffn_out=w_ffn_out[0].astype(BF16),
    )
    tiles_p = dict(tm_in=min(1024, bsz_p * t_p), tn_in=1024, tm_mix=min(512, bsz_p * t_p),
                   tm_ffn=min(1024, bsz_p * t_p), tf=512)
    tiles_s = dict(tm_in=min(1024, t_s), tn_in=1024, tm_mix=min(512, t_s), tm_ffn=min(1024, t_s), tf=512)
    y_p, ctx_states = _layer_group(x_prompt, mod3, lambda i, tm: 0, None, p, **tiles_p)
    init = (state_hgrn_s, state_mlstm_c, state_mlstm_n, state_mlstm_m[:, 0])
    y_s, _ = _layer_group(x_sample, mod3, lambda i, tm: 1 + (i * tm) // t_s, init, p, **tiles_s)

    s_fin, c_fin, n_fin, m_fin = ctx_states
    new_hgrn_s = s_fin[:, None]
    new_mlstm_c = c_fin[:, None]
    new_mlstm_n = n_fin[:, None, :, :, 0, :]
    new_mlstm_m = m_fin[:, None, :, :, 0, 0]
    return (y_p, y_s, new_hgrn_s, new_mlstm_c, new_mlstm_n, new_mlstm_m)
```

```python
import functools

import jax
import jax.numpy as jnp
from jax import lax
from jax.experimental import pallas as pl
from jax.experimental.pallas import tpu as pltpu

F32 = jnp.float32
BF16 = jnp.bfloat16

D_MODEL = 2048
N_HEADS = 8
HG_DK = 128
HG_DV = 128
ML_DQK = 64
ML_DV = 128
HEAD_W = 128
D_FF = 5632
CHUNK = 64
EPS = 1e-6
N_MOD = 6 * D_MODEL
LANES = 128
MOD_ROWS = 16
BLK = 2 * CHUNK
SCAN_ROWS_PER_STEP = 2048
BLOCKS_PER_TRIP = 8
MGATE_COLS = 256
FFN_ACC_COLS = 512
GATE_W = 4 * N_HEADS

SEC_HQ, SEC_HF_FW, SEC_HF_BW, SEC_HI, SEC_HGATE = 0, 8, 16, 24, 32
SEC_MQK, SEC_MV, SEC_MO = 40, 48, 56
SEC_GA, SEC_GB = 64, 80
PROJ_W = 96 * LANES
PLAIN_W = SEC_GA * LANES

W_MQ = 5 * N_HEADS * HG_DK
W_MK = W_MQ + N_HEADS * ML_DQK
W_MV = W_MK + N_HEADS * ML_DQK
W_GATES = W_MV + N_HEADS * ML_DV
IN_W = W_GATES + 4 * N_HEADS + N_HEADS * ML_DV + 2 * D_MODEL

VMEM_LIMIT = 56 * 1024 * 1024
VMEM_LIMIT_FFN = 60 * 1024 * 1024


def _dot(a, b):
    return jnp.dot(a, b, preferred_element_type=F32)


def _dot_nt(a, b):
    return lax.dot_general(a, b, (((1,), (1,)), ((), ())), preferred_element_type=F32)


def _dot_tn(a, b):
    return lax.dot_general(a, b, (((0,), (0,)), ((), ())), preferred_element_type=F32)


def _sigmoid_pair(z):
    t = 0.5 * jnp.tanh(0.5 * z)
    return 0.5 + t, 0.5 - t


def _sigmoid(z):
    return 0.5 * jnp.tanh(0.5 * z) + 0.5


def _silu(z):
    return z * _sigmoid(z)


def _log_sigmoid(z):
    return jnp.minimum(z, 0.0) - jnp.log1p(jnp.exp(-jnp.abs(z)))


def _mod_kernel(c_ref, w_ref, b_ref, o_ref):
    a = _silu(c_ref[...]).astype(BF16)
    o_ref[...] = _dot(a, w_ref[...].astype(BF16)) + b_ref[...]


def _modulation(cond, w_mod, b_mod):
    tn = 1024
    return pl.pallas_call(
        _mod_kernel,
        grid=(N_MOD // tn,),
        in_specs=[pl.BlockSpec((MOD_ROWS, D_MODEL), lambda j: (0, 0)),
                  pl.BlockSpec((D_MODEL, tn), lambda j: (0, j)),
                  pl.BlockSpec((1, tn), lambda j: (0, j))],
        out_specs=pl.BlockSpec((MOD_ROWS, tn), lambda j: (0, j)),
        out_shape=jax.ShapeDtypeStruct((MOD_ROWS, N_MOD), F32),
        compiler_params=pltpu.CompilerParams(dimension_semantics=("arbitrary",),
                                             vmem_limit_bytes=VMEM_LIMIT),
        name="modulation",
    )(cond, w_mod, b_mod)


def _rms_rows(x, g):
    return x * lax.rsqrt(jnp.mean(x * x, axis=-1, keepdims=True) + EPS) * g


def _inproj_kernel(x_ref, sh_ref, sc_ref, g_ref, w_ref, wg_ref, proj_ref, mgate_ref, gates_ref, xn_ref, *,
                   tm, n_plain):
    rows = 128
    j = pl.program_id(1)

    @pl.when(j == 0)
    def _():
        for r0 in range(0, tm, rows):
            x = x_ref[r0:r0 + rows, :]
            h = _rms_rows(x, g_ref[...]) * (1.0 + sc_ref[0]) + sh_ref[0]
            hb = h.astype(BF16)
            xn_ref[r0:r0 + rows, :] = hb
            gates_ref[r0:r0 + rows, :] = _dot_nt(hb, wg_ref[...])

    @pl.when(j < n_plain)
    def _():
        proj_ref[...] = _dot_nt(xn_ref[...], w_ref[...])

    @pl.when(j >= n_plain)
    def _():
        for c0 in range(0, w_ref.shape[0], MGATE_COLS):
            z = _dot_nt(xn_ref[...], w_ref[c0:c0 + MGATE_COLS, :])
            mgate_ref[:, c0:c0 + MGATE_COLS] = _sigmoid(z).astype(BF16)


def _in_proj(x2d, mod3, mod_row_fn, g_pre, w_main_t, w_gate_t, *, tm, tn):
    m = x2d.shape[0]
    n_plain = PLAIN_W // tn
    kern = functools.partial(_inproj_kernel, tm=tm, n_plain=n_plain)
    return pl.pallas_call(
        kern,
        grid=(m // tm, PROJ_W // tn),
        in_specs=[pl.BlockSpec((tm, D_MODEL), lambda i, j: (i, 0)),
                  pl.BlockSpec((1, 1, D_MODEL), lambda i, j: (mod_row_fn(i), 0, 0)),
                  pl.BlockSpec((1, 1, D_MODEL), lambda i, j: (mod_row_fn(i), 0, 1)),
                  pl.BlockSpec((1, D_MODEL), lambda i, j: (0, 0)),
                  pl.BlockSpec((tn, D_MODEL), lambda i, j: (j, 0)),
                  pl.BlockSpec((LANES, D_MODEL), lambda i, j: (0, 0))],
        out_specs=[pl.BlockSpec((tm, tn), lambda i, j: (i, jnp.minimum(j, n_plain - 1))),
                   pl.BlockSpec((tm, tn), lambda i, j: (i, jnp.maximum(j - n_plain, 0))),
                   pl.BlockSpec((tm, LANES), lambda i, j: (i, 0))],
        out_shape=[jax.ShapeDtypeStruct((m, PLAIN_W), F32),
                   jax.ShapeDtypeStruct((m, PROJ_W - PLAIN_W), BF16),
                   jax.ShapeDtypeStruct((m, LANES), F32)],
        scratch_shapes=[pltpu.VMEM((tm, D_MODEL), BF16)],
        compiler_params=pltpu.CompilerParams(dimension_semantics=("arbitrary", "arbitrary"),
                                             vmem_limit_bytes=VMEM_LIMIT),
        name="in_proj",
    )(x2d, mod3, mod3, g_pre, w_main_t, w_gate_t)


def _chunk_cumsum(x, pos, rev):
    n = x.shape[0]
    s = 1
    while s < CHUNK:
        if rev:
            x = x + jnp.where(pos < CHUNK - s, pltpu.roll(x, n - s, axis=0), 0.0)
        else:
            x = x + jnp.where(pos >= s, pltpu.roll(x, s, axis=0), 0.0)
        s *= 2
    return x


def _per_chunk(x):
    return x.reshape(BLK // CHUNK, CHUNK, x.shape[-1])


N_SCAN_IN = 13


def _scan_kernel(*refs, t_len, has_init, hps):
    n_io = N_SCAN_IN + (4 if has_init else 0) + 2 + (0 if has_init else 4)
    lead = pl.program_id(1) * hps
    for hh in range(hps):
        hcol = slice(hh * HEAD_W, (hh + 1) * HEAD_W)
        head = slice(hh, hh + 1)
        views = [r.at[:, hcol] for r in refs[:8]]
        views += [refs[8], refs[9].at[:, :, hcol], refs[10], refs[11], refs[12].at[:, hcol]]
        pos = N_SCAN_IN
        if has_init:
            views += [r.at[:, :, :, head] for r in refs[pos:pos + 3]] + [refs[pos + 3]]
            pos += 4
        views += [r.at[:, hcol] for r in refs[pos:pos + 2]]
        pos += 2
        if not has_init:
            views += [r.at[:, :, head] for r in refs[pos:pos + 4]]
        views += [r.at[hh] for r in refs[n_io:-1]] + [refs[-1]]
        _scan_head(views, lead + hh, pack_gates=hh == 0, t_len=t_len, has_init=has_init)


def _scan_head(refs, h_idx, *, pack_gates, t_len, has_init):
    (hq_ref, hff_ref, hfb_ref, hi_ref, hgate_ref, mqk_ref, mv_ref, mo_ref, gates_ref,
     lbl_ref, hgn_ref, gbias_ref, mln_ref) = refs[:13]
    pos = 13
    if has_init:
        s0_ref, c0_ref, n0_ref, m0_ref = refs[pos:pos + 4]
        pos += 4
    hg_out_ref, ml_out_ref = refs[pos:pos + 2]
    pos += 2
    if not has_init:
        sfin_ref, cfin_ref, nfin_ref, mfin_ref = refs[pos:pos + 4]
        pos += 4
    (qa_ref, ka_ref, qx_ref, vt_ref, ds_ref, dec_ref, stt_ref,
     sel_ref, g_ref, ut_ref, cloc_ref, nloc_ref, a_ref, mloc_ref, mst_ref, sold_ref, sloc_ref,
     cst_ref, nst_ref, hsum_ref, pk_ref) = refs[pos:]

    b_idx = pl.program_id(0)
    n_chunks = t_len // CHUNK
    n_blocks = t_len // BLK
    cpb = BLK // CHUNK
    mid = CHUNK // 2

    row = lax.broadcasted_iota(jnp.int32, (BLK, BLK), 0)
    col = lax.broadcasted_iota(jnp.int32, (BLK, BLK), 1)
    same_chunk = jnp.right_shift(row, 6) == jnp.right_shift(col, 6)
    masks = (same_chunk & (col <= row), same_chunk & (col >= row))
    in_pos = jnp.bitwise_and(row, CHUNK - 1)
    row_lo = row < CHUNK
    lane_lo = col < ML_DQK
    lane = lax.broadcasted_iota(jnp.int32, (1, LANES), 1)

    def first_chunk_only(x):
        return jnp.concatenate([x[:CHUNK], jnp.zeros_like(x[CHUNK:])], axis=0).astype(BF16)

    def second_chunk_only(x):
        return jnp.concatenate([jnp.zeros_like(x[:CHUNK]), x[CHUNK:]], axis=0).astype(BF16)

    logits = lbl_ref[...]
    l0, l1 = logits[:, 0, :], logits[:, 1, :]
    mx = jnp.maximum(l0, l1)
    e0, e1 = jnp.exp(l0 - mx), jnp.exp(l1 - mx)
    lb_all = e0 / (e0 + e1)

    is_f = jnp.bitwise_and(lane, GATE_W - 1) >= 2 * N_HEADS
    piece = jnp.right_shift(lane, 5)
    grp = min(BLOCKS_PER_TRIP, n_blocks)
    seq_unroll = n_chunks <= BLOCKS_PER_TRIP
    sel_rows = grp * BLK

    if pack_gates:
        @pl.when(h_idx == 0)
        def _():
            def ml_pack(i, carry):
                rows = pl.ds(pl.multiple_of(i * sel_rows, sel_rows), sel_rows)
                xg = gates_ref[rows, :] + gbias_ref[...]
                xg = jnp.where(is_f, _log_sigmoid(xg), xg)
                hi = xg.astype(BF16).astype(F32)
                rest = xg - hi
                mid = rest.astype(BF16).astype(F32)
                packed = jnp.where(piece == 0, hi, jnp.where(piece == 1, mid, rest - mid))
                pk_ref[rows, :] = packed.astype(BF16)
                return carry
            lax.fori_loop(0, t_len // sel_rows, ml_pack, 0)


    def hg_prep(i, carry):
        rows = pl.ds(pl.multiple_of(i * BLK, BLK), BLK)
        q = _silu(hq_ref[rows, :])
        vt = jnp.transpose(hi_ref[rows, :]).astype(BF16)
        vt_ref[i] = vt
        kd_cols = []
        for d, zf_ref in enumerate((hff_ref, hfb_ref)):
            rev = d == 1
            lb = lb_all[d:d + 1]
            oml = 1.0 - lb
            sp, sn = _sigmoid_pair(zf_ref[rows, :])
            k = oml * sn
            b = _chunk_cumsum(jnp.log2(lb + oml * sp), in_pos, rev)
            b3, q3, k3 = _per_chunk(b), _per_chunk(q), _per_chunk(k)
            i_mid, i_last = (CHUNK - 1 - mid, 0) if rev else (mid, CHUNK - 1)
            b_mid, b_last = b3[:, i_mid:i_mid + 1], b3[:, i_last:i_last + 1]
            qa_ref[d, rows, :] = (q3 * jnp.exp2(b3 - b_mid)).reshape(BLK, HG_DK).astype(BF16)
            ka_ref[d, rows, :] = (k3 * jnp.exp2(b_mid - b3)).reshape(BLK, HG_DK).astype(BF16)
            qi = q * jnp.exp2(b)
            qx_ref[d, rows, :HG_DK] = first_chunk_only(qi)
            qx_ref[d, rows, HG_DK:] = second_chunk_only(qi)
            kd = (k3 * jnp.exp2(b_last - b3)).reshape(BLK, HG_DK)
            kd_cols += [first_chunk_only(kd), second_chunk_only(kd)]
            dec = jnp.exp2(b_last)
            for c in range(cpb):
                dec_ref[d, i * cpb + c] = dec[c]
        ds_t = _dot(vt, jnp.concatenate(kd_cols, axis=1))
        for d in range(2):
            for c in range(cpb):
                j = d * cpb + c
                ds_ref[d, i * cpb + c] = ds_t[:, j * HG_DK:(j + 1) * HG_DK]
        return carry

    if has_init:
        st0 = tuple(jnp.transpose(s0_ref[0, 0, d, 0]) for d in range(2))
    else:
        st0 = (jnp.zeros((HG_DV, HG_DK), F32),) * 2

    def hg_state(n, carry):
        new = []
        for d in range(2):
            cn = (n_chunks - 1 - n) if d == 1 else n
            stt_ref[d, cn] = carry[d].astype(BF16)
            new.append(dec_ref[d, cn] * carry[d] + ds_ref[d, cn])
        return tuple(new)

    def hg_out(i, carry):
        rows = pl.ds(pl.multiple_of(i * BLK, BLK), BLK)
        vt = vt_ref[i]
        lhs, rhs_t = [], []
        for d in range(2):
            att = jnp.where(masks[d], _dot_nt(qa_ref[d, rows, :], ka_ref[d, rows, :]), 0.0)
            lhs += [att.astype(BF16), qx_ref[d, rows, :]]
            rhs_t += [vt] + [stt_ref[d, i * cpb + c] for c in range(cpb)]
        o = _dot_nt(jnp.concatenate(lhs, axis=1), jnp.concatenate(rhs_t, axis=1))
        o = (o * lax.rsqrt(jnp.mean(o * o, axis=-1, keepdims=True) + EPS) * hgn_ref[...]
             * _silu(hgate_ref[rows, :]))
        hg_out_ref[rows, :] = o.astype(BF16)
        return carry


    r_i = lax.broadcasted_iota(jnp.int32, (LANES, 4 * LANES), 0)
    c_i = lax.broadcasted_iota(jnp.int32, (LANES, 4 * LANES), 1)
    cblk = jnp.right_shift(c_i, 7)
    target = h_idx + jnp.where(cblk == 0, 0, jnp.where(cblk == 1, 2 * N_HEADS,
                                                       jnp.where(cblk == 2, N_HEADS, 3 * N_HEADS)))
    onehot = jnp.where((jnp.bitwise_and(r_i, GATE_W - 1) == target) & (r_i < 3 * GATE_W), 1.0, 0.0).astype(BF16)

    def ml_select(i, carry):
        rows = pl.ds(pl.multiple_of(i * sel_rows, sel_rows), sel_rows)
        sel_ref[rows, :] = _dot(pk_ref[rows, :], onehot)
        return carry

    def ml_prep(i, carry):
        rows = pl.ds(pl.multiple_of(i * BLK, BLK), BLK)
        qk = mqk_ref[rows, :]
        kk = jnp.where(lane_lo, pltpu.roll(qk, ML_DQK, axis=1), qk) * (ML_DQK ** -0.5)
        v = mv_ref[rows, :]
        e_dir = []
        for d in range(2):
            rev = d == 1
            li = sel_ref[rows, (2 * d) * LANES:(2 * d + 1) * LANES]
            lf = sel_ref[rows, (2 * d + 1) * LANES:(2 * d + 2) * LANES]
            g = _chunk_cumsum(lf, in_pos, rev)
            u = li - g
            g_ref[d, rows, :] = g
            ut_ref[d, rows, :] = jnp.transpose(u)
            i_last = 0 if rev else CHUNK - 1
            g_last = _per_chunk(g)[:, i_last:i_last + 1]
            w_end = g_last + _per_chunk(u)
            m_loc = jnp.max(w_end, axis=1, keepdims=True)
            e_dir.append(jnp.exp(w_end - m_loc).reshape(BLK, LANES))
            for c in range(cpb):
                a_ref[d, i * cpb + c] = g_last[c]
                mloc_ref[d, i * cpb + c] = m_loc[c]
        wk = jnp.where(lane_lo, e_dir[0], e_dir[1]) * kk
        v_cols = jnp.concatenate([first_chunk_only(v), second_chunk_only(v)], axis=1)
        c_loc = _dot_tn(wk.astype(BF16), v_cols)
        n_loc = jnp.sum(_per_chunk(wk), axis=1, keepdims=True)
        for c in range(cpb):
            nloc_ref[i * cpb + c] = n_loc[c]
            for d in range(2):
                cloc_ref[d, i * cpb + c] = c_loc[d * ML_DQK:(d + 1) * ML_DQK, c * ML_DV:(c + 1) * ML_DV]
        return carry

    def sel_trip(t, carry):
        ml_select(t, carry)
        for u in range(grp):
            ml_prep(t * grp + u, carry)
        return carry
    lax.fori_loop(0, n_blocks // grp, sel_trip, 0)

    if has_init:
        m0 = tuple(jnp.full((1, LANES), m0_ref[b_idx, d * N_HEADS + h_idx], F32) for d in range(2))
    else:
        m0 = (jnp.zeros((1, LANES), F32),) * 2

    def ml_stab(n, carry):
        new = []
        for d in range(2):
            cn = (n_chunks - 1 - n) if d == 1 else n
            m_st = carry[d]
            a, m_loc = a_ref[d, cn], mloc_ref[d, cn]
            m_new = jnp.maximum(a + m_st, m_loc)
            mst_ref[d, cn] = m_st
            sold_ref[d, cn] = jnp.exp(a + m_st - m_new)
            sloc_ref[d, cn] = jnp.exp(m_loc - m_new)
            new.append(m_new)
        return tuple(new)
    m_fin = lax.fori_loop(0, n_chunks, ml_stab, m0, unroll=seq_unroll)

    if has_init:
        c0 = tuple(c0_ref[0, 0, d, 0] for d in range(2))
        n0 = tuple(jnp.concatenate([n0_ref[0, 0, d, 0]] * 2, axis=1) for d in range(2))
    else:
        c0 = (jnp.zeros((ML_DQK, ML_DV), F32),) * 2
        n0 = (jnp.zeros((1, LANES), F32),) * 2

    def ml_state(n, carry):
        cs, ns = carry
        new_c, new_n = [], []
        for d in range(2):
            cn = (n_chunks - 1 - n) if d == 1 else n
            cst_ref[d, cn] = cs[d].astype(BF16)
            nst_ref[d, cn] = ns[d]
            s_old, s_loc = sold_ref[d, cn], sloc_ref[d, cn]
            new_c.append(s_old * cs[d] + s_loc * cloc_ref[d, cn])
            new_n.append(s_old * ns[d] + s_loc * nloc_ref[cn])
        return tuple(new_c), tuple(new_n)
    c_fin, n_fin = lax.fori_loop(0, n_chunks, ml_state, (c0, n0), unroll=seq_unroll)

    def ml_out(i, carry):
        rows = pl.ds(pl.multiple_of(i * BLK, BLK), BLK)
        chunks = pl.ds(i * cpb, cpb)
        qk = mqk_ref[rows, :]
        qb = qk[:, :ML_DQK].astype(BF16)
        kb = (qk[:, ML_DQK:] * (ML_DQK ** -0.5)).astype(BF16)
        vb = mv_ref[rows, :].astype(BF16)
        qk_raw = _dot_nt(qb, kb)
        qq = jnp.where(lane_lo, qk, pltpu.roll(qk, ML_DQK, axis=1))
        q_cols = jnp.where(lane_lo == row_lo, qq, 0.0).astype(BF16)
        c_rows = jnp.concatenate([cst_ref[d, chunks].reshape(BLK, ML_DV) for d in range(2)], axis=1)
        qc = _dot(q_cols, c_rows)
        parts = []
        for d in range(2):
            g = g_ref[d, rows, :]
            log_inter = (_per_chunk(g) + mst_ref[d, chunks]).reshape(BLK, LANES)
            log_d = jnp.where(masks[d], g + ut_ref[d, rows, :], -jnp.inf)
            m_t = jnp.maximum(log_inter, jnp.max(log_d, axis=-1, keepdims=True))
            s_inter = jnp.exp(log_inter - m_t)
            s_qk = qk_raw * jnp.exp(log_d - m_t)
            own_half = lane_lo if d == 0 else jnp.logical_not(lane_lo)
            q_n = jnp.where(own_half, (_per_chunk(qq) * nst_ref[d, chunks]).reshape(BLK, LANES), 0.0)
            den = jnp.sum(s_qk + s_inter * q_n, axis=-1, keepdims=True)
            parts.append((s_qk, s_inter, jnp.maximum(jnp.abs(den), jnp.exp(-m_t[:, 0:1]))))
        nv = _dot(jnp.concatenate([parts[0][0], parts[1][0]], axis=0).astype(BF16), vb)
        o = None
        for d in range(2):
            _, s_inter, den = parts[d]
            h = (nv[d * BLK:(d + 1) * BLK] + s_inter * qc[:, d * ML_DV:(d + 1) * ML_DV]) / den
            o = h if o is None else o + h
        hsum_ref[rows, :] = o
        return carry

    def ml_norm(i, carry):
        rows = pl.ds(pl.multiple_of(i * BLK, BLK), BLK)
        o = hsum_ref[rows, :]
        mu = jnp.mean(o, axis=-1, keepdims=True)
        oc = o - mu
        var = jnp.mean(oc * oc, axis=-1, keepdims=True)
        o = oc * lax.rsqrt(var + EPS) * mln_ref[...] * _sigmoid(mo_ref[rows, :])
        ml_out_ref[rows, :] = o.astype(BF16)
        return carry
    pair = min(2, n_blocks)

    def out_trip(t, carry):
        for u in range(pair):
            ml_out(t * pair + u, carry)
            hg_prep(t * pair + u, carry)
        return carry
    lax.fori_loop(0, n_blocks // pair, out_trip, 0)

    st_fin = lax.fori_loop(0, n_chunks, hg_state, st0, unroll=seq_unroll)

    def fin_trip(t, carry):
        for u in range(grp):
            hg_out(t * grp + u, carry)
            ml_norm(t * grp + u, carry)
        return carry
    lax.fori_loop(0, n_blocks // grp, fin_trip, 0)

    if not has_init:
        for d in range(2):
            sfin_ref[0, d, 0] = jnp.transpose(st_fin[d])
            cfin_ref[0, d, 0] = c_fin[d]
            nfin_ref[0, d, 0] = n_fin[d][:, d * ML_DQK:(d + 1) * ML_DQK]
            mfin_ref[0, d, 0] = m_fin[d]


def _scan(proj, gates, lb_logits, hg_norm_g, gate_bias, ml_norm_g, init, *, bsz, t_len, hps):
    has_init = init is not None
    m = bsz * t_len
    n_chunks = t_len // CHUNK
    n_blocks = t_len // BLK
    wide = hps * HEAD_W

    def col(sec):
        return pl.BlockSpec((t_len, wide), lambda b, h, sec=sec: (b, sec // hps + h))

    in_specs = [col(SEC_HQ), col(SEC_HF_FW), col(SEC_HF_BW), col(SEC_HI), col(SEC_HGATE),
                col(SEC_MQK), col(SEC_MV), col(SEC_MO),
                pl.BlockSpec((t_len, LANES), lambda b, h: (b, 0)),
                pl.BlockSpec((2, 2, wide), lambda b, h: (0, 0, h)),
                pl.BlockSpec((1, HG_DV), lambda b, h: (0, 0)),
                pl.BlockSpec((1, LANES), lambda b, h: (0, 0)),
                pl.BlockSpec((1, wide), lambda b, h: (0, h))]
    args = [proj] * 8 + [gates, lb_logits, hg_norm_g, gate_bias, ml_norm_g]
    out_specs = [pl.BlockSpec((t_len, wide), lambda b, h: (b, h)),
                 pl.BlockSpec((t_len, wide), lambda b, h: (b, h))]
    out_shape = [jax.ShapeDtypeStruct((m, N_HEADS * HG_DV), BF16),
                 jax.ShapeDtypeStruct((m, N_HEADS * ML_DV), BF16)]
    if has_init:
        s0, c0, n0, m0 = init
        in_specs += [pl.BlockSpec((1, 1, 2, hps, HG_DK, HG_DV), lambda b, h: (b, 0, 0, h, 0, 0)),
                     pl.BlockSpec((1, 1, 2, hps, ML_DQK, ML_DV), lambda b, h: (b, 0, 0, h, 0, 0)),
                     pl.BlockSpec((1, 1, 2, hps, 1, ML_DQK), lambda b, h: (b, 0, 0, h, 0, 0)),
                     pl.BlockSpec(memory_space=pltpu.SMEM)]
        args += [s0, c0, n0.reshape(n0.shape[:4] + (1, ML_DQK)), m0.reshape(bsz, 2 * N_HEADS)]
    else:
        out_specs += [pl.BlockSpec((1, 2, hps, HG_DK, HG_DV), lambda b, h: (b, 0, h, 0, 0)),
                      pl.BlockSpec((1, 2, hps, ML_DQK, ML_DV), lambda b, h: (b, 0, h, 0, 0)),
                      pl.BlockSpec((1, 2, hps, 1, ML_DQK), lambda b, h: (b, 0, h, 0, 0)),
                      pl.BlockSpec((1, 2, hps, 1, LANES), lambda b, h: (b, 0, h, 0, 0))]
        out_shape += [jax.ShapeDtypeStruct((bsz, 2, N_HEADS, HG_DK, HG_DV), F32),
                      jax.ShapeDtypeStruct((bsz, 2, N_HEADS, ML_DQK, ML_DV), F32),
                      jax.ShapeDtypeStruct((bsz, 2, N_HEADS, 1, ML_DQK), F32),
                      jax.ShapeDtypeStruct((bsz, 2, N_HEADS, 1, LANES), F32)]

    def per_head(shape, dtype):
        return pltpu.VMEM((hps,) + shape, dtype)

    def per_chunk_row(*lead):
        return per_head(lead + (n_chunks, 1, LANES), F32)

    scratch = [per_head((2, t_len, HG_DK), BF16),
               per_head((2, t_len, HG_DK), BF16),
               per_head((2, t_len, 2 * HG_DK), BF16),
               per_head((n_blocks, HG_DV, BLK), BF16),
               per_head((2, n_chunks, HG_DV, HG_DK), F32),
               per_chunk_row(2),
               per_head((2, n_chunks, HG_DV, HG_DK), BF16),
               per_head((t_len, 4 * LANES), F32),
               per_head((2, t_len, LANES), F32),
               per_head((2, t_len, BLK), F32),
               per_head((2, n_chunks, ML_DQK, ML_DV), F32),
               per_chunk_row(),
               per_chunk_row(2),
               per_chunk_row(2),
               per_chunk_row(2),
               per_chunk_row(2),
               per_chunk_row(2),
               per_head((2, n_chunks, ML_DQK, ML_DV), BF16),
               per_chunk_row(2),
               per_head((t_len, ML_DV), F32),
               pltpu.VMEM((t_len, LANES), BF16)]
    kern = functools.partial(_scan_kernel, t_len=t_len, has_init=has_init, hps=hps)
    return pl.pallas_call(
        kern,
        grid=(bsz, N_HEADS // hps),
        in_specs=in_specs,
        out_specs=out_specs,
        out_shape=out_shape,
        scratch_shapes=scratch,
        compiler_params=pltpu.CompilerParams(dimension_semantics=("arbitrary", "arbitrary"),
                                             vmem_limit_bytes=VMEM_LIMIT),
        name="scan_init" if has_init else "scan_ctx",
    )(*args)


def _mix_out_kernel(hg_ref, ml_ref, ga_ref, gb_ref, x_ref, gt_ref, sh_ref, sc_ref, gpost_ref, gpre_ref,
                    wuh_ref, wum_ref, wo_ref, x1_ref, h2_ref):
    y_hg = _dot(hg_ref[...], wuh_ref[...])
    y_ml = _dot(ml_ref[...], wum_ref[...])
    merged = ga_ref[...].astype(F32) * y_hg + gb_ref[...].astype(F32) * y_ml
    y = _dot(merged.astype(BF16), wo_ref[...])
    x1 = x_ref[...] + gt_ref[0] * _rms_rows(y, gpost_ref[...])
    x1_ref[...] = x1
    h2 = _rms_rows(x1, gpre_ref[...]) * (1.0 + sc_ref[0]) + sh_ref[0]
    h2_ref[...] = h2.astype(BF16)


def _mix_out(hg_act, ml_act, mgate, x2d, mod3, mod_row_fn, g_post_mix, g_pre_ffn, w_up_hg, w_up_ml, w_out, *, tm):
    m = x2d.shape[0]

    def resident(shape):
        return pl.BlockSpec(shape, lambda i: (0, 0), pipeline_mode=pl.Buffered(1))

    return pl.pallas_call(
        _mix_out_kernel,
        grid=(m // tm,),
        in_specs=[pl.BlockSpec((tm, N_HEADS * HG_DV), lambda i: (i, 0)),
                  pl.BlockSpec((tm, N_HEADS * ML_DV), lambda i: (i, 0)),
                  pl.BlockSpec((tm, D_MODEL), lambda i: (i, 0)),
                  pl.BlockSpec((tm, D_MODEL), lambda i: (i, 1)),
                  pl.BlockSpec((tm, D_MODEL), lambda i: (i, 0)),
                  pl.BlockSpec((1, 1, D_MODEL), lambda i: (mod_row_fn(i), 0, 2)),
                  pl.BlockSpec((1, 1, D_MODEL), lambda i: (mod_row_fn(i), 0, 3)),
                  pl.BlockSpec((1, 1, D_MODEL), lambda i: (mod_row_fn(i), 0, 4)),
                  resident((1, D_MODEL)),
                  resident((1, D_MODEL)),
                  resident((N_HEADS * HG_DV, D_MODEL)),
                  resident((N_HEADS * ML_DV, D_MODEL)),
                  resident((D_MODEL, D_MODEL))],
        out_specs=[pl.BlockSpec((tm, D_MODEL), lambda i: (i, 0)),
                   pl.BlockSpec((tm, D_MODEL), lambda i: (i, 0))],
        out_shape=[jax.ShapeDtypeStruct((m, D_MODEL), F32),
                   jax.ShapeDtypeStruct((m, D_MODEL), BF16)],
        compiler_params=pltpu.CompilerParams(dimension_semantics=("arbitrary",),
                                             vmem_limit_bytes=VMEM_LIMIT),
        name="mix_out",
    )(hg_act, ml_act, mgate, mgate, x2d, mod3, mod3, mod3, g_post_mix, g_pre_ffn, w_up_hg, w_up_ml, w_out)


def _ffn_kernel(h2_ref, x1_ref, gt_ref, gpost_ref, wa_ref, wb_ref, wo_ref, out_ref):
    j = pl.program_id(1)

    @pl.when(j == 0)
    def _():
        out_ref[...] = jnp.zeros_like(out_ref)

    h2 = h2_ref[...]
    hid = (_silu(_dot(h2, wa_ref[...])) * _dot(h2, wb_ref[...])).astype(BF16)
    for n0 in range(0, D_MODEL, FFN_ACC_COLS):
        out_ref[:, n0:n0 + FFN_ACC_COLS] += _dot(hid, wo_ref[:, n0:n0 + FFN_ACC_COLS])

    @pl.when(j == pl.num_programs(1) - 1)
    def _():
        rows = 128

        def body(i, carry):
            r = pl.ds(pl.multiple_of(i * rows, rows), rows)
            out_ref[r, :] = x1_ref[r, :] + gt_ref[0] * _rms_rows(out_ref[r, :], gpost_ref[...])
            return carry
        lax.fori_loop(0, out_ref.shape[0] // rows, body, 0)


def _ffn(h2, x1, mod3, mod_row_fn, g_post_ffn, w_ffn_in, w_ffn_out, *, tm, tf):
    m = x1.shape[0]
    nf = D_FF // tf
    return pl.pallas_call(
        _ffn_kernel,
        grid=(m // tm, nf),
        in_specs=[pl.BlockSpec((tm, D_MODEL), lambda i, j: (i, 0)),
                  pl.BlockSpec((tm, D_MODEL), lambda i, j: (i, 0)),
                  pl.BlockSpec((1, 1, D_MODEL), lambda i, j: (mod_row_fn(i), 0, 5)),
                  pl.BlockSpec((1, D_MODEL), lambda i, j: (0, 0)),
                  pl.BlockSpec((D_MODEL, tf), lambda i, j: (0, j)),
                  pl.BlockSpec((D_MODEL, tf), lambda i, j, nf=nf: (0, j + nf)),
                  pl.BlockSpec((tf, D_MODEL), lambda i, j: (j, 0))],
        out_specs=pl.BlockSpec((tm, D_MODEL), lambda i, j: (i, 0)),
        out_shape=jax.ShapeDtypeStruct((m, D_MODEL), F32),
        compiler_params=pltpu.CompilerParams(dimension_semantics=("arbitrary", "arbitrary"),
                                             vmem_limit_bytes=VMEM_LIMIT_FFN),
        name="ffn",
    )(h2, x1, mod3, g_post_ffn, w_ffn_in, w_ffn_in, w_ffn_out)


def _w_prep_kernel(wt_ref, main_ref, gate_ref):
    x = wt_ref[...]
    pieces = [x[:W_MQ]]
    for h in range(N_HEADS):
        pieces += [x[W_MQ + h * ML_DQK:W_MQ + (h + 1) * ML_DQK], x[W_MK + h * ML_DQK:W_MK + (h + 1) * ML_DQK]]
    pieces += [x[W_MV:W_GATES], x[W_GATES + GATE_W:]]
    main_ref[...] = jnp.concatenate(pieces, axis=0).astype(BF16)
    gate_ref[...] = jnp.concatenate([x[W_GATES:W_GATES + GATE_W]] * (LANES // GATE_W), axis=0).astype(BF16)


def _prep_w_in(w_t):
    cols = 256
    return pl.pallas_call(
        _w_prep_kernel,
        grid=(D_MODEL // cols,),
        in_specs=[pl.BlockSpec((IN_W, cols), lambda i: (0, i))],
        out_specs=[pl.BlockSpec((PROJ_W, cols), lambda i: (0, i)),
                   pl.BlockSpec((LANES, cols), lambda i: (0, i))],
        out_shape=[jax.ShapeDtypeStruct((PROJ_W, D_MODEL), BF16),
                   jax.ShapeDtypeStruct((LANES, D_MODEL), BF16)],
        compiler_params=pltpu.CompilerParams(dimension_semantics=("arbitrary",),
                                             vmem_limit_bytes=VMEM_LIMIT),
        name="w_prep",
    )(w_t)


def _layer_group(x, mod3, mod_row_fn, init, p, *, tm_in, tn_in, tm_mix, tm_ffn, tf):
    bsz, t_len, _ = x.shape
    x2d = x.reshape(bsz * t_len, D_MODEL)
    proj, mgate, gates = _in_proj(x2d, mod3, lambda i: mod_row_fn(i, tm_in), p["g_pre_mix"], p["w_main"],
                                  p["w_gate"], tm=tm_in, tn=tn_in)
    hps = max(1, min(N_HEADS, SCAN_ROWS_PER_STEP // t_len))
    outs = _scan(proj, gates, p["lb_logits"], p["hg_norm_g"], p["gate_bias"], p["ml_norm_g"], init,
                 bsz=bsz, t_len=t_len, hps=hps)
    hg_act, ml_act = outs[0], outs[1]
    x1, h2 = _mix_out(hg_act, ml_act, mgate, x2d, mod3, lambda i: mod_row_fn(i, tm_mix), p["g_post_mix"],
                      p["g_pre_ffn"], p["w_up_hg"], p["w_up_ml"], p["w_out"], tm=tm_mix)
    y = _ffn(h2, x1, mod3, lambda i: mod_row_fn(i, tm_ffn), p["g_post_ffn"], p["w_ffn_in"], p["w_ffn_out"],
             tm=tm_ffn, tf=tf)
    return y.reshape(bsz, t_len, D_MODEL), outs[2:]


def kernel(x_prompt, x_sample, c, state_hgrn_s, state_mlstm_c, state_mlstm_n, state_mlstm_m, c_ctx, w_mod, b_mod,
           norm_pre_mix, norm_post_mix, norm_pre_ffn, norm_post_ffn, w_in, hgrn_lb_logits, hgrn_norm_g, mlstm_b_i,
           mlstm_b_f, mlstm_norm_g, w_up_hgrn, w_up_mlstm, w_out, w_ffn_in, w_ffn_out):
    bsz_p, t_p, _ = x_prompt.shape
    bsz_s, t_s, _ = x_sample.shape

    cond = jnp.concatenate([c_ctx[None, :], c, jnp.zeros((MOD_ROWS - 1 - bsz_s, D_MODEL), F32)], axis=0)
    mod = _modulation(cond, w_mod[0], b_mod[0][None, :])
    mod3 = mod.reshape(MOD_ROWS, 1, N_MOD)

    w_main, w_gate = _prep_w_in(jnp.transpose(w_in[0]))
    p = dict(
        w_main=w_main, w_gate=w_gate,
        g_pre_mix=norm_pre_mix[0][None, :], g_post_mix=norm_post_mix[0][None, :],
        g_pre_ffn=norm_pre_ffn[0][None, :], g_post_ffn=norm_post_ffn[0][None, :],
        lb_logits=hgrn_lb_logits, hg_norm_g=hgrn_norm_g[0][None, :],
        gate_bias=jnp.tile(jnp.concatenate([mlstm_b_i[0], mlstm_b_f[0]]), LANES // GATE_W)[None, :],
        ml_norm_g=mlstm_norm_g[0][None, :],
        w_up_hg=w_up_hgrn[0].astype(BF16), w_up_ml=w_up_mlstm[0].astype(BF16), w_out=w_out[0].astype(BF16),
        w_ffn_in=w_ffn_in[0].astype(BF16), w_ffn_out=w_ffn_out[0].astype(BF16),
    )
    tiles_p = dict(tm_in=min(1024, bsz_p * t_p), tn_in=1024, tm_mix=min(512, bsz_p * t_p),
                   tm_ffn=min(1024, bsz_p * t_p), tf=512)
    tiles_s = dict(tm_in=min(1024, t_s), tn_in=1024, tm_mix=min(512, t_s), tm_ffn=min(1024, t_s), tf=512)
    y_p, ctx_states = _layer_group(x_prompt, mod3, lambda i, tm: 0, None, p, **tiles_p)
    init = (state_hgrn_s, state_mlstm_c, state_mlstm_n, state_mlstm_m[:, 0])
    y_s, _ = _layer_group(x_sample, mod3, lambda i, tm: 1 + (i * tm) // t_s, init, p, **tiles_s)

    s_fin, c_fin, n_fin, m_fin = ctx_states
    new_hgrn_s = s_fin[:, None]
    new_mlstm_c = c_fin[:, None]
    new_mlstm_n = n_fin[:, None, :, :, 0, :]
    new_mlstm_m = m_fin[:, None, :, :, 0, 0]
    return (y_p, y_s, new_hgrn_s, new_mlstm_c, new_mlstm_n, new_mlstm_m)
```

```python
import functools

import jax
import jax.numpy as jnp
from jax import lax
from jax.experimental import pallas as pl
from jax.experimental.pallas import tpu as pltpu

F32 = jnp.float32
BF16 = jnp.bfloat16

D_MODEL = 2048
N_HEADS = 8
HG_DK = 128
HG_DV = 128
ML_DQK = 64
ML_DV = 128
HEAD_W = 128
D_FF = 5632
CHUNK = 64
EPS = 1e-6
N_MOD = 6 * D_MODEL
LANES = 128
MOD_ROWS = 16
BLK = 2 * CHUNK
SCAN_ROWS_PER_STEP = 2048
BLOCKS_PER_TRIP = 8
MGATE_COLS = 256
FFN_ACC_COLS = 512
GATE_W = 4 * N_HEADS

SEC_HQ, SEC_HF_FW, SEC_HF_BW, SEC_HI, SEC_HGATE = 0, 8, 16, 24, 32
SEC_MQK, SEC_MV, SEC_MO = 40, 48, 56
SEC_GA, SEC_GB = 64, 80
PROJ_W = 96 * LANES
PLAIN_W = SEC_GA * LANES

W_MQ = 5 * N_HEADS * HG_DK
W_MK = W_MQ + N_HEADS * ML_DQK
W_MV = W_MK + N_HEADS * ML_DQK
W_GATES = W_MV + N_HEADS * ML_DV
IN_W = W_GATES + 4 * N_HEADS + N_HEADS * ML_DV + 2 * D_MODEL

VMEM_LIMIT = 56 * 1024 * 1024
VMEM_LIMIT_FFN = 60 * 1024 * 1024


def _dot(a, b):
    return jnp.dot(a, b, preferred_element_type=F32)


def _dot_nt(a, b):
    return lax.dot_general(a, b, (((1,), (1,)), ((), ())), preferred_element_type=F32)


def _dot_tn(a, b):
    return lax.dot_general(a, b, (((0,), (0,)), ((), ())), preferred_element_type=F32)


def _sigmoid_pair(z):
    t = 0.5 * jnp.tanh(0.5 * z)
    return 0.5 + t, 0.5 - t


def _sigmoid(z):
    return 0.5 * jnp.tanh(0.5 * z) + 0.5


def _silu(z):
    return z * _sigmoid(z)


def _log_sigmoid(z):
    return jnp.minimum(z, 0.0) - jnp.log1p(jnp.exp(-jnp.abs(z)))


def _mod_kernel(c_ref, w_ref, b_ref, o_ref):
    a = _silu(c_ref[...]).astype(BF16)
    o_ref[...] = _dot(a, w_ref[...].astype(BF16)) + b_ref[...]


def _modulation(cond, w_mod, b_mod):
    tn = 1024
    return pl.pallas_call(
        _mod_kernel,
        grid=(N_MOD // tn,),
        in_specs=[pl.BlockSpec((MOD_ROWS, D_MODEL), lambda j: (0, 0)),
                  pl.BlockSpec((D_MODEL, tn), lambda j: (0, j)),
                  pl.BlockSpec((1, tn), lambda j: (0, j))],
        out_specs=pl.BlockSpec((MOD_ROWS, tn), lambda j: (0, j)),
        out_shape=jax.ShapeDtypeStruct((MOD_ROWS, N_MOD), F32),
        compiler_params=pltpu.CompilerParams(dimension_semantics=("arbitrary",),
                                             vmem_limit_bytes=VMEM_LIMIT),
        name="modulation",
    )(cond, w_mod, b_mod)


def _rms_rows(x, g):
    return x * lax.rsqrt(jnp.mean(x * x, axis=-1, keepdims=True) + EPS) * g


def _inproj_kernel(x_ref, sh_ref, sc_ref, g_ref, w_ref, wg_ref, proj_ref, mgate_ref, gates_ref, xn_ref, *,
                   tm, n_plain):
    rows = 128
    j = pl.program_id(1)

    @pl.when(j == 0)
    def _():
        for r0 in range(0, tm, rows):
            x = x_ref[r0:r0 + rows, :]
            h = _rms_rows(x, g_ref[...]) * (1.0 + sc_ref[0]) + sh_ref[0]
            hb = h.astype(BF16)
            xn_ref[r0:r0 + rows, :] = hb
            gates_ref[r0:r0 + rows, :] = _dot_nt(hb, wg_ref[...])

    @pl.when(j < n_plain)
    def _():
        proj_ref[...] = _dot_nt(xn_ref[...], w_ref[...])

    @pl.when(j >= n_plain)
    def _():
        for c0 in range(0, w_ref.shape[0], MGATE_COLS):
            z = _dot_nt(xn_ref[...], w_ref[c0:c0 + MGATE_COLS, :])
            mgate_ref[:, c0:c0 + MGATE_COLS] = _sigmoid(z).astype(BF16)


def _in_proj(x2d, mod3, mod_row_fn, g_pre, w_main_t, w_gate_t, *, tm, tn):
    m = x2d.shape[0]
    n_plain = PLAIN_W // tn
    kern = functools.partial(_inproj_kernel, tm=tm, n_plain=n_plain)
    n_i = m // tm

    def x_tile(i, j):
        return jnp.minimum(i + jnp.where(j >= 1, 1, 0), n_i - 1), 0

    return pl.pallas_call(
        kern,
        grid=(n_i, PROJ_W // tn),
        in_specs=[pl.BlockSpec((tm, D_MODEL), x_tile),
                  pl.BlockSpec((1, 1, D_MODEL), lambda i, j: (mod_row_fn(i), 0, 0)),
                  pl.BlockSpec((1, 1, D_MODEL), lambda i, j: (mod_row_fn(i), 0, 1)),
                  pl.BlockSpec((1, D_MODEL), lambda i, j: (0, 0)),
                  pl.BlockSpec((tn, D_MODEL), lambda i, j: (j, 0)),
                  pl.BlockSpec((LANES, D_MODEL), lambda i, j: (0, 0))],
        out_specs=[pl.BlockSpec((tm, tn), lambda i, j: (i, jnp.minimum(j, n_plain - 1))),
                   pl.BlockSpec((tm, tn), lambda i, j: (i, jnp.maximum(j - n_plain, 0))),
                   pl.BlockSpec((tm, LANES), lambda i, j: (i, 0))],
        out_shape=[jax.ShapeDtypeStruct((m, PLAIN_W), F32),
                   jax.ShapeDtypeStruct((m, PROJ_W - PLAIN_W), BF16),
                   jax.ShapeDtypeStruct((m, LANES), F32)],
        scratch_shapes=[pltpu.VMEM((tm, D_MODEL), BF16)],
        compiler_params=pltpu.CompilerParams(dimension_semantics=("arbitrary", "arbitrary"),
                                             vmem_limit_bytes=VMEM_LIMIT),
        name="in_proj",
    )(x2d, mod3, mod3, g_pre, w_main_t, w_gate_t)


def _chunk_cumsum(x, pos, rev):
    n = x.shape[0]
    s = 1
    while s < CHUNK:
        if rev:
            x = x + jnp.where(pos < CHUNK - s, pltpu.roll(x, n - s, axis=0), 0.0)
        else:
            x = x + jnp.where(pos >= s, pltpu.roll(x, s, axis=0), 0.0)
        s *= 2
    return x


def _per_chunk(x):
    return x.reshape(BLK // CHUNK, CHUNK, x.shape[-1])


N_SCAN_IN = 13


def _scan_kernel(*refs, t_len, has_init, hps):
    n_io = N_SCAN_IN + (4 if has_init else 0) + 2 + (0 if has_init else 4)
    lead = pl.program_id(1) * hps
    for hh in range(hps):
        hcol = slice(hh * HEAD_W, (hh + 1) * HEAD_W)
        head = slice(hh, hh + 1)
        views = [r.at[:, hcol] for r in refs[:8]]
        views += [refs[8], refs[9].at[:, :, hcol], refs[10], refs[11], refs[12].at[:, hcol]]
        pos = N_SCAN_IN
        if has_init:
            views += [r.at[:, :, :, head] for r in refs[pos:pos + 3]] + [refs[pos + 3]]
            pos += 4
        views += [r.at[:, hcol] for r in refs[pos:pos + 2]]
        pos += 2
        if not has_init:
            views += [r.at[:, :, head] for r in refs[pos:pos + 4]]
        views += [r.at[hh] for r in refs[n_io:-1]] + [refs[-1]]
        _scan_head(views, lead + hh, pack_gates=hh == 0, t_len=t_len, has_init=has_init)


def _scan_head(refs, h_idx, *, pack_gates, t_len, has_init):
    (hq_ref, hff_ref, hfb_ref, hi_ref, hgate_ref, mqk_ref, mv_ref, mo_ref, gates_ref,
     lbl_ref, hgn_ref, gbias_ref, mln_ref) = refs[:13]
    pos = 13
    if has_init:
        s0_ref, c0_ref, n0_ref, m0_ref = refs[pos:pos + 4]
        pos += 4
    hg_out_ref, ml_out_ref = refs[pos:pos + 2]
    pos += 2
    if not has_init:
        sfin_ref, cfin_ref, nfin_ref, mfin_ref = refs[pos:pos + 4]
        pos += 4
    (qa_ref, ka_ref, qx_ref, vt_ref, ds_ref, dec_ref, stt_ref,
     sel_ref, g_ref, ut_ref, cloc_ref, nloc_ref, a_ref, mloc_ref, mst_ref, sold_ref, sloc_ref,
     cst_ref, nst_ref, hsum_ref, pk_ref) = refs[pos:]

    b_idx = pl.program_id(0)
    n_chunks = t_len // CHUNK
    n_blocks = t_len // BLK
    cpb = BLK // CHUNK
    mid = CHUNK // 2

    row = lax.broadcasted_iota(jnp.int32, (BLK, BLK), 0)
    col = lax.broadcasted_iota(jnp.int32, (BLK, BLK), 1)
    same_chunk = jnp.right_shift(row, 6) == jnp.right_shift(col, 6)
    masks = (same_chunk & (col <= row), same_chunk & (col >= row))
    in_pos = jnp.bitwise_and(row, CHUNK - 1)
    row_lo = row < CHUNK
    lane_lo = col < ML_DQK
    lane = lax.broadcasted_iota(jnp.int32, (1, LANES), 1)

    def first_chunk_only(x):
        return jnp.concatenate([x[:CHUNK], jnp.zeros_like(x[CHUNK:])], axis=0).astype(BF16)

    def second_chunk_only(x):
        return jnp.concatenate([jnp.zeros_like(x[:CHUNK]), x[CHUNK:]], axis=0).astype(BF16)

    logits = lbl_ref[...]
    l0, l1 = logits[:, 0, :], logits[:, 1, :]
    mx = jnp.maximum(l0, l1)
    e0, e1 = jnp.exp(l0 - mx), jnp.exp(l1 - mx)
    lb_all = e0 / (e0 + e1)

    is_f = jnp.bitwise_and(lane, GATE_W - 1) >= 2 * N_HEADS
    piece = jnp.right_shift(lane, 5)
    grp = min(BLOCKS_PER_TRIP, n_blocks)
    seq_unroll = n_chunks <= BLOCKS_PER_TRIP
    sel_rows = grp * BLK

    if pack_gates:
        @pl.when(h_idx == 0)
        def _():
            def ml_pack(i, carry):
                rows = pl.ds(pl.multiple_of(i * sel_rows, sel_rows), sel_rows)
                xg = gates_ref[rows, :] + gbias_ref[...]
                xg = jnp.where(is_f, _log_sigmoid(xg), xg)
                hi = xg.astype(BF16).astype(F32)
                rest = xg - hi
                mid = rest.astype(BF16).astype(F32)
                packed = jnp.where(piece == 0, hi, jnp.where(piece == 1, mid, rest - mid))
                pk_ref[rows, :] = packed.astype(BF16)
                return carry
            lax.fori_loop(0, t_len // sel_rows, ml_pack, 0)


    def hg_prep(i, carry):
        rows = pl.ds(pl.multiple_of(i * BLK, BLK), BLK)
        q = _silu(hq_ref[rows, :])
        vt = jnp.transpose(hi_ref[rows, :]).astype(BF16)
        vt_ref[i] = vt
        kd_cols = []
        for d, zf_ref in enumerate((hff_ref, hfb_ref)):
            rev = d == 1
            lb = lb_all[d:d + 1]
            oml = 1.0 - lb
            sp, sn = _sigmoid_pair(zf_ref[rows, :])
            k = oml * sn
            b = _chunk_cumsum(jnp.log2(lb + oml * sp), in_pos, rev)
            b3, q3, k3 = _per_chunk(b), _per_chunk(q), _per_chunk(k)
            i_mid, i_last = (CHUNK - 1 - mid, 0) if rev else (mid, CHUNK - 1)
            b_mid, b_last = b3[:, i_mid:i_mid + 1], b3[:, i_last:i_last + 1]
            qa_ref[d, rows, :] = (q3 * jnp.exp2(b3 - b_mid)).reshape(BLK, HG_DK).astype(BF16)
            ka_ref[d, rows, :] = (k3 * jnp.exp2(b_mid - b3)).reshape(BLK, HG_DK).astype(BF16)
            qi = q * jnp.exp2(b)
            qx_ref[d, rows, :HG_DK] = first_chunk_only(qi)
            qx_ref[d, rows, HG_DK:] = second_chunk_only(qi)
            kd = (k3 * jnp.exp2(b_last - b3)).reshape(BLK, HG_DK)
            kd_cols += [first_chunk_only(kd), second_chunk_only(kd)]
            dec = jnp.exp2(b_last)
            for c in range(cpb):
                dec_ref[d, i * cpb + c] = dec[c]
        ds_t = _dot(vt, jnp.concatenate(kd_cols, axis=1))
        for d in range(2):
            for c in range(cpb):
                j = d * cpb + c
                ds_ref[d, i * cpb + c] = ds_t[:, j * HG_DK:(j + 1) * HG_DK]
        return carry

    if has_init:
        st0 = tuple(jnp.transpose(s0_ref[0, 0, d, 0]) for d in range(2))
    else:
        st0 = (jnp.zeros((HG_DV, HG_DK), F32),) * 2

    def hg_state(n, carry):
        new = []
        for d in range(2):
            cn = (n_chunks - 1 - n) if d == 1 else n
            stt_ref[d, cn] = carry[d].astype(BF16)
            new.append(dec_ref[d, cn] * carry[d] + ds_ref[d, cn])
        return tuple(new)

    def hg_out(i, carry):
        rows = pl.ds(pl.multiple_of(i * BLK, BLK), BLK)
        vt = vt_ref[i]
        lhs, rhs_t = [], []
        for d in range(2):
            att = jnp.where(masks[d], _dot_nt(qa_ref[d, rows, :], ka_ref[d, rows, :]), 0.0)
            lhs += [att.astype(BF16), qx_ref[d, rows, :]]
            rhs_t += [vt] + [stt_ref[d, i * cpb + c] for c in range(cpb)]
        o = _dot_nt(jnp.concatenate(lhs, axis=1), jnp.concatenate(rhs_t, axis=1))
        o = (o * lax.rsqrt(jnp.mean(o * o, axis=-1, keepdims=True) + EPS) * hgn_ref[...]
             * _silu(hgate_ref[rows, :]))
        hg_out_ref[rows, :] = o.astype(BF16)
        return carry


    r_i = lax.broadcasted_iota(jnp.int32, (LANES, 4 * LANES), 0)
    c_i = lax.broadcasted_iota(jnp.int32, (LANES, 4 * LANES), 1)
    cblk = jnp.right_shift(c_i, 7)
    target = h_idx + jnp.where(cblk == 0, 0, jnp.where(cblk == 1, 2 * N_HEADS,
                                                       jnp.where(cblk == 2, N_HEADS, 3 * N_HEADS)))
    onehot = jnp.where((jnp.bitwise_and(r_i, GATE_W - 1) == target) & (r_i < 3 * GATE_W), 1.0, 0.0).astype(BF16)

    def ml_select(i, carry):
        rows = pl.ds(pl.multiple_of(i * sel_rows, sel_rows), sel_rows)
        sel_ref[rows, :] = _dot(pk_ref[rows, :], onehot)
        return carry

    def ml_prep(i, carry):
        rows = pl.ds(pl.multiple_of(i * BLK, BLK), BLK)
        qk = mqk_ref[rows, :]
        kk = jnp.where(lane_lo, pltpu.roll(qk, ML_DQK, axis=1), qk) * (ML_DQK ** -0.5)
        v = mv_ref[rows, :]
        e_dir = []
        for d in range(2):
            rev = d == 1
            li = sel_ref[rows, (2 * d) * LANES:(2 * d + 1) * LANES]
            lf = sel_ref[rows, (2 * d + 1) * LANES:(2 * d + 2) * LANES]
            g = _chunk_cumsum(lf, in_pos, rev)
            u = li - g
            g_ref[d, rows, :] = g
            ut_ref[d, rows, :] = jnp.transpose(u)
            i_last = 0 if rev else CHUNK - 1
            g_last = _per_chunk(g)[:, i_last:i_last + 1]
            w_end = g_last + _per_chunk(u)
            m_loc = jnp.max(w_end, axis=1, keepdims=True)
            e_dir.append(jnp.exp(w_end - m_loc).reshape(BLK, LANES))
            for c in range(cpb):
                a_ref[d, i * cpb + c] = g_last[c]
                mloc_ref[d, i * cpb + c] = m_loc[c]
        wk = jnp.where(lane_lo, e_dir[0], e_dir[1]) * kk
        v_cols = jnp.concatenate([first_chunk_only(v), second_chunk_only(v)], axis=1)
        c_loc = _dot_tn(wk.astype(BF16), v_cols)
        n_loc = jnp.sum(_per_chunk(wk), axis=1, keepdims=True)
        for c in range(cpb):
            nloc_ref[i * cpb + c] = n_loc[c]
            for d in range(2):
                cloc_ref[d, i * cpb + c] = c_loc[d * ML_DQK:(d + 1) * ML_DQK, c * ML_DV:(c + 1) * ML_DV]
        return carry

    def sel_trip(t, carry):
        ml_select(t, carry)
        for u in range(grp):
            ml_prep(t * grp + u, carry)
        return carry
    lax.fori_loop(0, n_blocks // grp, sel_trip, 0)

    if has_init:
        m0 = tuple(jnp.full((1, LANES), m0_ref[b_idx, d * N_HEADS + h_idx], F32) for d in range(2))
    else:
        m0 = (jnp.zeros((1, LANES), F32),) * 2

    def ml_stab(n, carry):
        new = []
        for d in range(2):
            cn = (n_chunks - 1 - n) if d == 1 else n
            m_st = carry[d]
            a, m_loc = a_ref[d, cn], mloc_ref[d, cn]
            m_new = jnp.maximum(a + m_st, m_loc)
            mst_ref[d, cn] = m_st
            sold_ref[d, cn] = jnp.exp(a + m_st - m_new)
            sloc_ref[d, cn] = jnp.exp(m_loc - m_new)
            new.append(m_new)
        return tuple(new)
    m_fin = lax.fori_loop(0, n_chunks, ml_stab, m0, unroll=seq_unroll)

    if has_init:
        c0 = tuple(c0_ref[0, 0, d, 0] for d in range(2))
        n0 = tuple(jnp.concatenate([n0_ref[0, 0, d, 0]] * 2, axis=1) for d in range(2))
    else:
        c0 = (jnp.zeros((ML_DQK, ML_DV), F32),) * 2
        n0 = (jnp.zeros((1, LANES), F32),) * 2

    def ml_state(n, carry):
        cs, ns = carry
        new_c, new_n = [], []
        for d in range(2):
            cn = (n_chunks - 1 - n) if d == 1 else n
            cst_ref[d, cn] = cs[d].astype(BF16)
            nst_ref[d, cn] = ns[d]
            s_old, s_loc = sold_ref[d, cn], sloc_ref[d, cn]
            new_c.append(s_old * cs[d] + s_loc * cloc_ref[d, cn])
            new_n.append(s_old * ns[d] + s_loc * nloc_ref[cn])
        return tuple(new_c), tuple(new_n)
    c_fin, n_fin = lax.fori_loop(0, n_chunks, ml_state, (c0, n0), unroll=seq_unroll)

    def ml_out(i, carry):
        rows = pl.ds(pl.multiple_of(i * BLK, BLK), BLK)
        chunks = pl.ds(i * cpb, cpb)
        qk = mqk_ref[rows, :]
        qb = qk[:, :ML_DQK].astype(BF16)
        kb = (qk[:, ML_DQK:] * (ML_DQK ** -0.5)).astype(BF16)
        vb = mv_ref[rows, :].astype(BF16)
        qk_raw = _dot_nt(qb, kb)
        qq = jnp.where(lane_lo, qk, pltpu.roll(qk, ML_DQK, axis=1))
        q_cols = jnp.where(lane_lo == row_lo, qq, 0.0).astype(BF16)
        c_rows = jnp.concatenate([cst_ref[d, chunks].reshape(BLK, ML_DV) for d in range(2)], axis=1)
        qc = _dot(q_cols, c_rows)
        parts = []
        for d in range(2):
            g = g_ref[d, rows, :]
            log_inter = (_per_chunk(g) + mst_ref[d, chunks]).reshape(BLK, LANES)
            log_d = jnp.where(masks[d], g + ut_ref[d, rows, :], -jnp.inf)
            m_t = jnp.maximum(log_inter, jnp.max(log_d, axis=-1, keepdims=True))
            s_inter = jnp.exp(log_inter - m_t)
            s_qk = qk_raw * jnp.exp(log_d - m_t)
            own_half = lane_lo if d == 0 else jnp.logical_not(lane_lo)
            q_n = jnp.where(own_half, (_per_chunk(qq) * nst_ref[d, chunks]).reshape(BLK, LANES), 0.0)
            den = jnp.sum(s_qk + s_inter * q_n, axis=-1, keepdims=True)
            parts.append((s_qk, s_inter, jnp.maximum(jnp.abs(den), jnp.exp(-m_t[:, 0:1]))))
        nv = _dot(jnp.concatenate([parts[0][0], parts[1][0]], axis=0).astype(BF16), vb)
        o = None
        for d in range(2):
            _, s_inter, den = parts[d]
            h = (nv[d * BLK:(d + 1) * BLK] + s_inter * qc[:, d * ML_DV:(d + 1) * ML_DV]) / den
            o = h if o is None else o + h
        hsum_ref[rows, :] = o
        return carry

    def ml_norm(i, carry):
        rows = pl.ds(pl.multiple_of(i * BLK, BLK), BLK)
        o = hsum_ref[rows, :]
        mu = jnp.mean(o, axis=-1, keepdims=True)
        oc = o - mu
        var = jnp.mean(oc * oc, axis=-1, keepdims=True)
        o = oc * lax.rsqrt(var + EPS) * mln_ref[...] * _sigmoid(mo_ref[rows, :])
        ml_out_ref[rows, :] = o.astype(BF16)
        return carry
    pair = min(2, n_blocks)

    def out_trip(t, carry):
        for u in range(pair):
            ml_out(t * pair + u, carry)
            hg_prep(t * pair + u, carry)
        return carry
    lax.fori_loop(0, n_blocks // pair, out_trip, 0)

    st_fin = lax.fori_loop(0, n_chunks, hg_state, st0, unroll=seq_unroll)

    def fin_trip(t, carry):
        for u in range(grp):
            hg_out(t * grp + u, carry)
            ml_norm(t * grp + u, carry)
        return carry
    lax.fori_loop(0, n_blocks // grp, fin_trip, 0)

    if not has_init:
        for d in range(2):
            sfin_ref[0, d, 0] = jnp.transpose(st_fin[d])
            cfin_ref[0, d, 0] = c_fin[d]
            nfin_ref[0, d, 0] = n_fin[d][:, d * ML_DQK:(d + 1) * ML_DQK]
            mfin_ref[0, d, 0] = m_fin[d]


def _scan(proj, gates, lb_logits, hg_norm_g, gate_bias, ml_norm_g, init, *, bsz, t_len, hps):
    has_init = init is not None
    m = bsz * t_len
    n_chunks = t_len // CHUNK
    n_blocks = t_len // BLK
    wide = hps * HEAD_W

    def col(sec):
        return pl.BlockSpec((t_len, wide), lambda b, h, sec=sec: (b, sec // hps + h))

    in_specs = [col(SEC_HQ), col(SEC_HF_FW), col(SEC_HF_BW), col(SEC_HI), col(SEC_HGATE),
                col(SEC_MQK), col(SEC_MV), col(SEC_MO),
                pl.BlockSpec((t_len, LANES), lambda b, h: (b, 0)),
                pl.BlockSpec((2, 2, wide), lambda b, h: (0, 0, h)),
                pl.BlockSpec((1, HG_DV), lambda b, h: (0, 0)),
                pl.BlockSpec((1, LANES), lambda b, h: (0, 0)),
                pl.BlockSpec((1, wide), lambda b, h: (0, h))]
    args = [proj] * 8 + [gates, lb_logits, hg_norm_g, gate_bias, ml_norm_g]
    out_specs = [pl.BlockSpec((t_len, wide), lambda b, h: (b, h)),
                 pl.BlockSpec((t_len, wide), lambda b, h: (b, h))]
    out_shape = [jax.ShapeDtypeStruct((m, N_HEADS * HG_DV), BF16),
                 jax.ShapeDtypeStruct((m, N_HEADS * ML_DV), BF16)]
    if has_init:
        s0, c0, n0, m0 = init
        in_specs += [pl.BlockSpec((1, 1, 2, hps, HG_DK, HG_DV), lambda b, h: (b, 0, 0, h, 0, 0)),
                     pl.BlockSpec((1, 1, 2, hps, ML_DQK, ML_DV), lambda b, h: (b, 0, 0, h, 0, 0)),
                     pl.BlockSpec((1, 1, 2, hps, 1, ML_DQK), lambda b, h: (b, 0, 0, h, 0, 0)),
                     pl.BlockSpec(memory_space=pltpu.SMEM)]
        args += [s0, c0, n0.reshape(n0.shape[:4] + (1, ML_DQK)), m0.reshape(bsz, 2 * N_HEADS)]
    else:
        out_specs += [pl.BlockSpec((1, 2, hps, HG_DK, HG_DV), lambda b, h: (b, 0, h, 0, 0)),
                      pl.BlockSpec((1, 2, hps, ML_DQK, ML_DV), lambda b, h: (b, 0, h, 0, 0)),
                      pl.BlockSpec((1, 2, hps, 1, ML_DQK), lambda b, h: (b, 0, h, 0, 0)),
                      pl.BlockSpec((1, 2, hps, 1, LANES), lambda b, h: (b, 0, h, 0, 0))]
        out_shape += [jax.ShapeDtypeStruct((bsz, 2, N_HEADS, HG_DK, HG_DV), F32),
                      jax.ShapeDtypeStruct((bsz, 2, N_HEADS, ML_DQK, ML_DV), F32),
                      jax.ShapeDtypeStruct((bsz, 2, N_HEADS, 1, ML_DQK), F32),
                      jax.ShapeDtypeStruct((bsz, 2, N_HEADS, 1, LANES), F32)]

    def per_head(shape, dtype):
        return pltpu.VMEM((hps,) + shape, dtype)

    def per_chunk_row(*lead):
        return per_head(lead + (n_chunks, 1, LANES), F32)

    scratch = [per_head((2, t_len, HG_DK), BF16),
               per_head((2, t_len, HG_DK), BF16),
               per_head((2, t_len, 2 * HG_DK), BF16),
               per_head((n_blocks, HG_DV, BLK), BF16),
               per_head((2, n_chunks, HG_DV, HG_DK), F32),
               per_chunk_row(2),
               per_head((2, n_chunks, HG_DV, HG_DK), BF16),
               per_head((t_len, 4 * LANES), F32),
               per_head((2, t_len, LANES), F32),
               per_head((2, t_len, BLK), F32),
               per_head((2, n_chunks, ML_DQK, ML_DV), F32),
               per_chunk_row(),
               per_chunk_row(2),
               per_chunk_row(2),
               per_chunk_row(2),
               per_chunk_row(2),
               per_chunk_row(2),
               per_head((2, n_chunks, ML_DQK, ML_DV), BF16),
               per_chunk_row(2),
               per_head((t_len, ML_DV), F32),
               pltpu.VMEM((t_len, LANES), BF16)]
    kern = functools.partial(_scan_kernel, t_len=t_len, has_init=has_init, hps=hps)
    return pl.pallas_call(
        kern,
        grid=(bsz, N_HEADS // hps),
        in_specs=in_specs,
        out_specs=out_specs,
        out_shape=out_shape,
        scratch_shapes=scratch,
        compiler_params=pltpu.CompilerParams(dimension_semantics=("arbitrary", "arbitrary"),
                                             vmem_limit_bytes=VMEM_LIMIT),
        name="scan_init" if has_init else "scan_ctx",
    )(*args)


def _mix_out_kernel(hg_ref, ml_ref, ga_ref, gb_ref, x_ref, gt_ref, sh_ref, sc_ref, gpost_ref, gpre_ref,
                    wuh_ref, wum_ref, wo_ref, x1_ref, h2_ref):
    y_hg = _dot(hg_ref[...], wuh_ref[...])
    y_ml = _dot(ml_ref[...], wum_ref[...])
    merged = ga_ref[...].astype(F32) * y_hg + gb_ref[...].astype(F32) * y_ml
    y = _dot(merged.astype(BF16), wo_ref[...])
    x1 = x_ref[...] + gt_ref[0] * _rms_rows(y, gpost_ref[...])
    x1_ref[...] = x1
    h2 = _rms_rows(x1, gpre_ref[...]) * (1.0 + sc_ref[0]) + sh_ref[0]
    h2_ref[...] = h2.astype(BF16)


def _mix_out(hg_act, ml_act, mgate, x2d, mod3, mod_row_fn, g_post_mix, g_pre_ffn, w_up_hg, w_up_ml, w_out, *, tm):
    m = x2d.shape[0]

    def resident(shape):
        return pl.BlockSpec(shape, lambda i: (0, 0), pipeline_mode=pl.Buffered(1))

    return pl.pallas_call(
        _mix_out_kernel,
        grid=(m // tm,),
        in_specs=[pl.BlockSpec((tm, N_HEADS * HG_DV), lambda i: (i, 0)),
                  pl.BlockSpec((tm, N_HEADS * ML_DV), lambda i: (i, 0)),
                  pl.BlockSpec((tm, D_MODEL), lambda i: (i, 0)),
                  pl.BlockSpec((tm, D_MODEL), lambda i: (i, 1)),
                  pl.BlockSpec((tm, D_MODEL), lambda i: (i, 0)),
                  pl.BlockSpec((1, 1, D_MODEL), lambda i: (mod_row_fn(i), 0, 2)),
                  pl.BlockSpec((1, 1, D_MODEL), lambda i: (mod_row_fn(i), 0, 3)),
                  pl.BlockSpec((1, 1, D_MODEL), lambda i: (mod_row_fn(i), 0, 4)),
                  resident((1, D_MODEL)),
                  resident((1, D_MODEL)),
                  resident((N_HEADS * HG_DV, D_MODEL)),
                  resident((N_HEADS * ML_DV, D_MODEL)),
                  resident((D_MODEL, D_MODEL))],
        out_specs=[pl.BlockSpec((tm, D_MODEL), lambda i: (i, 0)),
                   pl.BlockSpec((tm, D_MODEL), lambda i: (i, 0))],
        out_shape=[jax.ShapeDtypeStruct((m, D_MODEL), F32),
                   jax.ShapeDtypeStruct((m, D_MODEL), BF16)],
        compiler_params=pltpu.CompilerParams(dimension_semantics=("arbitrary",),
                                             vmem_limit_bytes=VMEM_LIMIT),
        name="mix_out",
    )(hg_act, ml_act, mgate, mgate, x2d, mod3, mod3, mod3, g_post_mix, g_pre_ffn, w_up_hg, w_up_ml, w_out)


def _ffn_kernel(h2_ref, x1_ref, gt_ref, gpost_ref, wa_ref, wb_ref, wo_ref, out_ref):
    j = pl.program_id(1)

    @pl.when(j == 0)
    def _():
        out_ref[...] = jnp.zeros_like(out_ref)

    h2 = h2_ref[...]
    hid = (_silu(_dot(h2, wa_ref[...])) * _dot(h2, wb_ref[...])).astype(BF16)
    for n0 in range(0, D_MODEL, FFN_ACC_COLS):
        out_ref[:, n0:n0 + FFN_ACC_COLS] += _dot(hid, wo_ref[:, n0:n0 + FFN_ACC_COLS])

    @pl.when(j == pl.num_programs(1) - 1)
    def _():
        rows = 128

        def body(i, carry):
            r = pl.ds(pl.multiple_of(i * rows, rows), rows)
            out_ref[r, :] = x1_ref[r, :] + gt_ref[0] * _rms_rows(out_ref[r, :], gpost_ref[...])
            return carry
        lax.fori_loop(0, out_ref.shape[0] // rows, body, 0)


def _ffn(h2, x1, mod3, mod_row_fn, g_post_ffn, w_ffn_in, w_ffn_out, *, tm, tf):
    m = x1.shape[0]
    nf = D_FF // tf

    def x1_tile(i, j):
        return jnp.where(j == nf - 1, i, jnp.maximum(i - 1, 0)), 0

    return pl.pallas_call(
        _ffn_kernel,
        grid=(m // tm, nf),
        in_specs=[pl.BlockSpec((tm, D_MODEL), lambda i, j: (i, 0)),
                  pl.BlockSpec((tm, D_MODEL), x1_tile),
                  pl.BlockSpec((1, 1, D_MODEL), lambda i, j: (mod_row_fn(i), 0, 5)),
                  pl.BlockSpec((1, D_MODEL), lambda i, j: (0, 0)),
                  pl.BlockSpec((D_MODEL, tf), lambda i, j: (0, j)),
                  pl.BlockSpec((D_MODEL, tf), lambda i, j, nf=nf: (0, j + nf)),
                  pl.BlockSpec((tf, D_MODEL), lambda i, j: (j, 0))],
        out_specs=pl.BlockSpec((tm, D_MODEL), lambda i, j: (i, 0)),
        out_shape=jax.ShapeDtypeStruct((m, D_MODEL), F32),
        compiler_params=pltpu.CompilerParams(dimension_semantics=("arbitrary", "arbitrary"),
                                             vmem_limit_bytes=VMEM_LIMIT_FFN),
        name="ffn",
    )(h2, x1, mod3, g_post_ffn, w_ffn_in, w_ffn_in, w_ffn_out)


def _w_prep_kernel(wt_ref, main_ref, gate_ref):
    x = wt_ref[...]
    pieces = [x[:W_MQ]]
    for h in range(N_HEADS):
        pieces += [x[W_MQ + h * ML_DQK:W_MQ + (h + 1) * ML_DQK], x[W_MK + h * ML_DQK:W_MK + (h + 1) * ML_DQK]]
    pieces += [x[W_MV:W_GATES], x[W_GATES + GATE_W:]]
    main_ref[...] = jnp.concatenate(pieces, axis=0).astype(BF16)
    gate_ref[...] = jnp.concatenate([x[W_GATES:W_GATES + GATE_W]] * (LANES // GATE_W), axis=0).astype(BF16)


def _prep_w_in(w_t):
    cols = 256
    return pl.pallas_call(
        _w_prep_kernel,
        grid=(D_MODEL // cols,),
        in_specs=[pl.BlockSpec((IN_W, cols), lambda i: (0, i))],
        out_specs=[pl.BlockSpec((PROJ_W, cols), lambda i: (0, i)),
                   pl.BlockSpec((LANES, cols), lambda i: (0, i))],
        out_shape=[jax.ShapeDtypeStruct((PROJ_W, D_MODEL), BF16),
                   jax.ShapeDtypeStruct((LANES, D_MODEL), BF16)],
        compiler_params=pltpu.CompilerParams(dimension_semantics=("arbitrary",),
                                             vmem_limit_bytes=VMEM_LIMIT),
        name="w_prep",
    )(w_t)


def _layer_group(x, mod3, mod_row_fn, init, p, *, tm_in, tn_in, tm_mix, tm_ffn, tf):
    bsz, t_len, _ = x.shape
    x2d = x.reshape(bsz * t_len, D_MODEL)
    proj, mgate, gates = _in_proj(x2d, mod3, lambda i: mod_row_fn(i, tm_in), p["g_pre_mix"], p["w_main"],
                                  p["w_gate"], tm=tm_in, tn=tn_in)
    hps = max(1, min(N_HEADS, SCAN_ROWS_PER_STEP // t_len))
    outs = _scan(proj, gates, p["lb_logits"], p["hg_norm_g"], p["gate_bias"], p["ml_norm_g"], init,
                 bsz=bsz, t_len=t_len, hps=hps)
    hg_act, ml_act = outs[0], outs[1]
    x1, h2 = _mix_out(hg_act, ml_act, mgate, x2d, mod3, lambda i: mod_row_fn(i, tm_mix), p["g_post_mix"],
                      p["g_pre_ffn"], p["w_up_hg"], p["w_up_ml"], p["w_out"], tm=tm_mix)
    y = _ffn(h2, x1, mod3, lambda i: mod_row_fn(i, tm_ffn), p["g_post_ffn"], p["w_ffn_in"], p["w_ffn_out"],
             tm=tm_ffn, tf=tf)
    return y.reshape(bsz, t_len, D_MODEL), outs[2:]


def kernel(x_prompt, x_sample, c, state_hgrn_s, state_mlstm_c, state_mlstm_n, state_mlstm_m, c_ctx, w_mod, b_mod,
           norm_pre_mix, norm_post_mix, norm_pre_ffn, norm_post_ffn, w_in, hgrn_lb_logits, hgrn_norm_g, mlstm_b_i,
           mlstm_b_f, mlstm_norm_g, w_up_hgrn, w_up_mlstm, w_out, w_ffn_in, w_ffn_out):
    bsz_p, t_p, _ = x_prompt.shape
    bsz_s, t_s, _ = x_sample.shape

    cond = jnp.concatenate([c_ctx[None, :], c, jnp.zeros((MOD_ROWS - 1 - bsz_s, D_MODEL), F32)], axis=0)
    mod = _modulation(cond, w_mod[0], b_mod[0][None, :])
    mod3 = mod.reshape(MOD_ROWS, 1, N_MOD)

    w_main, w_gate = _prep_w_in(jnp.transpose(w_in[0]))
    p = dict(
        w_main=w_main, w_gate=w_gate,
        g_pre_mix=norm_pre_mix[0][None, :], g_post_mix=norm_post_mix[0][None, :],
        g_pre_ffn=norm_pre_ffn[0][None, :], g_post_ffn=norm_post_ffn[0][None, :],
        lb_logits=hgrn_lb_logits, hg_norm_g=hgrn_norm_g[0][None, :],
        gate_bias=jnp.tile(jnp.concatenate([mlstm_b_i[0], mlstm_b_f[0]]), LANES // GATE_W)[None, :],
        ml_norm_g=mlstm_norm_g[0][None, :],
        w_up_hg=w_up_hgrn[0].astype(BF16), w_up_ml=w_up_mlstm[0].astype(BF16), w_out=w_out[0].astype(BF16),
        w_ffn_in=w_ffn_in[0].astype(BF16), w_ffn_out=w_ffn_out[0].astype(BF16),
    )
    tiles_p = dict(tm_in=min(1024, bsz_p * t_p), tn_in=1024, tm_mix=min(512, bsz_p * t_p),
                   tm_ffn=min(1024, bsz_p * t_p), tf=512)
    tiles_s = dict(tm_in=min(1024, t_s), tn_in=1024, tm_mix=min(512, t_s), tm_ffn=min(1024, t_s), tf=512)
    y_p, ctx_states = _layer_group(x_prompt, mod3, lambda i, tm: 0, None, p, **tiles_p)
    init = (state_hgrn_s, state_mlstm_c, state_mlstm_n, state_mlstm_m[:, 0])
    y_s, _ = _layer_group(x_sample, mod3, lambda i, tm: 1 + (i * tm) // t_s, init, p, **tiles_s)

    s_fin, c_fin, n_fin, m_fin = ctx_states
    new_hgrn_s = s_fin[:, None]
    new_mlstm_c = c_fin[:, None]
    new_mlstm_n = n_fin[:, None, :, :, 0, :]
    new_mlstm_m = m_fin[:, None, :, :, 0, 0]
    return (y_p, y_s, new_hgrn_s, new_mlstm_c, new_mlstm_n, new_mlstm_m)
```

```python
import functools

import jax
import jax.numpy as jnp
from jax import lax
from jax.experimental import pallas as pl
from jax.experimental.pallas import tpu as pltpu

F32 = jnp.float32
BF16 = jnp.bfloat16

D_MODEL = 2048
N_HEADS = 8
HG_DK = 128
HG_DV = 128
ML_DQK = 64
ML_DV = 128
HEAD_W = 128
D_FF = 5632
CHUNK = 64
EPS = 1e-6
N_MOD = 6 * D_MODEL
LANES = 128
MOD_ROWS = 16
BLK = 2 * CHUNK
SCAN_ROWS_PER_STEP = 2048
BLOCKS_PER_TRIP = 8
MGATE_COLS = 256
FFN_ACC_COLS = 512
GATE_W = 4 * N_HEADS

SEC_HQ, SEC_HF_FW, SEC_HF_BW, SEC_HI, SEC_HGATE = 0, 8, 16, 24, 32
SEC_MQK, SEC_MV, SEC_MO = 40, 48, 56
SEC_GA, SEC_GB = 64, 80
PROJ_W = 96 * LANES
PLAIN_W = SEC_GA * LANES

W_MQ = 5 * N_HEADS * HG_DK
W_MK = W_MQ + N_HEADS * ML_DQK
W_MV = W_MK + N_HEADS * ML_DQK
W_GATES = W_MV + N_HEADS * ML_DV
IN_W = W_GATES + 4 * N_HEADS + N_HEADS * ML_DV + 2 * D_MODEL

VMEM_LIMIT = 56 * 1024 * 1024
VMEM_LIMIT_FFN = 60 * 1024 * 1024


def _dot(a, b):
    return jnp.dot(a, b, preferred_element_type=F32)


def _dot_nt(a, b):
    return lax.dot_general(a, b, (((1,), (1,)), ((), ())), preferred_element_type=F32)


def _dot_tn(a, b):
    return lax.dot_general(a, b, (((0,), (0,)), ((), ())), preferred_element_type=F32)


def _sigmoid_pair(z):
    t = 0.5 * jnp.tanh(0.5 * z)
    return 0.5 + t, 0.5 - t


def _sigmoid(z):
    return 0.5 * jnp.tanh(0.5 * z) + 0.5


def _silu(z):
    return z * _sigmoid(z)


def _log_sigmoid(z):
    return jnp.minimum(z, 0.0) - jnp.log1p(jnp.exp(-jnp.abs(z)))


def _mod_kernel(c_ref, w_ref, b_ref, o_ref):
    a = _silu(c_ref[...]).astype(BF16)
    o_ref[...] = _dot(a, w_ref[...].astype(BF16)) + b_ref[...]


def _modulation(cond, w_mod, b_mod):
    tn = 1024
    return pl.pallas_call(
        _mod_kernel,
        grid=(N_MOD // tn,),
        in_specs=[pl.BlockSpec((MOD_ROWS, D_MODEL), lambda j: (0, 0)),
                  pl.BlockSpec((D_MODEL, tn), lambda j: (0, j)),
                  pl.BlockSpec((1, tn), lambda j: (0, j))],
        out_specs=pl.BlockSpec((MOD_ROWS, tn), lambda j: (0, j)),
        out_shape=jax.ShapeDtypeStruct((MOD_ROWS, N_MOD), F32),
        compiler_params=pltpu.CompilerParams(dimension_semantics=("arbitrary",),
                                             vmem_limit_bytes=VMEM_LIMIT),
        name="modulation",
    )(cond, w_mod, b_mod)


def _rms_rows(x, g):
    return x * lax.rsqrt(jnp.mean(x * x, axis=-1, keepdims=True) + EPS) * g


def _inproj_kernel(x_ref, sh_ref, sc_ref, g_ref, w_ref, wg_ref, proj_ref, mgate_ref, gates_ref, xn_ref, *,
                   tm, n_plain):
    rows = 128
    j = pl.program_id(1)

    @pl.when(j == 0)
    def _():
        for r0 in range(0, tm, rows):
            x = x_ref[r0:r0 + rows, :]
            h = _rms_rows(x, g_ref[...]) * (1.0 + sc_ref[0]) + sh_ref[0]
            hb = h.astype(BF16)
            xn_ref[r0:r0 + rows, :] = hb
            gates_ref[r0:r0 + rows, :] = _dot_nt(hb, wg_ref[...])

    @pl.when(j < n_plain)
    def _():
        proj_ref[...] = _dot_nt(xn_ref[...], w_ref[...])

    @pl.when(j >= n_plain)
    def _():
        for c0 in range(0, w_ref.shape[0], MGATE_COLS):
            z = _dot_nt(xn_ref[...], w_ref[c0:c0 + MGATE_COLS, :])
            mgate_ref[:, c0:c0 + MGATE_COLS] = _sigmoid(z).astype(BF16)


def _in_proj(x2d, mod3, mod_row_fn, g_pre, w_main_t, w_gate_t, *, tm, tn):
    m = x2d.shape[0]
    n_plain = PLAIN_W // tn
    kern = functools.partial(_inproj_kernel, tm=tm, n_plain=n_plain)
    return pl.pallas_call(
        kern,
        grid=(m // tm, PROJ_W // tn),
        in_specs=[pl.BlockSpec((tm, D_MODEL), lambda i, j: (i, 0)),
                  pl.BlockSpec((1, 1, D_MODEL), lambda i, j: (mod_row_fn(i), 0, 0)),
                  pl.BlockSpec((1, 1, D_MODEL), lambda i, j: (mod_row_fn(i), 0, 1)),
                  pl.BlockSpec((1, D_MODEL), lambda i, j: (0, 0)),
                  pl.BlockSpec((tn, D_MODEL), lambda i, j: (j, 0)),
                  pl.BlockSpec((LANES, D_MODEL), lambda i, j: (0, 0))],
        out_specs=[pl.BlockSpec((tm, tn), lambda i, j: (i, jnp.minimum(j, n_plain - 1))),
                   pl.BlockSpec((tm, tn), lambda i, j: (i, jnp.maximum(j - n_plain, 0))),
                   pl.BlockSpec((tm, LANES), lambda i, j: (i, 0))],
        out_shape=[jax.ShapeDtypeStruct((m, PLAIN_W), F32),
                   jax.ShapeDtypeStruct((m, PROJ_W - PLAIN_W), BF16),
                   jax.ShapeDtypeStruct((m, LANES), F32)],
        scratch_shapes=[pltpu.VMEM((tm, D_MODEL), BF16)],
        compiler_params=pltpu.CompilerParams(dimension_semantics=("arbitrary", "arbitrary"),
                                             vmem_limit_bytes=VMEM_LIMIT),
        name="in_proj",
    )(x2d, mod3, mod3, g_pre, w_main_t, w_gate_t)


def _chunk_cumsum(x, pos, rev):
    n = x.shape[0]
    s = 1
    while s < CHUNK:
        if rev:
            x = x + jnp.where(pos < CHUNK - s, pltpu.roll(x, n - s, axis=0), 0.0)
        else:
            x = x + jnp.where(pos >= s, pltpu.roll(x, s, axis=0), 0.0)
        s *= 2
    return x


def _per_chunk(x):
    return x.reshape(BLK // CHUNK, CHUNK, x.shape[-1])


N_SCAN_IN = 13


def _scan_kernel(*refs, t_len, has_init, hps):
    n_io = N_SCAN_IN + (4 if has_init else 0) + 2 + (0 if has_init else 4)
    lead = pl.program_id(1) * hps
    for hh in range(hps):
        hcol = slice(hh * HEAD_W, (hh + 1) * HEAD_W)
        head = slice(hh, hh + 1)
        views = [r.at[:, hcol] for r in refs[:8]]
        views += [refs[8], refs[9].at[:, :, hcol], refs[10], refs[11], refs[12].at[:, hcol]]
        pos = N_SCAN_IN
        if has_init:
            views += [r.at[:, :, :, head] for r in refs[pos:pos + 3]] + [refs[pos + 3]]
            pos += 4
        views += [r.at[:, hcol] for r in refs[pos:pos + 2]]
        pos += 2
        if not has_init:
            views += [r.at[:, :, head] for r in refs[pos:pos + 4]]
        views += [r.at[hh] for r in refs[n_io:-1]] + [refs[-1]]
        _scan_head(views, lead + hh, pack_gates=hh == 0, t_len=t_len, has_init=has_init)


def _scan_head(refs, h_idx, *, pack_gates, t_len, has_init):
    (hq_ref, hff_ref, hfb_ref, hi_ref, hgate_ref, mqk_ref, mv_ref, mo_ref, gates_ref,
     lbl_ref, hgn_ref, gbias_ref, mln_ref) = refs[:13]
    pos = 13
    if has_init:
        s0_ref, c0_ref, n0_ref, m0_ref = refs[pos:pos + 4]
        pos += 4
    hg_out_ref, ml_out_ref = refs[pos:pos + 2]
    pos += 2
    if not has_init:
        sfin_ref, cfin_ref, nfin_ref, mfin_ref = refs[pos:pos + 4]
        pos += 4
    (qa_ref, ka_ref, qx_ref, vt_ref, ds_ref, dec_ref, stt_ref,
     sel_ref, g_ref, ut_ref, cloc_ref, nloc_ref, a_ref, mloc_ref, mst_ref, sold_ref, sloc_ref,
     cst_ref, nst_ref, hsum_ref, qq_ref, qcol_ref, qkraw_ref, pk_ref) = refs[pos:]

    b_idx = pl.program_id(0)
    n_chunks = t_len // CHUNK
    n_blocks = t_len // BLK
    cpb = BLK // CHUNK
    mid = CHUNK // 2

    row = lax.broadcasted_iota(jnp.int32, (BLK, BLK), 0)
    col = lax.broadcasted_iota(jnp.int32, (BLK, BLK), 1)
    same_chunk = jnp.right_shift(row, 6) == jnp.right_shift(col, 6)
    masks = (same_chunk & (col <= row), same_chunk & (col >= row))
    in_pos = jnp.bitwise_and(row, CHUNK - 1)
    row_lo = row < CHUNK
    lane_lo = col < ML_DQK
    lane = lax.broadcasted_iota(jnp.int32, (1, LANES), 1)

    def first_chunk_only(x):
        return jnp.concatenate([x[:CHUNK], jnp.zeros_like(x[CHUNK:])], axis=0).astype(BF16)

    def second_chunk_only(x):
        return jnp.concatenate([jnp.zeros_like(x[:CHUNK]), x[CHUNK:]], axis=0).astype(BF16)

    logits = lbl_ref[...]
    l0, l1 = logits[:, 0, :], logits[:, 1, :]
    mx = jnp.maximum(l0, l1)
    e0, e1 = jnp.exp(l0 - mx), jnp.exp(l1 - mx)
    lb_all = e0 / (e0 + e1)

    is_f = jnp.bitwise_and(lane, GATE_W - 1) >= 2 * N_HEADS
    piece = jnp.right_shift(lane, 5)
    grp = min(BLOCKS_PER_TRIP, n_blocks)
    seq_unroll = n_chunks <= BLOCKS_PER_TRIP
    sel_rows = grp * BLK

    if pack_gates:
        @pl.when(h_idx == 0)
        def _():
            def ml_pack(i, carry):
                rows = pl.ds(pl.multiple_of(i * sel_rows, sel_rows), sel_rows)
                xg = gates_ref[rows, :] + gbias_ref[...]
                xg = jnp.where(is_f, _log_sigmoid(xg), xg)
                hi = xg.astype(BF16).astype(F32)
                rest = xg - hi
                mid = rest.astype(BF16).astype(F32)
                packed = jnp.where(piece == 0, hi, jnp.where(piece == 1, mid, rest - mid))
                pk_ref[rows, :] = packed.astype(BF16)
                return carry
            lax.fori_loop(0, t_len // sel_rows, ml_pack, 0)


    def hg_prep(i, carry):
        rows = pl.ds(pl.multiple_of(i * BLK, BLK), BLK)
        q = _silu(hq_ref[rows, :])
        vt = jnp.transpose(hi_ref[rows, :]).astype(BF16)
        vt_ref[i] = vt
        kd_cols = []
        for d, zf_ref in enumerate((hff_ref, hfb_ref)):
            rev = d == 1
            lb = lb_all[d:d + 1]
            oml = 1.0 - lb
            sp, sn = _sigmoid_pair(zf_ref[rows, :])
            k = oml * sn
            b = _chunk_cumsum(jnp.log2(lb + oml * sp), in_pos, rev)
            b3, q3, k3 = _per_chunk(b), _per_chunk(q), _per_chunk(k)
            i_mid, i_last = (CHUNK - 1 - mid, 0) if rev else (mid, CHUNK - 1)
            b_mid, b_last = b3[:, i_mid:i_mid + 1], b3[:, i_last:i_last + 1]
            qa_ref[d, rows, :] = (q3 * jnp.exp2(b3 - b_mid)).reshape(BLK, HG_DK).astype(BF16)
            ka_ref[d, rows, :] = (k3 * jnp.exp2(b_mid - b3)).reshape(BLK, HG_DK).astype(BF16)
            qi = q * jnp.exp2(b)
            qx_ref[d, rows, :HG_DK] = first_chunk_only(qi)
            qx_ref[d, rows, HG_DK:] = second_chunk_only(qi)
            kd = (k3 * jnp.exp2(b_last - b3)).reshape(BLK, HG_DK)
            kd_cols += [first_chunk_only(kd), second_chunk_only(kd)]
            dec = jnp.exp2(b_last)
            for c in range(cpb):
                dec_ref[d, i * cpb + c] = dec[c]
        ds_t = _dot(vt, jnp.concatenate(kd_cols, axis=1))
        for d in range(2):
            for c in range(cpb):
                j = d * cpb + c
                ds_ref[d, i * cpb + c] = ds_t[:, j * HG_DK:(j + 1) * HG_DK]
        return carry

    if has_init:
        st0 = tuple(jnp.transpose(s0_ref[0, 0, d, 0]) for d in range(2))
    else:
        st0 = (jnp.zeros((HG_DV, HG_DK), F32),) * 2

    def hg_state(n, carry):
        new = []
        for d in range(2):
            cn = (n_chunks - 1 - n) if d == 1 else n
            stt_ref[d, cn] = carry[d].astype(BF16)
            new.append(dec_ref[d, cn] * carry[d] + ds_ref[d, cn])
        return tuple(new)

    def hg_out(i, carry):
        rows = pl.ds(pl.multiple_of(i * BLK, BLK), BLK)
        vt = vt_ref[i]
        lhs, rhs_t = [], []
        for d in range(2):
            att = jnp.where(masks[d], _dot_nt(qa_ref[d, rows, :], ka_ref[d, rows, :]), 0.0)
            lhs += [att.astype(BF16), qx_ref[d, rows, :]]
            rhs_t += [vt] + [stt_ref[d, i * cpb + c] for c in range(cpb)]
        o = _dot_nt(jnp.concatenate(lhs, axis=1), jnp.concatenate(rhs_t, axis=1))
        o = (o * lax.rsqrt(jnp.mean(o * o, axis=-1, keepdims=True) + EPS) * hgn_ref[...]
             * _silu(hgate_ref[rows, :]))
        hg_out_ref[rows, :] = o.astype(BF16)
        return carry


    r_i = lax.broadcasted_iota(jnp.int32, (LANES, 4 * LANES), 0)
    c_i = lax.broadcasted_iota(jnp.int32, (LANES, 4 * LANES), 1)
    cblk = jnp.right_shift(c_i, 7)
    target = h_idx + jnp.where(cblk == 0, 0, jnp.where(cblk == 1, 2 * N_HEADS,
                                                       jnp.where(cblk == 2, N_HEADS, 3 * N_HEADS)))
    onehot = jnp.where((jnp.bitwise_and(r_i, GATE_W - 1) == target) & (r_i < 3 * GATE_W), 1.0, 0.0).astype(BF16)

    def ml_select(i, carry):
        rows = pl.ds(pl.multiple_of(i * sel_rows, sel_rows), sel_rows)
        sel_ref[rows, :] = _dot(pk_ref[rows, :], onehot)
        return carry

    def ml_prep(i, carry):
        rows = pl.ds(pl.multiple_of(i * BLK, BLK), BLK)
        qk = mqk_ref[rows, :]
        swapped = pltpu.roll(qk, ML_DQK, axis=1)
        kk = jnp.where(lane_lo, swapped, qk) * (ML_DQK ** -0.5)
        qq = jnp.where(lane_lo, qk, swapped)
        qq_ref[rows, :] = qq
        qcol_ref[rows, :] = jnp.where(lane_lo == row_lo, qq, 0.0).astype(BF16)
        qkraw_ref[rows, :] = _dot_nt(qk[:, :ML_DQK].astype(BF16), kk[:, :ML_DQK].astype(BF16))
        v = mv_ref[rows, :]
        e_dir = []
        for d in range(2):
            rev = d == 1
            li = sel_ref[rows, (2 * d) * LANES:(2 * d + 1) * LANES]
            lf = sel_ref[rows, (2 * d + 1) * LANES:(2 * d + 2) * LANES]
            g = _chunk_cumsum(lf, in_pos, rev)
            u = li - g
            g_ref[d, rows, :] = g
            ut_ref[d, rows, :] = jnp.transpose(u)
            i_last = 0 if rev else CHUNK - 1
            g_last = _per_chunk(g)[:, i_last:i_last + 1]
            w_end = g_last + _per_chunk(u)
            m_loc = jnp.max(w_end, axis=1, keepdims=True)
            e_dir.append(jnp.exp(w_end - m_loc).reshape(BLK, LANES))
            for c in range(cpb):
                a_ref[d, i * cpb + c] = g_last[c]
                mloc_ref[d, i * cpb + c] = m_loc[c]
        wk = jnp.where(lane_lo, e_dir[0], e_dir[1]) * kk
        v_cols = jnp.concatenate([first_chunk_only(v), second_chunk_only(v)], axis=1)
        c_loc = _dot_tn(wk.astype(BF16), v_cols)
        n_loc = jnp.sum(_per_chunk(wk), axis=1, keepdims=True)
        for c in range(cpb):
            nloc_ref[i * cpb + c] = n_loc[c]
            for d in range(2):
                cloc_ref[d, i * cpb + c] = c_loc[d * ML_DQK:(d + 1) * ML_DQK, c * ML_DV:(c + 1) * ML_DV]
        return carry

    def sel_trip(t, carry):
        ml_select(t, carry)
        for u in range(grp):
            ml_prep(t * grp + u, carry)
        return carry
    lax.fori_loop(0, n_blocks // grp, sel_trip, 0)

    if has_init:
        m0 = tuple(jnp.full((1, LANES), m0_ref[b_idx, d * N_HEADS + h_idx], F32) for d in range(2))
    else:
        m0 = (jnp.zeros((1, LANES), F32),) * 2

    def ml_stab(n, carry):
        new = []
        for d in range(2):
            cn = (n_chunks - 1 - n) if d == 1 else n
            m_st = carry[d]
            a, m_loc = a_ref[d, cn], mloc_ref[d, cn]
            m_new = jnp.maximum(a + m_st, m_loc)
            mst_ref[d, cn] = m_st
            sold_ref[d, cn] = jnp.exp(a + m_st - m_new)
            sloc_ref[d, cn] = jnp.exp(m_loc - m_new)
            new.append(m_new)
        return tuple(new)
    m_fin = lax.fori_loop(0, n_chunks, ml_stab, m0, unroll=seq_unroll)

    if has_init:
        c0 = tuple(c0_ref[0, 0, d, 0] for d in range(2))
        n0 = tuple(jnp.concatenate([n0_ref[0, 0, d, 0]] * 2, axis=1) for d in range(2))
    else:
        c0 = (jnp.zeros((ML_DQK, ML_DV), F32),) * 2
        n0 = (jnp.zeros((1, LANES), F32),) * 2

    def ml_state(n, carry):
        cs, ns = carry
        new_c, new_n = [], []
        for d in range(2):
            cn = (n_chunks - 1 - n) if d == 1 else n
            cst_ref[d, cn] = cs[d].astype(BF16)
            nst_ref[d, cn] = ns[d]
            s_old, s_loc = sold_ref[d, cn], sloc_ref[d, cn]
            new_c.append(s_old * cs[d] + s_loc * cloc_ref[d, cn])
            new_n.append(s_old * ns[d] + s_loc * nloc_ref[cn])
        return tuple(new_c), tuple(new_n)
    c_fin, n_fin = lax.fori_loop(0, n_chunks, ml_state, (c0, n0), unroll=seq_unroll)

    def ml_out(i, carry):
        rows = pl.ds(pl.multiple_of(i * BLK, BLK), BLK)
        chunks = pl.ds(i * cpb, cpb)
        vb = mv_ref[rows, :].astype(BF16)
        qk_raw = qkraw_ref[rows, :]
        qq = qq_ref[rows, :]
        q_cols = qcol_ref[rows, :]
        c_rows = jnp.concatenate([cst_ref[d, chunks].reshape(BLK, ML_DV) for d in range(2)], axis=1)
        qc = _dot(q_cols, c_rows)
        parts = []
        for d in range(2):
            g = g_ref[d, rows, :]
            log_inter = (_per_chunk(g) + mst_ref[d, chunks]).reshape(BLK, LANES)
            log_d = jnp.where(masks[d], g + ut_ref[d, rows, :], -jnp.inf)
            m_t = jnp.maximum(log_inter, jnp.max(log_d, axis=-1, keepdims=True))
            s_inter = jnp.exp(log_inter - m_t)
            s_qk = qk_raw * jnp.exp(log_d - m_t)
            own_half = lane_lo if d == 0 else jnp.logical_not(lane_lo)
            q_n = jnp.where(own_half, (_per_chunk(qq) * nst_ref[d, chunks]).reshape(BLK, LANES), 0.0)
            den = jnp.sum(s_qk + s_inter * q_n, axis=-1, keepdims=True)
            parts.append((s_qk, s_inter, jnp.maximum(jnp.abs(den), jnp.exp(-m_t[:, 0:1]))))
        nv = _dot(jnp.concatenate([parts[0][0], parts[1][0]], axis=0).astype(BF16), vb)
        o = None
        for d in range(2):
            _, s_inter, den = parts[d]
            h = (nv[d * BLK:(d + 1) * BLK] + s_inter * qc[:, d * ML_DV:(d + 1) * ML_DV]) / den
            o = h if o is None else o + h
        hsum_ref[rows, :] = o
        return carry

    def ml_norm(i, carry):
        rows = pl.ds(pl.multiple_of(i * BLK, BLK), BLK)
        o = hsum_ref[rows, :]
        mu = jnp.mean(o, axis=-1, keepdims=True)
        oc = o - mu
        var = jnp.mean(oc * oc, axis=-1, keepdims=True)
        o = oc * lax.rsqrt(var + EPS) * mln_ref[...] * _sigmoid(mo_ref[rows, :])
        ml_out_ref[rows, :] = o.astype(BF16)
        return carry
    pair = min(2, n_blocks)

    def out_trip(t, carry):
        for u in range(pair):
            ml_out(t * pair + u, carry)
            hg_prep(t * pair + u, carry)
        return carry
    lax.fori_loop(0, n_blocks // pair, out_trip, 0)

    st_fin = lax.fori_loop(0, n_chunks, hg_state, st0, unroll=seq_unroll)

    def fin_trip(t, carry):
        for u in range(grp):
            hg_out(t * grp + u, carry)
            ml_norm(t * grp + u, carry)
        return carry
    lax.fori_loop(0, n_blocks // grp, fin_trip, 0)

    if not has_init:
        for d in range(2):
            sfin_ref[0, d, 0] = jnp.transpose(st_fin[d])
            cfin_ref[0, d, 0] = c_fin[d]
            nfin_ref[0, d, 0] = n_fin[d][:, d * ML_DQK:(d + 1) * ML_DQK]
            mfin_ref[0, d, 0] = m_fin[d]


def _scan(proj, gates, lb_logits, hg_norm_g, gate_bias, ml_norm_g, init, *, bsz, t_len, hps):
    has_init = init is not None
    m = bsz * t_len
    n_chunks = t_len // CHUNK
    n_blocks = t_len // BLK
    wide = hps * HEAD_W

    def col(sec):
        return pl.BlockSpec((t_len, wide), lambda b, h, sec=sec: (b, sec // hps + h))

    in_specs = [col(SEC_HQ), col(SEC_HF_FW), col(SEC_HF_BW), col(SEC_HI), col(SEC_HGATE),
                col(SEC_MQK), col(SEC_MV), col(SEC_MO),
                pl.BlockSpec((t_len, LANES), lambda b, h: (b, 0)),
                pl.BlockSpec((2, 2, wide), lambda b, h: (0, 0, h)),
                pl.BlockSpec((1, HG_DV), lambda b, h: (0, 0)),
                pl.BlockSpec((1, LANES), lambda b, h: (0, 0)),
                pl.BlockSpec((1, wide), lambda b, h: (0, h))]
    args = [proj] * 8 + [gates, lb_logits, hg_norm_g, gate_bias, ml_norm_g]
    out_specs = [pl.BlockSpec((t_len, wide), lambda b, h: (b, h)),
                 pl.BlockSpec((t_len, wide), lambda b, h: (b, h))]
    out_shape = [jax.ShapeDtypeStruct((m, N_HEADS * HG_DV), BF16),
                 jax.ShapeDtypeStruct((m, N_HEADS * ML_DV), BF16)]
    if has_init:
        s0, c0, n0, m0 = init
        in_specs += [pl.BlockSpec((1, 1, 2, hps, HG_DK, HG_DV), lambda b, h: (b, 0, 0, h, 0, 0)),
                     pl.BlockSpec((1, 1, 2, hps, ML_DQK, ML_DV), lambda b, h: (b, 0, 0, h, 0, 0)),
                     pl.BlockSpec((1, 1, 2, hps, 1, ML_DQK), lambda b, h: (b, 0, 0, h, 0, 0)),
                     pl.BlockSpec(memory_space=pltpu.SMEM)]
        args += [s0, c0, n0.reshape(n0.shape[:4] + (1, ML_DQK)), m0.reshape(bsz, 2 * N_HEADS)]
    else:
        out_specs += [pl.BlockSpec((1, 2, hps, HG_DK, HG_DV), lambda b, h: (b, 0, h, 0, 0)),
                      pl.BlockSpec((1, 2, hps, ML_DQK, ML_DV), lambda b, h: (b, 0, h, 0, 0)),
                      pl.BlockSpec((1, 2, hps, 1, ML_DQK), lambda b, h: (b, 0, h, 0, 0)),
                      pl.BlockSpec((1, 2, hps, 1, LANES), lambda b, h: (b, 0, h, 0, 0))]
        out_shape += [jax.ShapeDtypeStruct((bsz, 2, N_HEADS, HG_DK, HG_DV), F32),
                      jax.ShapeDtypeStruct((bsz, 2, N_HEADS, ML_DQK, ML_DV), F32),
                      jax.ShapeDtypeStruct((bsz, 2, N_HEADS, 1, ML_DQK), F32),
                      jax.ShapeDtypeStruct((bsz, 2, N_HEADS, 1, LANES), F32)]

    def per_head(shape, dtype):
        return pltpu.VMEM((hps,) + shape, dtype)

    def per_chunk_row(*lead):
        return per_head(lead + (n_chunks, 1, LANES), F32)

    scratch = [per_head((2, t_len, HG_DK), BF16),
               per_head((2, t_len, HG_DK), BF16),
               per_head((2, t_len, 2 * HG_DK), BF16),
               per_head((n_blocks, HG_DV, BLK), BF16),
               per_head((2, n_chunks, HG_DV, HG_DK), F32),
               per_chunk_row(2),
               per_head((2, n_chunks, HG_DV, HG_DK), BF16),
               per_head((t_len, 4 * LANES), F32),
               per_head((2, t_len, LANES), F32),
               per_head((2, t_len, BLK), F32),
               per_head((2, n_chunks, ML_DQK, ML_DV), F32),
               per_chunk_row(),
               per_chunk_row(2),
               per_chunk_row(2),
               per_chunk_row(2),
               per_chunk_row(2),
               per_chunk_row(2),
               per_head((2, n_chunks, ML_DQK, ML_DV), BF16),
               per_chunk_row(2),
               per_head((t_len, ML_DV), F32),
               per_head((t_len, LANES), F32),
               per_head((t_len, LANES), BF16),
               per_head((t_len, BLK), F32),
               pltpu.VMEM((t_len, LANES), BF16)]
    kern = functools.partial(_scan_kernel, t_len=t_len, has_init=has_init, hps=hps)
    return pl.pallas_call(
        kern,
        grid=(bsz, N_HEADS // hps),
        in_specs=in_specs,
        out_specs=out_specs,
        out_shape=out_shape,
        scratch_shapes=scratch,
        compiler_params=pltpu.CompilerParams(dimension_semantics=("arbitrary", "arbitrary"),
                                             vmem_limit_bytes=VMEM_LIMIT),
        name="scan_init" if has_init else "scan_ctx",
    )(*args)


def _mix_out_kernel(hg_ref, ml_ref, ga_ref, gb_ref, x_ref, gt_ref, sh_ref, sc_ref, gpost_ref, gpre_ref,
                    wuh_ref, wum_ref, wo_ref, x1_ref, h2_ref):
    y_hg = _dot(hg_ref[...], wuh_ref[...])
    y_ml = _dot(ml_ref[...], wum_ref[...])
    merged = ga_ref[...].astype(F32) * y_hg + gb_ref[...].astype(F32) * y_ml
    y = _dot(merged.astype(BF16), wo_ref[...])
    x1 = x_ref[...] + gt_ref[0] * _rms_rows(y, gpost_ref[...])
    x1_ref[...] = x1
    h2 = _rms_rows(x1, gpre_ref[...]) * (1.0 + sc_ref[0]) + sh_ref[0]
    h2_ref[...] = h2.astype(BF16)


def _mix_out(hg_act, ml_act, mgate, x2d, mod3, mod_row_fn, g_post_mix, g_pre_ffn, w_up_hg, w_up_ml, w_out, *, tm):
    m = x2d.shape[0]

    def resident(shape):
        return pl.BlockSpec(shape, lambda i: (0, 0), pipeline_mode=pl.Buffered(1))

    return pl.pallas_call(
        _mix_out_kernel,
        grid=(m // tm,),
        in_specs=[pl.BlockSpec((tm, N_HEADS * HG_DV), lambda i: (i, 0)),
                  pl.BlockSpec((tm, N_HEADS * ML_DV), lambda i: (i, 0)),
                  pl.BlockSpec((tm, D_MODEL), lambda i: (i, 0)),
                  pl.BlockSpec((tm, D_MODEL), lambda i: (i, 1)),
                  pl.BlockSpec((tm, D_MODEL), lambda i: (i, 0)),
                  pl.BlockSpec((1, 1, D_MODEL), lambda i: (mod_row_fn(i), 0, 2)),
                  pl.BlockSpec((1, 1, D_MODEL), lambda i: (mod_row_fn(i), 0, 3)),
                  pl.BlockSpec((1, 1, D_MODEL), lambda i: (mod_row_fn(i), 0, 4)),
                  resident((1, D_MODEL)),
                  resident((1, D_MODEL)),
                  resident((N_HEADS * HG_DV, D_MODEL)),
                  resident((N_HEADS * ML_DV, D_MODEL)),
                  resident((D_MODEL, D_MODEL))],
        out_specs=[pl.BlockSpec((tm, D_MODEL), lambda i: (i, 0)),
                   pl.BlockSpec((tm, D_MODEL), lambda i: (i, 0))],
        out_shape=[jax.ShapeDtypeStruct((m, D_MODEL), F32),
                   jax.ShapeDtypeStruct((m, D_MODEL), BF16)],
        compiler_params=pltpu.CompilerParams(dimension_semantics=("arbitrary",),
                                             vmem_limit_bytes=VMEM_LIMIT),
        name="mix_out",
    )(hg_act, ml_act, mgate, mgate, x2d, mod3, mod3, mod3, g_post_mix, g_pre_ffn, w_up_hg, w_up_ml, w_out)


def _ffn_kernel(h2_ref, x1_ref, gt_ref, gpost_ref, wa_ref, wb_ref, wo_ref, out_ref):
    j = pl.program_id(1)

    @pl.when(j == 0)
    def _():
        out_ref[...] = jnp.zeros_like(out_ref)

    h2 = h2_ref[...]
    hid = (_silu(_dot(h2, wa_ref[...])) * _dot(h2, wb_ref[...])).astype(BF16)
    for n0 in range(0, D_MODEL, FFN_ACC_COLS):
        out_ref[:, n0:n0 + FFN_ACC_COLS] += _dot(hid, wo_ref[:, n0:n0 + FFN_ACC_COLS])

    @pl.when(j == pl.num_programs(1) - 1)
    def _():
        rows = 128

        def body(i, carry):
            r = pl.ds(pl.multiple_of(i * rows, rows), rows)
            out_ref[r, :] = x1_ref[r, :] + gt_ref[0] * _rms_rows(out_ref[r, :], gpost_ref[...])
            return carry
        lax.fori_loop(0, out_ref.shape[0] // rows, body, 0)


def _ffn(h2, x1, mod3, mod_row_fn, g_post_ffn, w_ffn_in, w_ffn_out, *, tm, tf):
    m = x1.shape[0]
    nf = D_FF // tf
    return pl.pallas_call(
        _ffn_kernel,
        grid=(m // tm, nf),
        in_specs=[pl.BlockSpec((tm, D_MODEL), lambda i, j: (i, 0)),
                  pl.BlockSpec((tm, D_MODEL), lambda i, j: (i, 0)),
                  pl.BlockSpec((1, 1, D_MODEL), lambda i, j: (mod_row_fn(i), 0, 5)),
                  pl.BlockSpec((1, D_MODEL), lambda i, j: (0, 0)),
                  pl.BlockSpec((D_MODEL, tf), lambda i, j: (0, j)),
                  pl.BlockSpec((D_MODEL, tf), lambda i, j, nf=nf: (0, j + nf)),
                  pl.BlockSpec((tf, D_MODEL), lambda i, j: (j, 0))],
        out_specs=pl.BlockSpec((tm, D_MODEL), lambda i, j: (i, 0)),
        out_shape=jax.ShapeDtypeStruct((m, D_MODEL), F32),
        compiler_params=pltpu.CompilerParams(dimension_semantics=("arbitrary", "arbitrary"),
                                             vmem_limit_bytes=VMEM_LIMIT_FFN),
        name="ffn",
    )(h2, x1, mod3, g_post_ffn, w_ffn_in, w_ffn_in, w_ffn_out)


def _w_prep_kernel(wt_ref, main_ref, gate_ref):
    x = wt_ref[...]
    pieces = [x[:W_MQ]]
    for h in range(N_HEADS):
        pieces += [x[W_MQ + h * ML_DQK:W_MQ + (h + 1) * ML_DQK], x[W_MK + h * ML_DQK:W_MK + (h + 1) * ML_DQK]]
    pieces += [x[W_MV:W_GATES], x[W_GATES + GATE_W:]]
    main_ref[...] = jnp.concatenate(pieces, axis=0).astype(BF16)
    gate_ref[...] = jnp.concatenate([x[W_GATES:W_GATES + GATE_W]] * (LANES // GATE_W), axis=0).astype(BF16)


def _prep_w_in(w_t):
    cols = 256
    return pl.pallas_call(
        _w_prep_kernel,
        grid=(D_MODEL // cols,),
        in_specs=[pl.BlockSpec((IN_W, cols), lambda i: (0, i))],
        out_specs=[pl.BlockSpec((PROJ_W, cols), lambda i: (0, i)),
                   pl.BlockSpec((LANES, cols), lambda i: (0, i))],
        out_shape=[jax.ShapeDtypeStruct((PROJ_W, D_MODEL), BF16),
                   jax.ShapeDtypeStruct((LANES, D_MODEL), BF16)],
        compiler_params=pltpu.CompilerParams(dimension_semantics=("arbitrary",),
                                             vmem_limit_bytes=VMEM_LIMIT),
        name="w_prep",
    )(w_t)


def _layer_group(x, mod3, mod_row_fn, init, p, *, tm_in, tn_in, tm_mix, tm_ffn, tf):
    bsz, t_len, _ = x.shape
    x2d = x.reshape(bsz * t_len, D_MODEL)
    proj, mgate, gates = _in_proj(x2d, mod3, lambda i: mod_row_fn(i, tm_in), p["g_pre_mix"], p["w_main"],
                                  p["w_gate"], tm=tm_in, tn=tn_in)
    hps = max(1, min(N_HEADS, SCAN_ROWS_PER_STEP // t_len))
    outs = _scan(proj, gates, p["lb_logits"], p["hg_norm_g"], p["gate_bias"], p["ml_norm_g"], init,
                 bsz=bsz, t_len=t_len, hps=hps)
    hg_act, ml_act = outs[0], outs[1]
    x1, h2 = _mix_out(hg_act, ml_act, mgate, x2d, mod3, lambda i: mod_row_fn(i, tm_mix), p["g_post_mix"],
                      p["g_pre_ffn"], p["w_up_hg"], p["w_up_ml"], p["w_out"], tm=tm_mix)
    y = _ffn(h2, x1, mod3, lambda i: mod_row_fn(i, tm_ffn), p["g_post_ffn"], p["w_ffn_in"], p["w_ffn_out"],
             tm=tm_ffn, tf=tf)
    return y.reshape(bsz, t_len, D_MODEL), outs[2:]


def kernel(x_prompt, x_sample, c, state_hgrn_s, state_mlstm_c, state_mlstm_n, state_mlstm_m, c_ctx, w_mod, b_mod,
           norm_pre_mix, norm_post_mix, norm_pre_ffn, norm_post_ffn, w_in, hgrn_lb_logits, hgrn_norm_g, mlstm_b_i,
           mlstm_b_f, mlstm_norm_g, w_up_hgrn, w_up_mlstm, w_out, w_ffn_in, w_ffn_out):
    bsz_p, t_p, _ = x_prompt.shape
    bsz_s, t_s, _ = x_sample.shape

    cond = jnp.concatenate([c_ctx[None, :], c, jnp.zeros((MOD_ROWS - 1 - bsz_s, D_MODEL), F32)], axis=0)
    mod = _modulation(cond, w_mod[0], b_mod[0][None, :])
    mod3 = mod.reshape(MOD_ROWS, 1, N_MOD)

    w_main, w_gate = _prep_w_in(jnp.transpose(w_in[0]))
    p = dict(
        w_main=w_main, w_gate=w_gate,
        g_pre_mix=norm_pre_mix[0][None, :], g_post_mix=norm_post_mix[0][None, :],
        g_pre_ffn=norm_pre_ffn[0][None, :], g_post_ffn=norm_post_ffn[0][None, :],
        lb_logits=hgrn_lb_logits, hg_norm_g=hgrn_norm_g[0][None, :],
        gate_bias=jnp.tile(jnp.concatenate([mlstm_b_i[0], mlstm_b_f[0]]), LANES // GATE_W)[None, :],
        ml_norm_g=mlstm_norm_g[0][None, :],
        w_up_hg=w_up_hgrn[0].astype(BF16), w_up_ml=w_up_mlstm[0].astype(BF16), w_out=w_out[0].astype(BF16),
        w_ffn_in=w_ffn_in[0].astype(BF16), w_ffn_out=w_ffn_out[0].astype(BF16),
    )
    tiles_p = dict(tm_in=min(1024, bsz_p * t_p), tn_in=1024, tm_mix=min(512, bsz_p * t_p),
                   tm_ffn=min(1024, bsz_p * t_p), tf=512)
    tiles_s = dict(tm_in=min(1024, t_s), tn_in=1024, tm_mix=min(512, t_s), tm_ffn=min(1024, t_s), tf=512)
    y_p, ctx_states = _layer_group(x_prompt, mod3, lambda i, tm: 0, None, p, **tiles_p)
    init = (state_hgrn_s, state_mlstm_c, state_mlstm_n, state_mlstm_m[:, 0])
    y_s, _ = _layer_group(x_sample, mod3, lambda i, tm: 1 + (i * tm) // t_s, init, p, **tiles_s)

    s_fin, c_fin, n_fin, m_fin = ctx_states
    new_hgrn_s = s_fin[:, None]
    new_mlstm_c = c_fin[:, None]
    new_mlstm_n = n_fin[:, None, :, :, 0, :]
    new_mlstm_m = m_fin[:, None, :, :, 0, 0]
    return (y_p, y_s, new_hgrn_s, new_mlstm_c, new_mlstm_n, new_mlstm_m)
```

```python
import functools

import jax
import jax.numpy as jnp
from jax import lax
from jax.experimental import pallas as pl
from jax.experimental.pallas import tpu as pltpu

F32 = jnp.float32
BF16 = jnp.bfloat16

D_MODEL = 2048
N_HEADS = 8
HG_DK = 128
HG_DV = 128
ML_DQK = 64
ML_DV = 128
HEAD_W = 128
D_FF = 5632
CHUNK = 64
EPS = 1e-6
N_MOD = 6 * D_MODEL
LANES = 128
MOD_ROWS = 16
BLK = 2 * CHUNK
SCAN_ROWS_PER_STEP = 2048
BLOCKS_PER_TRIP = 8
MGATE_COLS = 256
FFN_ACC_COLS = 512
GATE_W = 4 * N_HEADS

HEAD_SECTIONS = 8
SEC_GA = N_HEADS * HEAD_SECTIONS
PROJ_W = 96 * LANES
PLAIN_W = SEC_GA * LANES

W_MQ = 5 * N_HEADS * HG_DK
W_MK = W_MQ + N_HEADS * ML_DQK
W_MV = W_MK + N_HEADS * ML_DQK
W_GATES = W_MV + N_HEADS * ML_DV
IN_W = W_GATES + 4 * N_HEADS + N_HEADS * ML_DV + 2 * D_MODEL

VMEM_LIMIT = 56 * 1024 * 1024
VMEM_LIMIT_FFN = 60 * 1024 * 1024


def _dot(a, b):
    return jnp.dot(a, b, preferred_element_type=F32)


def _dot_nt(a, b):
    return lax.dot_general(a, b, (((1,), (1,)), ((), ())), preferred_element_type=F32)


def _dot_tn(a, b):
    return lax.dot_general(a, b, (((0,), (0,)), ((), ())), preferred_element_type=F32)


def _sigmoid_pair(z):
    t = 0.5 * jnp.tanh(0.5 * z)
    return 0.5 + t, 0.5 - t


def _sigmoid(z):
    return 0.5 * jnp.tanh(0.5 * z) + 0.5


def _silu(z):
    return z * _sigmoid(z)


def _log_sigmoid(z):
    return jnp.minimum(z, 0.0) - jnp.log1p(jnp.exp(-jnp.abs(z)))


def _mod_kernel(c_ref, w_ref, b_ref, o_ref):
    a = _silu(c_ref[...]).astype(BF16)
    o_ref[...] = _dot(a, w_ref[...].astype(BF16)) + b_ref[...]


def _modulation(cond, w_mod, b_mod):
    tn = 1024
    return pl.pallas_call(
        _mod_kernel,
        grid=(N_MOD // tn,),
        in_specs=[pl.BlockSpec((MOD_ROWS, D_MODEL), lambda j: (0, 0)),
                  pl.BlockSpec((D_MODEL, tn), lambda j: (0, j)),
                  pl.BlockSpec((1, tn), lambda j: (0, j))],
        out_specs=pl.BlockSpec((MOD_ROWS, tn), lambda j: (0, j)),
        out_shape=jax.ShapeDtypeStruct((MOD_ROWS, N_MOD), F32),
        compiler_params=pltpu.CompilerParams(dimension_semantics=("arbitrary",),
                                             vmem_limit_bytes=VMEM_LIMIT),
        name="modulation",
    )(cond, w_mod, b_mod)


def _rms_rows(x, g):
    return x * lax.rsqrt(jnp.mean(x * x, axis=-1, keepdims=True) + EPS) * g


def _inproj_kernel(x_ref, sh_ref, sc_ref, g_ref, w_ref, wg_ref, proj_ref, mgate_ref, gates_ref, xn_ref, *,
                   tm, n_plain):
    rows = 128
    j = pl.program_id(1)

    @pl.when(j == 0)
    def _():
        for r0 in range(0, tm, rows):
            x = x_ref[r0:r0 + rows, :]
            h = _rms_rows(x, g_ref[...]) * (1.0 + sc_ref[0]) + sh_ref[0]
            hb = h.astype(BF16)
            xn_ref[r0:r0 + rows, :] = hb
            gates_ref[r0:r0 + rows, :] = _dot_nt(hb, wg_ref[...])

    @pl.when(j < n_plain)
    def _():
        proj_ref[...] = _dot_nt(xn_ref[...], w_ref[...])

    @pl.when(j >= n_plain)
    def _():
        for c0 in range(0, w_ref.shape[0], MGATE_COLS):
            z = _dot_nt(xn_ref[...], w_ref[c0:c0 + MGATE_COLS, :])
            mgate_ref[:, c0:c0 + MGATE_COLS] = _sigmoid(z).astype(BF16)


def _in_proj(x2d, mod3, mod_row_fn, g_pre, w_main_t, w_gate_t, *, tm, tn):
    m = x2d.shape[0]
    n_plain = PLAIN_W // tn
    kern = functools.partial(_inproj_kernel, tm=tm, n_plain=n_plain)
    return pl.pallas_call(
        kern,
        grid=(m // tm, PROJ_W // tn),
        in_specs=[pl.BlockSpec((tm, D_MODEL), lambda i, j: (i, 0)),
                  pl.BlockSpec((1, 1, D_MODEL), lambda i, j: (mod_row_fn(i), 0, 0)),
                  pl.BlockSpec((1, 1, D_MODEL), lambda i, j: (mod_row_fn(i), 0, 1)),
                  pl.BlockSpec((1, D_MODEL), lambda i, j: (0, 0)),
                  pl.BlockSpec((tn, D_MODEL), lambda i, j: (j, 0)),
                  pl.BlockSpec((LANES, D_MODEL), lambda i, j: (0, 0))],
        out_specs=[pl.BlockSpec((tm, tn), lambda i, j: (i, jnp.minimum(j, n_plain - 1))),
                   pl.BlockSpec((tm, tn), lambda i, j: (i, jnp.maximum(j - n_plain, 0))),
                   pl.BlockSpec((tm, LANES), lambda i, j: (i, 0))],
        out_shape=[jax.ShapeDtypeStruct((m, PLAIN_W), F32),
                   jax.ShapeDtypeStruct((m, PROJ_W - PLAIN_W), BF16),
                   jax.ShapeDtypeStruct((m, LANES), F32)],
        scratch_shapes=[pltpu.VMEM((tm, D_MODEL), BF16)],
        compiler_params=pltpu.CompilerParams(dimension_semantics=("arbitrary", "arbitrary"),
                                             vmem_limit_bytes=VMEM_LIMIT),
        name="in_proj",
    )(x2d, mod3, mod3, g_pre, w_main_t, w_gate_t)


def _chunk_cumsum(x, pos, rev):
    n = x.shape[0]
    s = 1
    while s < CHUNK:
        if rev:
            x = x + jnp.where(pos < CHUNK - s, pltpu.roll(x, n - s, axis=0), 0.0)
        else:
            x = x + jnp.where(pos >= s, pltpu.roll(x, s, axis=0), 0.0)
        s *= 2
    return x


def _per_chunk(x):
    return x.reshape(BLK // CHUNK, CHUNK, x.shape[-1])


N_SCAN_IN = 6


def _scan_kernel(*refs, t_len, has_init, hps):
    n_io = N_SCAN_IN + (4 if has_init else 0) + 2 + (0 if has_init else 4)
    lead = pl.program_id(1) * hps
    for hh in range(hps):
        hcol = slice(hh * HEAD_W, (hh + 1) * HEAD_W)
        head = slice(hh, hh + 1)
        sec0 = hh * HEAD_SECTIONS * HEAD_W
        views = [refs[0].at[:, sec0 + s * HEAD_W:sec0 + (s + 1) * HEAD_W] for s in range(HEAD_SECTIONS)]
        views += [refs[1], refs[2].at[:, :, hcol], refs[3], refs[4], refs[5].at[:, hcol]]
        pos = N_SCAN_IN
        if has_init:
            views += [r.at[:, :, :, head] for r in refs[pos:pos + 3]] + [refs[pos + 3]]
            pos += 4
        views += [r.at[:, hcol] for r in refs[pos:pos + 2]]
        pos += 2
        if not has_init:
            views += [r.at[:, :, head] for r in refs[pos:pos + 4]]
        views += [r.at[hh] for r in refs[n_io:-1]] + [refs[-1]]
        _scan_head(views, lead + hh, pack_gates=hh == 0, t_len=t_len, has_init=has_init)


def _scan_head(refs, h_idx, *, pack_gates, t_len, has_init):
    (hq_ref, hff_ref, hfb_ref, hi_ref, hgate_ref, mqk_ref, mv_ref, mo_ref, gates_ref,
     lbl_ref, hgn_ref, gbias_ref, mln_ref) = refs[:13]
    pos = 13
    if has_init:
        s0_ref, c0_ref, n0_ref, m0_ref = refs[pos:pos + 4]
        pos += 4
    hg_out_ref, ml_out_ref = refs[pos:pos + 2]
    pos += 2
    if not has_init:
        sfin_ref, cfin_ref, nfin_ref, mfin_ref = refs[pos:pos + 4]
        pos += 4
    (qa_ref, ka_ref, qx_ref, vt_ref, ds_ref, dec_ref, stt_ref,
     sel_ref, g_ref, ut_ref, cloc_ref, nloc_ref, a_ref, mloc_ref, mst_ref, sold_ref, sloc_ref,
     cst_ref, nst_ref, hsum_ref, pk_ref) = refs[pos:]

    b_idx = pl.program_id(0)
    n_chunks = t_len // CHUNK
    n_blocks = t_len // BLK
    cpb = BLK // CHUNK
    mid = CHUNK // 2

    row = lax.broadcasted_iota(jnp.int32, (BLK, BLK), 0)
    col = lax.broadcasted_iota(jnp.int32, (BLK, BLK), 1)
    same_chunk = jnp.right_shift(row, 6) == jnp.right_shift(col, 6)
    masks = (same_chunk & (col <= row), same_chunk & (col >= row))
    in_pos = jnp.bitwise_and(row, CHUNK - 1)
    row_lo = row < CHUNK
    lane_lo = col < ML_DQK
    lane = lax.broadcasted_iota(jnp.int32, (1, LANES), 1)

    def first_chunk_only(x):
        return jnp.concatenate([x[:CHUNK], jnp.zeros_like(x[CHUNK:])], axis=0).astype(BF16)

    def second_chunk_only(x):
        return jnp.concatenate([jnp.zeros_like(x[:CHUNK]), x[CHUNK:]], axis=0).astype(BF16)

    logits = lbl_ref[...]
    l0, l1 = logits[:, 0, :], logits[:, 1, :]
    mx = jnp.maximum(l0, l1)
    e0, e1 = jnp.exp(l0 - mx), jnp.exp(l1 - mx)
    lb_all = e0 / (e0 + e1)

    is_f = jnp.bitwise_and(lane, GATE_W - 1) >= 2 * N_HEADS
    piece = jnp.right_shift(lane, 5)
    grp = min(BLOCKS_PER_TRIP, n_blocks)
    seq_unroll = n_chunks <= BLOCKS_PER_TRIP
    sel_rows = grp * BLK

    if pack_gates:
        @pl.when(h_idx == 0)
        def _():
            def ml_pack(i, carry):
                rows = pl.ds(pl.multiple_of(i * sel_rows, sel_rows), sel_rows)
                xg = gates_ref[rows, :] + gbias_ref[...]
                xg = jnp.where(is_f, _log_sigmoid(xg), xg)
                hi = xg.astype(BF16).astype(F32)
                rest = xg - hi
                mid = rest.astype(BF16).astype(F32)
                packed = jnp.where(piece == 0, hi, jnp.where(piece == 1, mid, rest - mid))
                pk_ref[rows, :] = packed.astype(BF16)
                return carry
            lax.fori_loop(0, t_len // sel_rows, ml_pack, 0)


    def hg_prep(i, carry):
        rows = pl.ds(pl.multiple_of(i * BLK, BLK), BLK)
        q = _silu(hq_ref[rows, :])
        vt = jnp.transpose(hi_ref[rows, :]).astype(BF16)
        vt_ref[i] = vt
        kd_cols = []
        for d, zf_ref in enumerate((hff_ref, hfb_ref)):
            rev = d == 1
            lb = lb_all[d:d + 1]
            oml = 1.0 - lb
            sp, sn = _sigmoid_pair(zf_ref[rows, :])
            k = oml * sn
            b = _chunk_cumsum(jnp.log2(lb + oml * sp), in_pos, rev)
            b3, q3, k3 = _per_chunk(b), _per_chunk(q), _per_chunk(k)
            i_mid, i_last = (CHUNK - 1 - mid, 0) if rev else (mid, CHUNK - 1)
            b_mid, b_last = b3[:, i_mid:i_mid + 1], b3[:, i_last:i_last + 1]
            qa_ref[d, rows, :] = (q3 * jnp.exp2(b3 - b_mid)).reshape(BLK, HG_DK).astype(BF16)
            ka_ref[d, rows, :] = (k3 * jnp.exp2(b_mid - b3)).reshape(BLK, HG_DK).astype(BF16)
            qi = q * jnp.exp2(b)
            qx_ref[d, rows, :HG_DK] = first_chunk_only(qi)
            qx_ref[d, rows, HG_DK:] = second_chunk_only(qi)
            kd = (k3 * jnp.exp2(b_last - b3)).reshape(BLK, HG_DK)
            kd_cols += [first_chunk_only(kd), second_chunk_only(kd)]
            dec = jnp.exp2(b_last)
            for c in range(cpb):
                dec_ref[d, i * cpb + c] = dec[c]
        ds_t = _dot(vt, jnp.concatenate(kd_cols, axis=1))
        for d in range(2):
            for c in range(cpb):
                j = d * cpb + c
                ds_ref[d, i * cpb + c] = ds_t[:, j * HG_DK:(j + 1) * HG_DK]
        return carry

    if has_init:
        st0 = tuple(jnp.transpose(s0_ref[0, 0, d, 0]) for d in range(2))
    else:
        st0 = (jnp.zeros((HG_DV, HG_DK), F32),) * 2

    def hg_state(n, carry):
        new = []
        for d in range(2):
            cn = (n_chunks - 1 - n) if d == 1 else n
            stt_ref[d, cn] = carry[d].astype(BF16)
            new.append(dec_ref[d, cn] * carry[d] + ds_ref[d, cn])
        return tuple(new)

    def hg_out(i, carry):
        rows = pl.ds(pl.multiple_of(i * BLK, BLK), BLK)
        vt = vt_ref[i]
        lhs, rhs_t = [], []
        for d in range(2):
            att = jnp.where(masks[d], _dot_nt(qa_ref[d, rows, :], ka_ref[d, rows, :]), 0.0)
            lhs += [att.astype(BF16), qx_ref[d, rows, :]]
            rhs_t += [vt] + [stt_ref[d, i * cpb + c] for c in range(cpb)]
        o = _dot_nt(jnp.concatenate(lhs, axis=1), jnp.concatenate(rhs_t, axis=1))
        o = (o * lax.rsqrt(jnp.mean(o * o, axis=-1, keepdims=True) + EPS) * hgn_ref[...]
             * _silu(hgate_ref[rows, :]))
        hg_out_ref[rows, :] = o.astype(BF16)
        return carry


    r_i = lax.broadcasted_iota(jnp.int32, (LANES, 4 * LANES), 0)
    c_i = lax.broadcasted_iota(jnp.int32, (LANES, 4 * LANES), 1)
    cblk = jnp.right_shift(c_i, 7)
    target = h_idx + jnp.where(cblk == 0, 0, jnp.where(cblk == 1, 2 * N_HEADS,
                                                       jnp.where(cblk == 2, N_HEADS, 3 * N_HEADS)))
    onehot = jnp.where((jnp.bitwise_and(r_i, GATE_W - 1) == target) & (r_i < 3 * GATE_W), 1.0, 0.0).astype(BF16)

    def ml_select(i, carry):
        rows = pl.ds(pl.multiple_of(i * sel_rows, sel_rows), sel_rows)
        sel_ref[rows, :] = _dot(pk_ref[rows, :], onehot)
        return carry

    def ml_prep(i, carry):
        rows = pl.ds(pl.multiple_of(i * BLK, BLK), BLK)
        qk = mqk_ref[rows, :]
        kk = jnp.where(lane_lo, pltpu.roll(qk, ML_DQK, axis=1), qk) * (ML_DQK ** -0.5)
        v = mv_ref[rows, :]
        e_dir = []
        for d in range(2):
            rev = d == 1
            li = sel_ref[rows, (2 * d) * LANES:(2 * d + 1) * LANES]
            lf = sel_ref[rows, (2 * d + 1) * LANES:(2 * d + 2) * LANES]
            g = _chunk_cumsum(lf, in_pos, rev)
            u = li - g
            g_ref[d, rows, :] = g
            ut_ref[d, rows, :] = jnp.transpose(u)
            i_last = 0 if rev else CHUNK - 1
            g_last = _per_chunk(g)[:, i_last:i_last + 1]
            w_end = g_last + _per_chunk(u)
            m_loc = jnp.max(w_end, axis=1, keepdims=True)
            e_dir.append(jnp.exp(w_end - m_loc).reshape(BLK, LANES))
            for c in range(cpb):
                a_ref[d, i * cpb + c] = g_last[c]
                mloc_ref[d, i * cpb + c] = m_loc[c]
        wk = jnp.where(lane_lo, e_dir[0], e_dir[1]) * kk
        v_cols = jnp.concatenate([first_chunk_only(v), second_chunk_only(v)], axis=1)
        c_loc = _dot_tn(wk.astype(BF16), v_cols)
        n_loc = jnp.sum(_per_chunk(wk), axis=1, keepdims=True)
        for c in range(cpb):
            nloc_ref[i * cpb + c] = n_loc[c]
            for d in range(2):
                cloc_ref[d, i * cpb + c] = c_loc[d * ML_DQK:(d + 1) * ML_DQK, c * ML_DV:(c + 1) * ML_DV]
        return carry

    def sel_trip(t, carry):
        ml_select(t, carry)
        for u in range(grp):
            ml_prep(t * grp + u, carry)
        return carry
    lax.fori_loop(0, n_blocks // grp, sel_trip, 0)

    if has_init:
        m0 = tuple(jnp.full((1, LANES), m0_ref[b_idx, d * N_HEADS + h_idx], F32) for d in range(2))
    else:
        m0 = (jnp.zeros((1, LANES), F32),) * 2

    def ml_stab(n, carry):
        new = []
        for d in range(2):
            cn = (n_chunks - 1 - n) if d == 1 else n
            m_st = carry[d]
            a, m_loc = a_ref[d, cn], mloc_ref[d, cn]
            m_new = jnp.maximum(a + m_st, m_loc)
            mst_ref[d, cn] = m_st
            sold_ref[d, cn] = jnp.exp(a + m_st - m_new)
            sloc_ref[d, cn] = jnp.exp(m_loc - m_new)
            new.append(m_new)
        return tuple(new)
    m_fin = lax.fori_loop(0, n_chunks, ml_stab, m0, unroll=seq_unroll)

    if has_init:
        c0 = tuple(c0_ref[0, 0, d, 0] for d in range(2))
        n0 = tuple(jnp.concatenate([n0_ref[0, 0, d, 0]] * 2, axis=1) for d in range(2))
    else:
        c0 = (jnp.zeros((ML_DQK, ML_DV), F32),) * 2
        n0 = (jnp.zeros((1, LANES), F32),) * 2

    def ml_state(n, carry):
        cs, ns = carry
        new_c, new_n = [], []
        for d in range(2):
            cn = (n_chunks - 1 - n) if d == 1 else n
            cst_ref[d, cn] = cs[d].astype(BF16)
            nst_ref[d, cn] = ns[d]
            s_old, s_loc = sold_ref[d, cn], sloc_ref[d, cn]
            new_c.append(s_old * cs[d] + s_loc * cloc_ref[d, cn])
            new_n.append(s_old * ns[d] + s_loc * nloc_ref[cn])
        return tuple(new_c), tuple(new_n)
    c_fin, n_fin = lax.fori_loop(0, n_chunks, ml_state, (c0, n0), unroll=seq_unroll)

    def ml_out(i, carry):
        rows = pl.ds(pl.multiple_of(i * BLK, BLK), BLK)
        chunks = pl.ds(i * cpb, cpb)
        qk = mqk_ref[rows, :]
        qb = qk[:, :ML_DQK].astype(BF16)
        kb = (qk[:, ML_DQK:] * (ML_DQK ** -0.5)).astype(BF16)
        vb = mv_ref[rows, :].astype(BF16)
        qk_raw = _dot_nt(qb, kb)
        qq = jnp.where(lane_lo, qk, pltpu.roll(qk, ML_DQK, axis=1))
        q_cols = jnp.where(lane_lo == row_lo, qq, 0.0).astype(BF16)
        c_rows = jnp.concatenate([cst_ref[d, chunks].reshape(BLK, ML_DV) for d in range(2)], axis=1)
        qc = _dot(q_cols, c_rows)
        parts = []
        for d in range(2):
            g = g_ref[d, rows, :]
            log_inter = (_per_chunk(g) + mst_ref[d, chunks]).reshape(BLK, LANES)
            log_d = jnp.where(masks[d], g + ut_ref[d, rows, :], -jnp.inf)
            m_t = jnp.maximum(log_inter, jnp.max(log_d, axis=-1, keepdims=True))
            s_inter = jnp.exp(log_inter - m_t)
            s_qk = qk_raw * jnp.exp(log_d - m_t)
            own_half = lane_lo if d == 0 else jnp.logical_not(lane_lo)
            q_n = jnp.where(own_half, (_per_chunk(qq) * nst_ref[d, chunks]).reshape(BLK, LANES), 0.0)
            den = jnp.sum(s_qk + s_inter * q_n, axis=-1, keepdims=True)
            parts.append((s_qk, s_inter, jnp.maximum(jnp.abs(den), jnp.exp(-m_t[:, 0:1]))))
        nv = _dot(jnp.concatenate([parts[0][0], parts[1][0]], axis=0).astype(BF16), vb)
        o = None
        for d in range(2):
            _, s_inter, den = parts[d]
            h = (nv[d * BLK:(d + 1) * BLK] + s_inter * qc[:, d * ML_DV:(d + 1) * ML_DV]) / den
            o = h if o is None else o + h
        hsum_ref[rows, :] = o
        return carry

    def ml_norm(i, carry):
        rows = pl.ds(pl.multiple_of(i * BLK, BLK), BLK)
        o = hsum_ref[rows, :]
        mu = jnp.mean(o, axis=-1, keepdims=True)
        oc = o - mu
        var = jnp.mean(oc * oc, axis=-1, keepdims=True)
        o = oc * lax.rsqrt(var + EPS) * mln_ref[...] * _sigmoid(mo_ref[rows, :])
        ml_out_ref[rows, :] = o.astype(BF16)
        return carry
    pair = min(2, n_blocks)

    def out_trip(t, carry):
        for u in range(pair):
            ml_out(t * pair + u, carry)
            hg_prep(t * pair + u, carry)
        return carry
    lax.fori_loop(0, n_blocks // pair, out_trip, 0)

    st_fin = lax.fori_loop(0, n_chunks, hg_state, st0, unroll=seq_unroll)

    def fin_trip(t, carry):
        for u in range(grp):
            hg_out(t * grp + u, carry)
            ml_norm(t * grp + u, carry)
        return carry
    lax.fori_loop(0, n_blocks // grp, fin_trip, 0)

    if not has_init:
        for d in range(2):
            sfin_ref[0, d, 0] = jnp.transpose(st_fin[d])
            cfin_ref[0, d, 0] = c_fin[d]
            nfin_ref[0, d, 0] = n_fin[d][:, d * ML_DQK:(d + 1) * ML_DQK]
            mfin_ref[0, d, 0] = m_fin[d]


def _scan(proj, gates, lb_logits, hg_norm_g, gate_bias, ml_norm_g, init, *, bsz, t_len, hps):
    has_init = init is not None
    m = bsz * t_len
    n_chunks = t_len // CHUNK
    n_blocks = t_len // BLK
    wide = hps * HEAD_W

    in_specs = [pl.BlockSpec((t_len, HEAD_SECTIONS * wide), lambda b, h: (b, h)),
                pl.BlockSpec((t_len, LANES), lambda b, h: (b, 0)),
                pl.BlockSpec((2, 2, wide), lambda b, h: (0, 0, h)),
                pl.BlockSpec((1, HG_DV), lambda b, h: (0, 0)),
                pl.BlockSpec((1, LANES), lambda b, h: (0, 0)),
                pl.BlockSpec((1, wide), lambda b, h: (0, h))]
    args = [proj, gates, lb_logits, hg_norm_g, gate_bias, ml_norm_g]
    out_specs = [pl.BlockSpec((t_len, wide), lambda b, h: (b, h)),
                 pl.BlockSpec((t_len, wide), lambda b, h: (b, h))]
    out_shape = [jax.ShapeDtypeStruct((m, N_HEADS * HG_DV), BF16),
                 jax.ShapeDtypeStruct((m, N_HEADS * ML_DV), BF16)]
    if has_init:
        s0, c0, n0, m0 = init
        in_specs += [pl.BlockSpec((1, 1, 2, hps, HG_DK, HG_DV), lambda b, h: (b, 0, 0, h, 0, 0)),
                     pl.BlockSpec((1, 1, 2, hps, ML_DQK, ML_DV), lambda b, h: (b, 0, 0, h, 0, 0)),
                     pl.BlockSpec((1, 1, 2, hps, 1, ML_DQK), lambda b, h: (b, 0, 0, h, 0, 0)),
                     pl.BlockSpec(memory_space=pltpu.SMEM)]
        args += [s0, c0, n0.reshape(n0.shape[:4] + (1, ML_DQK)), m0.reshape(bsz, 2 * N_HEADS)]
    else:
        out_specs += [pl.BlockSpec((1, 2, hps, HG_DK, HG_DV), lambda b, h: (b, 0, h, 0, 0)),
                      pl.BlockSpec((1, 2, hps, ML_DQK, ML_DV), lambda b, h: (b, 0, h, 0, 0)),
                      pl.BlockSpec((1, 2, hps, 1, ML_DQK), lambda b, h: (b, 0, h, 0, 0)),
                      pl.BlockSpec((1, 2, hps, 1, LANES), lambda b, h: (b, 0, h, 0, 0))]
        out_shape += [jax.ShapeDtypeStruct((bsz, 2, N_HEADS, HG_DK, HG_DV), F32),
                      jax.ShapeDtypeStruct((bsz, 2, N_HEADS, ML_DQK, ML_DV), F32),
                      jax.ShapeDtypeStruct((bsz, 2, N_HEADS, 1, ML_DQK), F32),
                      jax.ShapeDtypeStruct((bsz, 2, N_HEADS, 1, LANES), F32)]

    def per_head(shape, dtype):
        return pltpu.VMEM((hps,) + shape, dtype)

    def per_chunk_row(*lead):
        return per_head(lead + (n_chunks, 1, LANES), F32)

    scratch = [per_head((2, t_len, HG_DK), BF16),
               per_head((2, t_len, HG_DK), BF16),
               per_head((2, t_len, 2 * HG_DK), BF16),
               per_head((n_blocks, HG_DV, BLK), BF16),
               per_head((2, n_chunks, HG_DV, HG_DK), F32),
               per_chunk_row(2),
               per_head((2, n_chunks, HG_DV, HG_DK), BF16),
               per_head((t_len, 4 * LANES), F32),
               per_head((2, t_len, LANES), F32),
               per_head((2, t_len, BLK), F32),
               per_head((2, n_chunks, ML_DQK, ML_DV), F32),
               per_chunk_row(),
               per_chunk_row(2),
               per_chunk_row(2),
               per_chunk_row(2),
               per_chunk_row(2),
               per_chunk_row(2),
               per_head((2, n_chunks, ML_DQK, ML_DV), BF16),
               per_chunk_row(2),
               per_head((t_len, ML_DV), F32),
               pltpu.VMEM((t_len, LANES), BF16)]
    kern = functools.partial(_scan_kernel, t_len=t_len, has_init=has_init, hps=hps)
    return pl.pallas_call(
        kern,
        grid=(bsz, N_HEADS // hps),
        in_specs=in_specs,
        out_specs=out_specs,
        out_shape=out_shape,
        scratch_shapes=scratch,
        compiler_params=pltpu.CompilerParams(dimension_semantics=("arbitrary", "arbitrary"),
                                             vmem_limit_bytes=VMEM_LIMIT),
        name="scan_init" if has_init else "scan_ctx",
    )(*args)


def _mix_out_kernel(hg_ref, ml_ref, ga_ref, gb_ref, x_ref, gt_ref, sh_ref, sc_ref, gpost_ref, gpre_ref,
                    wuh_ref, wum_ref, wo_ref, x1_ref, h2_ref):
    y_hg = _dot(hg_ref[...], wuh_ref[...])
    y_ml = _dot(ml_ref[...], wum_ref[...])
    merged = ga_ref[...].astype(F32) * y_hg + gb_ref[...].astype(F32) * y_ml
    y = _dot(merged.astype(BF16), wo_ref[...])
    x1 = x_ref[...] + gt_ref[0] * _rms_rows(y, gpost_ref[...])
    x1_ref[...] = x1
    h2 = _rms_rows(x1, gpre_ref[...]) * (1.0 + sc_ref[0]) + sh_ref[0]
    h2_ref[...] = h2.astype(BF16)


def _mix_out(hg_act, ml_act, mgate, x2d, mod3, mod_row_fn, g_post_mix, g_pre_ffn, w_up_hg, w_up_ml, w_out, *, tm):
    m = x2d.shape[0]

    def resident(shape):
        return pl.BlockSpec(shape, lambda i: (0, 0), pipeline_mode=pl.Buffered(1))

    return pl.pallas_call(
        _mix_out_kernel,
        grid=(m // tm,),
        in_specs=[pl.BlockSpec((tm, N_HEADS * HG_DV), lambda i: (i, 0)),
                  pl.BlockSpec((tm, N_HEADS * ML_DV), lambda i: (i, 0)),
                  pl.BlockSpec((tm, D_MODEL), lambda i: (i, 0)),
                  pl.BlockSpec((tm, D_MODEL), lambda i: (i, 1)),
                  pl.BlockSpec((tm, D_MODEL), lambda i: (i, 0)),
                  pl.BlockSpec((1, 1, D_MODEL), lambda i: (mod_row_fn(i), 0, 2)),
                  pl.BlockSpec((1, 1, D_MODEL), lambda i: (mod_row_fn(i), 0, 3)),
                  pl.BlockSpec((1, 1, D_MODEL), lambda i: (mod_row_fn(i), 0, 4)),
                  resident((1, D_MODEL)),
                  resident((1, D_MODEL)),
                  resident((N_HEADS * HG_DV, D_MODEL)),
                  resident((N_HEADS * ML_DV, D_MODEL)),
                  resident((D_MODEL, D_MODEL))],
        out_specs=[pl.BlockSpec((tm, D_MODEL), lambda i: (i, 0)),
                   pl.BlockSpec((tm, D_MODEL), lambda i: (i, 0))],
        out_shape=[jax.ShapeDtypeStruct((m, D_MODEL), F32),
                   jax.ShapeDtypeStruct((m, D_MODEL), BF16)],
        compiler_params=pltpu.CompilerParams(dimension_semantics=("arbitrary",),
                                             vmem_limit_bytes=VMEM_LIMIT),
        name="mix_out",
    )(hg_act, ml_act, mgate, mgate, x2d, mod3, mod3, mod3, g_post_mix, g_pre_ffn, w_up_hg, w_up_ml, w_out)


def _ffn_kernel(h2_ref, x1_ref, gt_ref, gpost_ref, wa_ref, wb_ref, wo_ref, out_ref):
    j = pl.program_id(1)

    @pl.when(j == 0)
    def _():
        out_ref[...] = jnp.zeros_like(out_ref)

    h2 = h2_ref[...]
    hid = (_silu(_dot(h2, wa_ref[...])) * _dot(h2, wb_ref[...])).astype(BF16)
    for n0 in range(0, D_MODEL, FFN_ACC_COLS):
        out_ref[:, n0:n0 + FFN_ACC_COLS] += _dot(hid, wo_ref[:, n0:n0 + FFN_ACC_COLS])

    @pl.when(j == pl.num_programs(1) - 1)
    def _():
        rows = 128

        def body(i, carry):
            r = pl.ds(pl.multiple_of(i * rows, rows), rows)
            out_ref[r, :] = x1_ref[r, :] + gt_ref[0] * _rms_rows(out_ref[r, :], gpost_ref[...])
            return carry
        lax.fori_loop(0, out_ref.shape[0] // rows, body, 0)


def _ffn(h2, x1, mod3, mod_row_fn, g_post_ffn, w_ffn_in, w_ffn_out, *, tm, tf):
    m = x1.shape[0]
    nf = D_FF // tf
    return pl.pallas_call(
        _ffn_kernel,
        grid=(m // tm, nf),
        in_specs=[pl.BlockSpec((tm, D_MODEL), lambda i, j: (i, 0)),
                  pl.BlockSpec((tm, D_MODEL), lambda i, j: (i, 0)),
                  pl.BlockSpec((1, 1, D_MODEL), lambda i, j: (mod_row_fn(i), 0, 5)),
                  pl.BlockSpec((1, D_MODEL), lambda i, j: (0, 0)),
                  pl.BlockSpec((D_MODEL, tf), lambda i, j: (0, j)),
                  pl.BlockSpec((D_MODEL, tf), lambda i, j, nf=nf: (0, j + nf)),
                  pl.BlockSpec((tf, D_MODEL), lambda i, j: (j, 0))],
        out_specs=pl.BlockSpec((tm, D_MODEL), lambda i, j: (i, 0)),
        out_shape=jax.ShapeDtypeStruct((m, D_MODEL), F32),
        compiler_params=pltpu.CompilerParams(dimension_semantics=("arbitrary", "arbitrary"),
                                             vmem_limit_bytes=VMEM_LIMIT_FFN),
        name="ffn",
    )(h2, x1, mod3, g_post_ffn, w_ffn_in, w_ffn_in, w_ffn_out)


def _w_prep_kernel(wt_ref, main_ref, gate_ref):
    x = wt_ref[...]
    w_mo = W_GATES + GATE_W
    pieces = []
    for h in range(N_HEADS):
        pieces += [x[s * N_HEADS * HG_DK + h * HG_DK:s * N_HEADS * HG_DK + (h + 1) * HG_DK] for s in range(5)]
        pieces += [x[W_MQ + h * ML_DQK:W_MQ + (h + 1) * ML_DQK], x[W_MK + h * ML_DQK:W_MK + (h + 1) * ML_DQK],
                   x[W_MV + h * ML_DV:W_MV + (h + 1) * ML_DV], x[w_mo + h * ML_DV:w_mo + (h + 1) * ML_DV]]
    pieces.append(x[w_mo + N_HEADS * ML_DV:])
    main_ref[...] = jnp.concatenate(pieces, axis=0).astype(BF16)
    gate_ref[...] = jnp.concatenate([x[W_GATES:W_GATES + GATE_W]] * (LANES // GATE_W), axis=0).astype(BF16)


def _prep_w_in(w_t):
    cols = 256
    return pl.pallas_call(
        _w_prep_kernel,
        grid=(D_MODEL // cols,),
        in_specs=[pl.BlockSpec((IN_W, cols), lambda i: (0, i))],
        out_specs=[pl.BlockSpec((PROJ_W, cols), lambda i: (0, i)),
                   pl.BlockSpec((LANES, cols), lambda i: (0, i))],
        out_shape=[jax.ShapeDtypeStruct((PROJ_W, D_MODEL), BF16),
                   jax.ShapeDtypeStruct((LANES, D_MODEL), BF16)],
        compiler_params=pltpu.CompilerParams(dimension_semantics=("arbitrary",),
                                             vmem_limit_bytes=VMEM_LIMIT),
        name="w_prep",
    )(w_t)


def _layer_group(x, mod3, mod_row_fn, init, p, *, tm_in, tn_in, tm_mix, tm_ffn, tf):
    bsz, t_len, _ = x.shape
    x2d = x.reshape(bsz * t_len, D_MODEL)
    proj, mgate, gates = _in_proj(x2d, mod3, lambda i: mod_row_fn(i, tm_in), p["g_pre_mix"], p["w_main"],
                                  p["w_gate"], tm=tm_in, tn=tn_in)
    hps = max(1, min(N_HEADS, SCAN_ROWS_PER_STEP // t_len))
    outs = _scan(proj, gates, p["lb_logits"], p["hg_norm_g"], p["gate_bias"], p["ml_norm_g"], init,
                 bsz=bsz, t_len=t_len, hps=hps)
    hg_act, ml_act = outs[0], outs[1]
    x1, h2 = _mix_out(hg_act, ml_act, mgate, x2d, mod3, lambda i: mod_row_fn(i, tm_mix), p["g_post_mix"],
                      p["g_pre_ffn"], p["w_up_hg"], p["w_up_ml"], p["w_out"], tm=tm_mix)
    y = _ffn(h2, x1, mod3, lambda i: mod_row_fn(i, tm_ffn), p["g_post_ffn"], p["w_ffn_in"], p["w_ffn_out"],
             tm=tm_ffn, tf=tf)
    return y.reshape(bsz, t_len, D_MODEL), outs[2:]


def kernel(x_prompt, x_sample, c, state_hgrn_s, state_mlstm_c, state_mlstm_n, state_mlstm_m, c_ctx, w_mod, b_mod,
           norm_pre_mix, norm_post_mix, norm_pre_ffn, norm_post_ffn, w_in, hgrn_lb_logits, hgrn_norm_g, mlstm_b_i,
           mlstm_b_f, mlstm_norm_g, w_up_hgrn, w_up_mlstm, w_out, w_ffn_in, w_ffn_out):
    bsz_p, t_p, _ = x_prompt.shape
    bsz_s, t_s, _ = x_sample.shape

    cond = jnp.concatenate([c_ctx[None, :], c, jnp.zeros((MOD_ROWS - 1 - bsz_s, D_MODEL), F32)], axis=0)
    mod = _modulation(cond, w_mod[0], b_mod[0][None, :])
    mod3 = mod.reshape(MOD_ROWS, 1, N_MOD)

    w_main, w_gate = _prep_w_in(jnp.transpose(w_in[0]))
    p = dict(
        w_main=w_main, w_gate=w_gate,
        g_pre_mix=norm_pre_mix[0][None, :], g_post_mix=norm_post_mix[0][None, :],
        g_pre_ffn=norm_pre_ffn[0][None, :], g_post_ffn=norm_post_ffn[0][None, :],
        lb_logits=hgrn_lb_logits, hg_norm_g=hgrn_norm_g[0][None, :],
        gate_bias=jnp.tile(jnp.concatenate([mlstm_b_i[0], mlstm_b_f[0]]), LANES // GATE_W)[None, :],
        ml_norm_g=mlstm_norm_g[0][None, :],
        w_up_hg=w_up_hgrn[0].astype(BF16), w_up_ml=w_up_mlstm[0].astype(BF16), w_out=w_out[0].astype(BF16),
        w_ffn_in=w_ffn_in[0].astype(BF16), w_ffn_out=w_ffn_out[0].astype(BF16),
    )
    tiles_p = dict(tm_in=min(1024, bsz_p * t_p), tn_in=1024, tm_mix=min(512, bsz_p * t_p),
                   tm_ffn=min(1024, bsz_p * t_p), tf=512)
    tiles_s = dict(tm_in=min(1024, t_s), tn_in=1024, tm_mix=min(512, t_s), tm_ffn=min(1024, t_s), tf=512)
    y_p, ctx_states = _layer_group(x_prompt, mod3, lambda i, tm: 0, None, p, **tiles_p)
    init = (state_hgrn_s, state_mlstm_c, state_mlstm_n, state_mlstm_m[:, 0])
    y_s, _ = _layer_group(x_sample, mod3, lambda i, tm: 1 + (i * tm) // t_s, init, p, **tiles_s)

    s_fin, c_fin, n_fin, m_fin = ctx_states
    new_hgrn_s = s_fin[:, None]
    new_mlstm_c = c_fin[:, None]
    new_mlstm_n = n_fin[:, None, :, :, 0, :]
    new_mlstm_m = m_fin[:, None, :, :, 0, 0]
    return (y_p, y_s, new_hgrn_s, new_mlstm_c, new_mlstm_n, new_mlstm_m)
```
